```python
import math
import jax, jax.numpy as jnp
from jax import lax
import numpy as np

D_MODEL = 1024
BATCH = 4
SEQ = 4096
DEPTH = 1

N_META = 16
BLOCK_Q = 128
RMS_EPS = 1e-6

MLA_HEADS = 16
MLA_Q_RANK = 256
MLA_KV_RANK = 128
MLA_NOPE_DIM = 64
MLA_ROPE_DIM = 32
MLA_V_DIM = 64
MLA_WIDTH = MLA_HEADS * MLA_V_DIM
MLA_SCALE = 1.0 / math.sqrt(MLA_NOPE_DIM + MLA_ROPE_DIM)
ROPE_THETA = 10000.0

FOX_HEADS = 16
FOX_HEAD_DIM = 64
FOX_WIDTH = FOX_HEADS * FOX_HEAD_DIM
FOX_SCALE = 1.0 / math.sqrt(FOX_HEAD_DIM)

IN_SPLITS = (MLA_Q_RANK, MLA_KV_RANK, MLA_ROPE_DIM, MLA_WIDTH,
             FOX_WIDTH, FOX_WIDTH, FOX_WIDTH, FOX_HEADS, FOX_WIDTH,
             D_MODEL, D_MODEL)
IN_WIDTH = sum(IN_SPLITS)

kernel_name = 'hybrid_mla_fox_gated_merge'


def rmsnorm(x, g):
    xf = x.astype(jnp.float32)
    y = xf * lax.rsqrt(jnp.mean(xf * xf, axis=-1, keepdims=True) + RMS_EPS)
    return (y * g.astype(jnp.float32)).astype(x.dtype)


def rope(x, pos):
    half = x.shape[-1] // 2
    inv_freq = ROPE_THETA ** (-jnp.arange(half, dtype=jnp.float32) / half)
    ang = pos.astype(jnp.float32)[:, None] * inv_freq[None, :]
    cos, sin = jnp.cos(ang), jnp.sin(ang)
    x1 = x[..., :half].astype(jnp.float32)
    x2 = x[..., half:].astype(jnp.float32)
    return jnp.concatenate([x1 * cos - x2 * sin, x1 * sin + x2 * cos], axis=-1).astype(x.dtype)


def causal_block_attention(q, k, v, scale, cum=None):
    B, H, L, _ = q.shape
    n_real = L - N_META
    nb = n_real // BLOCK_Q
    key_pos = jnp.arange(L)

    def attend(qb, qpos, cq=None):
        s = jnp.einsum('bhqd,bhkd->bhqk', qb, k, preferred_element_type=jnp.float32) * scale
        if cq is not None:
            s = s + (cq[..., :, None] - cum[:, :, None, :])
        s = jnp.where(key_pos[None, :] <= qpos[:, None], s, -jnp.inf)
        p = jax.nn.softmax(s, axis=-1)
        return jnp.einsum('bhqk,bhkd->bhqd', p.astype(v.dtype), v)

    out_meta = attend(q[:, :, :N_META], key_pos[:N_META],
                      None if cum is None else cum[:, :, :N_META])

    def to_blocks(a):
        a = a[:, :, N_META:]
        a = a.reshape((B, H, nb, BLOCK_Q) + a.shape[3:])
        return jnp.moveaxis(a, 2, 0)

    xs = (to_blocks(q), key_pos[N_META:].reshape(nb, BLOCK_Q))
    if cum is not None:
        xs = xs + (to_blocks(cum),)
    out_real = lax.map(lambda args: attend(*args), xs)
    out_real = jnp.moveaxis(out_real, 0, 2).reshape(B, H, n_real, v.shape[-1])
    return jnp.concatenate([out_meta, out_real], axis=2)


def setup_inputs(seed: int = 0) -> dict:
    key = jax.random.key(seed)
    ks = jax.random.split(key, 13)
    f32 = jnp.float32

    def gain(k, n):
        return 1.0 + 0.1 * jax.random.normal(k, (DEPTH, n), f32)

    def dense(k, fan_in, fan_out):
        return jax.random.normal(k, (DEPTH, fan_in, fan_out), f32) * fan_in ** -0.5

    return {
        'x': jax.random.normal(ks[0], (BATCH, SEQ, D_MODEL), f32),
        'meta_tokens': jax.random.normal(ks[1], (N_META, D_MODEL), f32),
        'pre_norm_g': gain(ks[2], D_MODEL),
        'w_in': dense(ks[3], D_MODEL, IN_WIDTH),
        'fox_forget_b': jax.random.uniform(ks[4], (DEPTH, FOX_HEADS), f32, 1.0, 4.0),
        'mla_q_norm_g': gain(ks[5], MLA_Q_RANK),
        'mla_kv_norm_g': gain(ks[6], MLA_KV_RANK),
        'w_uq': dense(ks[7], MLA_Q_RANK, MLA_HEADS * (MLA_NOPE_DIM + MLA_ROPE_DIM)),
        'w_ukv': dense(ks[8], MLA_KV_RANK, MLA_HEADS * (MLA_NOPE_DIM + MLA_V_DIM)),
        'w_br_mla': dense(ks[9], MLA_WIDTH, D_MODEL),
        'w_br_fox': dense(ks[10], FOX_WIDTH, D_MODEL),
        'w_out': dense(ks[11], D_MODEL, D_MODEL),
        'post_norm_g': gain(ks[12], D_MODEL),
    }


def reference(x, meta_tokens, pre_norm_g, w_in, fox_forget_b, mla_q_norm_g, mla_kv_norm_g,
              w_uq, w_ukv, w_br_mla, w_br_fox, w_out, post_norm_g):
    B, S, D = x.shape
    L = S + N_META
    pos = jnp.arange(L)
    h = jnp.concatenate([jnp.broadcast_to(meta_tokens[None].astype(x.dtype), (B, N_META, D)), x], axis=1)
    split_idx = np.cumsum(IN_SPLITS)[:-1].tolist()

    def fox_heads(t):
        return t.reshape(B, L, FOX_HEADS, FOX_HEAD_DIM).transpose(0, 2, 1, 3)

    for l in range(DEPTH):
        u = rmsnorm(h, pre_norm_g[l])
        proj = u @ w_in[l]
        (cq, ckv, k_pe_raw, z_mla, fq, fk, fv, f_logit, z_fox, gate_a, gate_b) = jnp.split(proj, split_idx, axis=-1)

        q = (rmsnorm(cq, mla_q_norm_g[l]) @ w_uq[l]).reshape(B, L, MLA_HEADS, MLA_NOPE_DIM + MLA_ROPE_DIM).transpose(0, 2, 1, 3)
        kv = (rmsnorm(ckv, mla_kv_norm_g[l]) @ w_ukv[l]).reshape(B, L, MLA_HEADS, MLA_NOPE_DIM + MLA_V_DIM).transpose(0, 2, 1, 3)
        q_nope, q_pe = q[..., :MLA_NOPE_DIM], q[..., MLA_NOPE_DIM:]
        k_nope, v_mla = kv[..., :MLA_NOPE_DIM], kv[..., MLA_NOPE_DIM:]
        k_pe = rope(k_pe_raw, pos)[:, None]
        q_m = jnp.concatenate([q_nope, rope(q_pe, pos)], axis=-1)
        k_m = jnp.concatenate([k_nope, jnp.broadcast_to(k_pe, (B, MLA_HEADS, L, MLA_ROPE_DIM))], axis=-1)
        o_mla = causal_block_attention(q_m, k_m, v_mla, MLA_SCALE)
        o_mla = o_mla.transpose(0, 2, 1, 3).reshape(B, L, MLA_WIDTH)
        y_mla = (o_mla * jax.nn.silu(z_mla)) @ w_br_mla[l]

        log_f = jax.nn.log_sigmoid((f_logit + fox_forget_b[l]).astype(jnp.float32)).transpose(0, 2, 1)
        cum = jnp.cumsum(log_f, axis=-1)
        o_fox = causal_block_attention(fox_heads(fq), fox_heads(fk), fox_heads(fv), FOX_SCALE, cum)
        o_fox = o_fox.transpose(0, 2, 1, 3).reshape(B, L, FOX_WIDTH)
        y_fox = (o_fox * jax.nn.silu(z_fox)) @ w_br_fox[l]

        mixed = (jax.nn.sigmoid(gate_a) * y_mla + jax.nn.sigmoid(gate_b) * y_fox) @ w_out[l]
        h = h + rmsnorm(mixed, post_norm_g[l])

    return h[:, N_META:]
```

```python
import functools
import math

import numpy as np
import jax
import jax.numpy as jnp
from jax import lax
from jax.experimental import pallas as pl
from jax.experimental.pallas import tpu as pltpu

F32 = jnp.float32
BF16 = jnp.bfloat16

N_META = 16
RMS_EPS = 1e-6
HEADS = 16
MLA_Q_RANK = 256
MLA_KV_RANK = 128
MLA_NOPE = 64
MLA_ROPE = 32
MLA_V = 64
MLA_SCALE = 1.0 / math.sqrt(MLA_NOPE + MLA_ROPE)
ROPE_THETA = 10000.0
FOX_DIM = 64
FOX_SCALE = 1.0 / math.sqrt(FOX_DIM)

LANE = 128
HEAD_PAIRS = HEADS // 2
SMALL_W = 5 * LANE
KPE_LANE = 64
ROW_TILE = 512
ATT_TILE = 512
L_ROWS = 16
NEG_INF = -1e30
VMEM_LIMIT = 56 * 1024 * 1024

NT_DIMS = (((1,), (1,)), ((), ()))


def _rms(x, g):
    return x * lax.rsqrt(jnp.mean(x * x, axis=-1, keepdims=True) + RMS_EPS) * g


def _sigmoid(x):
    return 1.0 / (1.0 + jnp.exp(-x))


def _split3(x):
    hi = x.astype(BF16)
    r1 = x - hi.astype(F32)
    mid = r1.astype(BF16)
    lo = (r1 - mid.astype(F32)).astype(BF16)
    return hi, mid, lo


def _const_spec(shape):
    nd = len(shape)
    return pl.BlockSpec(shape, lambda *_: (0,) * nd, pipeline_mode=pl.Buffered(1))


def _in_proj_kernel(x_ref, g_ref, ws_ref, wzm_ref, wfq_ref, wfk_ref, wfvt_ref, wzf_ref, wga_ref, wgb_ref,
                    small_ref, szm_ref, fq_ref, fk_ref, fvt_ref, szf_ref, ga_ref, gb_ref):
    u = _rms(x_ref[...], g_ref[...]).astype(BF16)

    def mm(w_ref):
        return jnp.dot(u, w_ref[...], preferred_element_type=F32)

    small_ref[...] = mm(ws_ref)
    z = mm(wzm_ref)
    szm_ref[...] = (z * _sigmoid(z)).astype(BF16)
    fq_ref[...] = (mm(wfq_ref) * FOX_SCALE).astype(BF16)
    fk_ref[...] = mm(wfk_ref).astype(BF16)
    vt = lax.dot_general(wfvt_ref[...], u, NT_DIMS, preferred_element_type=F32)
    for hp in range(HEAD_PAIRS):
        fvt_ref[hp] = vt[hp * LANE:(hp + 1) * LANE, :].astype(BF16)
    z = mm(wzf_ref)
    szf_ref[...] = (z * _sigmoid(z)).astype(BF16)
    ga_ref[...] = _sigmoid(mm(wga_ref)).astype(BF16)
    gb_ref[...] = _sigmoid(mm(wgb_ref)).astype(BF16)


def _in_proj(x2d, g, weights, tm, vt_shape, vt_spec):
    rows, d = x2d.shape
    width = weights[1].shape[1]
    row_spec = lambda w: pl.BlockSpec((tm, w), lambda i: (i, 0))
    out_shape = [jax.ShapeDtypeStruct((rows, SMALL_W), F32)]
    out_specs = [row_spec(SMALL_W)]
    for name in ("szm", "fq", "fk", "fvt", "szf", "ga", "gb"):
        if name == "fvt":
            out_shape.append(jax.ShapeDtypeStruct(vt_shape, BF16))
            out_specs.append(vt_spec)
        else:
            out_shape.append(jax.ShapeDtypeStruct((rows, width), BF16))
            out_specs.append(row_spec(width))
    return pl.pallas_call(
        _in_proj_kernel,
        grid=(rows // tm,),
        in_specs=[row_spec(d), _const_spec(g.shape)] + [_const_spec(w.shape) for w in weights],
        out_specs=out_specs,
        out_shape=out_shape,
        compiler_params=pltpu.CompilerParams(dimension_semantics=("parallel",), vmem_limit_bytes=VMEM_LIMIT),
        name="in_proj",
    )(x2d, g, *weights)


def _mla_prep_kernel(small_ref, gq_ref, gkv_ref, wqm_ref, wqr_ref, wkm_ref, wvt_ref,
                     cosq_ref, sinq_ref, cosk_ref, sink_ref, q_ref, k_ref, vt_ref):
    sm = small_ref[...]
    cn = _rms(sm[:, 0:MLA_Q_RANK], gq_ref[...]).astype(BF16)
    kn = _rms(sm[:, MLA_Q_RANK:MLA_Q_RANK + MLA_KV_RANK], gkv_ref[...]).astype(BF16)
    k_rope = sm[:, 3 * LANE:4 * LANE] * cosk_ref[...] + sm[:, 4 * LANE:5 * LANE] * sink_ref[...]
    qm = jnp.dot(cn, wqm_ref[...], preferred_element_type=F32)
    qr = jnp.dot(cn, wqr_ref[...], preferred_element_type=F32)
    km = jnp.dot(kn, wkm_ref[...], preferred_element_type=F32)
    cosq = cosq_ref[...]
    sinq = sinq_ref[...]
    for h in range(HEADS):
        sl = slice(h * LANE, (h + 1) * LANE)
        q_ref[:, sl] = (qm[:, sl] * cosq + qr[:, sl] * sinq).astype(BF16)
        k_ref[:, sl] = (km[:, sl] + k_rope).astype(BF16)
    vt = lax.dot_general(wvt_ref[...], kn, NT_DIMS, preferred_element_type=F32)
    for hp in range(HEAD_PAIRS):
        vt_ref[hp] = vt[hp * LANE:(hp + 1) * LANE, :].astype(BF16)


def _mla_prep(small, gq, gkv, weights, tables, tm, tiles_per_seq, vt_shape, vt_spec):
    rows = small.shape[0]
    wide = HEADS * LANE
    tab_spec = pl.BlockSpec((tm, LANE), lambda i: (i % tiles_per_seq, 0))
    return pl.pallas_call(
        _mla_prep_kernel,
        grid=(rows // tm,),
        in_specs=[pl.BlockSpec((tm, SMALL_W), lambda i: (i, 0)), _const_spec(gq.shape), _const_spec(gkv.shape)]
        + [_const_spec(w.shape) for w in weights] + [tab_spec] * 4,
        out_specs=[pl.BlockSpec((tm, wide), lambda i: (i, 0)), pl.BlockSpec((tm, wide), lambda i: (i, 0)), vt_spec],
        out_shape=[jax.ShapeDtypeStruct((rows, wide), BF16), jax.ShapeDtypeStruct((rows, wide), BF16),
                   jax.ShapeDtypeStruct(vt_shape, BF16)],
        compiler_params=pltpu.CompilerParams(dimension_semantics=("parallel",), vmem_limit_bytes=VMEM_LIMIT),
        name="mla_prep",
    )(small, gq, gkv, *weights, *tables)


def _log_sigmoid(z):
    return jnp.minimum(z, 0.0) - jnp.log1p(jnp.exp(-jnp.abs(z)))


def _tri(n):
    row = lax.broadcasted_iota(jnp.int32, (n, n), 0)
    col = lax.broadcasted_iota(jnp.int32, (n, n), 1)
    return jnp.where(col <= row, 1.0, 0.0).astype(BF16)


def _cumsum_rows(tri, x):
    return sum(jnp.dot(tri, part, preferred_element_type=F32) for part in _split3(x))


def _place(x, place_ref):
    return sum(jnp.dot(part, place_ref[k], preferred_element_type=F32)
               for k, part in enumerate(_split3(x))).astype(BF16)


def _decay_kernel(sm_ref, bias_ref, place_ref, ex_ref, crow_ref, beta_ref, *, tile):
    n_tiles = sm_ref.shape[0] // tile
    tri = _tri(tile)
    eye = jnp.where(lax.broadcasted_iota(jnp.int32, (LANE, LANE), 0)
                    == lax.broadcasted_iota(jnp.int32, (LANE, LANE), 1), 1.0, 0.0).astype(BF16)
    carry = jnp.zeros((1, LANE), F32)
    for j in range(n_tiles):
        rows = slice(j * tile, (j + 1) * tile)
        within = _cumsum_rows(tri, _log_sigmoid(sm_ref[rows, :] + bias_ref[...]))
        ex_ref[rows, :] = _place(-within, place_ref)
        cum = within + carry
        cum_t = sum(lax.dot_general(eye, part, NT_DIMS, preferred_element_type=F32) for part in _split3(cum))
        crow_ref[:, rows] = cum_t[0:HEADS, :]
        beta_ref[j:j + 1, :] = carry
        carry = cum[tile - 1:tile, :]


def _decay_meta_kernel(sm_ref, bias_ref, place_ref, ex_ref):
    n = sm_ref.shape[0]
    cum = _cumsum_rows(_tri(n), _log_sigmoid(sm_ref[...] + bias_ref[...]))
    ex_ref[...] = _place(cum[n - 1:n, :] - cum, place_ref)


def _decay_prep(small, small_meta, bias_row, place, batch, seq):
    width = place.shape[2]
    ex, crow, beta = pl.pallas_call(
        functools.partial(_decay_kernel, tile=ATT_TILE),
        grid=(batch,),
        in_specs=[pl.BlockSpec((seq, LANE), lambda b: (b, 3)), _const_spec(bias_row.shape), _const_spec(place.shape)],
        out_specs=[pl.BlockSpec((seq, width), lambda b: (b, 0)),
                   pl.BlockSpec((None, HEADS, seq), lambda b: (b, 0, 0)),
                   pl.BlockSpec((None, seq // ATT_TILE, LANE), lambda b: (b, 0, 0))],
        out_shape=[jax.ShapeDtypeStruct((batch * seq, width), BF16),
                   jax.ShapeDtypeStruct((batch, HEADS, seq), F32),
                   jax.ShapeDtypeStruct((batch, seq // ATT_TILE, LANE), F32)],
        compiler_params=pltpu.CompilerParams(dimension_semantics=("parallel",), vmem_limit_bytes=VMEM_LIMIT),
        name="decay_prep",
    )(small, bias_row, place)
    ex_meta = pl.pallas_call(
        _decay_meta_kernel,
        grid=(1,),
        in_specs=[pl.BlockSpec((N_META, LANE), lambda i: (0, 3)), _const_spec(bias_row.shape),
                  _const_spec(place.shape)],
        out_specs=pl.BlockSpec((N_META, width), lambda i: (0, 0)),
        out_shape=jax.ShapeDtypeStruct((N_META, width), BF16),
        name="decay_prep_meta",
    )(small_meta, bias_row, place)
    return ex, crow, beta, ex_meta


def _attend_tile(q_h, k_t, vt_t, m_old, acc_old, off, causal):
    s = lax.dot_general(k_t, q_h, NT_DIMS, preferred_element_type=F32)
    if causal:
        key = lax.broadcasted_iota(jnp.int32, s.shape, 0)
        qry = lax.broadcasted_iota(jnp.int32, s.shape, 1)
        s = jnp.where(key <= qry, s, NEG_INF)
    m_tile = jnp.max(s, axis=0, keepdims=True)
    if off is not None:
        m_tile = m_tile + off
    m_new = jnp.maximum(m_old, m_tile) if m_old is not None else m_tile
    shift = m_new if off is None else m_new - off
    p = jnp.exp(s - shift).astype(BF16)
    v_aug = jnp.concatenate([vt_t, jnp.ones((L_ROWS, vt_t.shape[1]), BF16)], axis=0)
    pv = jnp.dot(v_aug, p, preferred_element_type=F32)
    if m_old is None:
        return m_new, pv
    return m_new, jnp.exp(m_old - m_new) * acc_old + pv


def _attend_head(i, q_h, k_meta, vt_meta, k_tile, vt_tile, off_meta, off_tile):
    m, acc = _attend_tile(q_h, k_meta, vt_meta, None, None, off_meta, False)

    def body(j, carry):
        return _attend_tile(q_h, k_tile(j), vt_tile(j), carry[0], carry[1],
                            None if off_tile is None else off_tile(j), False)

    m, acc = lax.fori_loop(0, i, body, (m, acc))
    m, acc = _attend_tile(q_h, k_tile(i), vt_tile(i), m, acc, None if off_tile is None else off_tile(i), True)
    return acc[0:FOX_DIM, :] / acc[FOX_DIM:FOX_DIM + 1, :]


def _mla_attn_kernel(q_ref, k_ref, vt_ref, km_ref, vmt_ref, o_ref):
    i = pl.program_id(2)
    outs = []
    for hh in range(2):
        lanes = slice(hh * LANE, (hh + 1) * LANE)
        rows64 = slice(hh * MLA_V, (hh + 1) * MLA_V)
        k_tile = lambda j, lanes=lanes: k_ref[pl.ds(pl.multiple_of(j * ATT_TILE, ATT_TILE), ATT_TILE), lanes]
        vt_tile = lambda j, rows64=rows64: vt_ref[j, rows64, :]
        outs.append(_attend_head(i, q_ref[:, lanes], km_ref[:, lanes], vmt_ref[rows64, :],
                                 k_tile, vt_tile, None, None))
    o_ref[...] = jnp.concatenate(outs, axis=0).T.astype(BF16)


def _fox_attn_kernel(beta_ref, q_ref, k_ref, ex_ref, vt_ref, km_ref, exm_ref, vmt_ref, crow_ref, o_ref, kaug_ref):
    b = pl.program_id(0)
    hp = pl.program_id(1)
    i = pl.program_id(2)
    n_tiles = k_ref.shape[0] // ATT_TILE
    lane = lax.broadcasted_iota(jnp.int32, (1, LANE), 1)
    in_a = lane < FOX_DIM

    @pl.when(i == 0)
    def _():
        def build(c, _):
            rows = pl.ds(pl.multiple_of(c * ATT_TILE, ATT_TILE), ATT_TILE)
            kk = k_ref[rows, :]
            ee = ex_ref[rows, :]
            kaug_ref[0, rows, :] = jnp.where(in_a, kk, ee)
            kaug_ref[1, rows, :] = jnp.where(in_a, ee, kk)
            return 0
        lax.fori_loop(0, n_tiles, build, 0)

    q = q_ref[...]
    km = km_ref[...]
    exm = exm_ref[...]
    outs = []
    for hh in range(2):
        own = in_a if hh == 0 else jnp.logical_not(in_a)
        base = FOX_DIM if hh == 0 else 0
        ones = jnp.where((lane >= base) & (lane < base + 3), 1.0, 0.0).astype(BF16)
        q_h = jnp.where(own, q, ones)
        k_meta = jnp.where(own, km, exm)
        rows64 = slice(hh * FOX_DIM, (hh + 1) * FOX_DIM)
        cum_q = crow_ref[hh:hh + 1, :]
        beta_base = ((b * HEAD_PAIRS + hp) * 2 + hh) * n_tiles
        k_tile = lambda j, hh=hh: kaug_ref[hh, pl.ds(pl.multiple_of(j * ATT_TILE, ATT_TILE), ATT_TILE), :]
        vt_tile = lambda j, rows64=rows64: vt_ref[j, rows64, :]
        off_tile = lambda j, cum_q=cum_q, beta_base=beta_base: cum_q - beta_ref[beta_base + j]
        outs.append(_attend_head(i, q_h, k_meta, vmt_ref[rows64, :], k_tile, vt_tile, cum_q, off_tile))
    o_ref[...] = jnp.concatenate(outs, axis=0).T.astype(BF16)


def _attention_specs(batch, seq, width):
    n_tiles = seq // ATT_TILE
    q_spec = pl.BlockSpec((None, ATT_TILE, width), lambda b, hp, i, *_: (b, i, hp))
    k_spec = pl.BlockSpec((None, seq, width), lambda b, hp, i, *_: (b, 0, hp))
    vt_spec = pl.BlockSpec((None, None, n_tiles, LANE, ATT_TILE), lambda b, hp, i, *_: (b, hp, 0, 0, 0))
    km_spec = pl.BlockSpec((N_META, width), lambda b, hp, i, *_: (0, hp))
    vmt_spec = pl.BlockSpec((None, LANE, N_META), lambda b, hp, i, *_: (hp, 0, 0))
    o_spec = pl.BlockSpec((None, ATT_TILE, LANE), lambda b, hp, i, *_: (b, i, hp))
    return q_spec, k_spec, vt_spec, km_spec, vmt_spec, o_spec


def _mla_attention(q, k, vt, km, vmt):
    batch, seq, _ = q.shape
    q_spec, k_spec, vt_spec, km_spec, vmt_spec, o_spec = _attention_specs(batch, seq, 2 * LANE)
    return pl.pallas_call(
        _mla_attn_kernel,
        grid=(batch, HEAD_PAIRS, seq // ATT_TILE),
        in_specs=[q_spec, k_spec, vt_spec, km_spec, vmt_spec],
        out_specs=o_spec,
        out_shape=jax.ShapeDtypeStruct((batch, seq, HEADS * MLA_V), BF16),
        compiler_params=pltpu.CompilerParams(dimension_semantics=("parallel", "parallel", "arbitrary"),
                                             vmem_limit_bytes=VMEM_LIMIT),
        name="mla_attention",
    )(q, k, vt, km, vmt)


def _fox_attention(beta, q, k, ex, vt, km, exm, vmt, crow):
    batch, seq, _ = q.shape
    q_spec, k_spec, vt_spec, km_spec, vmt_spec, o_spec = _attention_specs(batch, seq, LANE)
    crow_spec = pl.BlockSpec((None, None, 2, ATT_TILE), lambda b, hp, i, *_: (b, hp, 0, i))
    grid_spec = pltpu.PrefetchScalarGridSpec(
        num_scalar_prefetch=1,
        grid=(batch, HEAD_PAIRS, seq // ATT_TILE),
        in_specs=[q_spec, k_spec, k_spec, vt_spec, km_spec, km_spec, vmt_spec, crow_spec],
        out_specs=o_spec,
        scratch_shapes=[pltpu.VMEM((2, seq, LANE), BF16)],
    )
    return pl.pallas_call(
        _fox_attn_kernel,
        grid_spec=grid_spec,
        out_shape=jax.ShapeDtypeStruct((batch, seq, HEADS * FOX_DIM), BF16),
        compiler_params=pltpu.CompilerParams(dimension_semantics=("parallel", "parallel", "arbitrary"),
                                             vmem_limit_bytes=VMEM_LIMIT),
        name="fox_attention",
    )(beta, q, k, ex, vt, km, exm, vmt, crow)


def _merge_kernel(om_ref, szm_ref, of_ref, szf_ref, ga_ref, gb_ref, x_ref, wa_ref, wb_ref, wo_ref, g_ref, out_ref):
    ya = jnp.dot(om_ref[...] * szm_ref[...], wa_ref[...], preferred_element_type=F32)
    yb = jnp.dot(of_ref[...] * szf_ref[...], wb_ref[...], preferred_element_type=F32)
    mixed_in = (ga_ref[...].astype(F32) * ya + gb_ref[...].astype(F32) * yb).astype(BF16)
    mixed = jnp.dot(mixed_in, wo_ref[...], preferred_element_type=F32)
    out_ref[...] = x_ref[...] + _rms(mixed, g_ref[...])


def _merge(om, szm, of, szf, ga, gb, x2d, wa, wb, wo, g):
    rows, d = x2d.shape
    row_spec = pl.BlockSpec((ROW_TILE, d), lambda i: (i, 0))
    return pl.pallas_call(
        _merge_kernel,
        grid=(rows // ROW_TILE,),
        in_specs=[row_spec] * 7 + [_const_spec(w.shape) for w in (wa, wb, wo, g)],
        out_specs=row_spec,
        out_shape=jax.ShapeDtypeStruct((rows, d), F32),
        compiler_params=pltpu.CompilerParams(dimension_semantics=("parallel",), vmem_limit_bytes=VMEM_LIMIT),
        name="merge",
    )(om, szm, of, szf, ga, gb, x2d, wa, wb, wo, g)


def _rotate_half_cols(w):
    half = w.shape[-1] // 2
    return jnp.concatenate([-w[..., half:], w[..., :half]], axis=-1)


def _pad_cols(w, before, total):
    return jnp.pad(w, ((0, 0), (before, total - before - w.shape[1])))


def _in_proj_weights(w_in):
    widths = (MLA_Q_RANK, MLA_KV_RANK, MLA_ROPE, HEADS * MLA_V, HEADS * FOX_DIM, HEADS * FOX_DIM,
              HEADS * FOX_DIM, HEADS, HEADS * FOX_DIM, w_in.shape[0], w_in.shape[0])
    assert sum(widths) == w_in.shape[1]
    bounds = np.cumsum((0,) + widths)
    w_cq, w_ckv, w_kpe, w_zm, w_fq, w_fk, w_fv, w_fl, w_zf, w_ga, w_gb = (
        w_in[:, bounds[n]:bounds[n + 1]] for n in range(len(widths)))
    w_small = jnp.concatenate([
        w_cq, w_ckv,
        _pad_cols(w_fl, 0, KPE_LANE), _pad_cols(w_kpe, 0, LANE - KPE_LANE),
        _pad_cols(_rotate_half_cols(w_kpe), KPE_LANE, LANE)], axis=1)
    assert w_small.shape[1] == SMALL_W
    return [w.astype(BF16) for w in (w_small, w_zm, w_fq, w_fk, w_fv.T, w_zf, w_ga, w_gb)]


def _mla_weights(w_uq, w_ukv):
    dq = MLA_NOPE + MLA_ROPE
    wq = w_uq.reshape(MLA_Q_RANK, HEADS, dq)
    pad = LANE - dq
    wq_main = jnp.pad(wq, ((0, 0), (0, 0), (0, pad)))
    wq_rot = jnp.pad(_rotate_half_cols(wq[:, :, MLA_NOPE:]), ((0, 0), (0, 0), (MLA_NOPE, pad)))
    wkv = w_ukv.reshape(MLA_KV_RANK, HEADS, MLA_NOPE + MLA_V)
    wk_main = jnp.pad(wkv[:, :, :MLA_NOPE], ((0, 0), (0, 0), (0, LANE - MLA_NOPE)))
    wv_t = wkv[:, :, MLA_NOPE:].reshape(MLA_KV_RANK, HEADS * MLA_V).T
    flat = lambda w: w.reshape(w.shape[0], HEADS * LANE)
    return [w.astype(BF16) for w in (flat(wq_main), flat(wq_rot), flat(wk_main), wv_t)]


def _rope_tables(first_pos, n_pos):
    half = MLA_ROPE // 2
    inv_freq = ROPE_THETA ** (-np.arange(half, dtype=np.float64) / half)
    ang = (first_pos + np.arange(n_pos, dtype=np.float64))[:, None] * inv_freq[None, :]
    cos = np.concatenate([np.cos(ang), np.cos(ang)], axis=1)
    sin = np.concatenate([np.sin(ang), np.sin(ang)], axis=1)
    pad = np.zeros((n_pos, LANE - MLA_NOPE - MLA_ROPE))
    ones = np.ones((n_pos, MLA_NOPE))
    zeros = np.zeros((n_pos, MLA_NOPE))
    cos_q = np.concatenate([ones, cos, pad], axis=1) * MLA_SCALE
    sin_q = np.concatenate([zeros, sin, pad], axis=1) * MLA_SCALE
    cos_k = np.concatenate([zeros, cos, pad], axis=1)
    sin_k = np.concatenate([zeros, sin, pad], axis=1)
    return [jnp.asarray(t, F32) for t in (cos_q, sin_q, cos_k, sin_k)]


def _placement():
    place = np.zeros((3, LANE, HEAD_PAIRS * LANE), np.float32)
    for h in range(HEADS):
        base = (h // 2) * LANE + (FOX_DIM if h % 2 == 0 else 0)
        for part in range(3):
            place[part, h, base + part] = 1.0
    return jnp.asarray(place, BF16)


def kernel(x, meta_tokens, pre_norm_g, w_in, fox_forget_b, mla_q_norm_g, mla_kv_norm_g, w_uq, w_ukv,
           w_br_mla, w_br_fox, w_out, post_norm_g):
    batch, seq, d = x.shape
    assert pre_norm_g.shape[0] == 1, "one layer supported"
    assert meta_tokens.shape[0] == N_META and seq % ATT_TILE == 0 and ATT_TILE == ROW_TILE
    n_tiles = seq // ATT_TILE
    x2d = x.reshape(batch * seq, d)

    in_w = _in_proj_weights(w_in[0])
    mla_w = _mla_weights(w_uq[0], w_ukv[0])
    g_pre = pre_norm_g.astype(F32)
    gq = mla_q_norm_g.astype(F32)
    gkv = mla_kv_norm_g.astype(F32)
    bias_row = jnp.pad(fox_forget_b.astype(F32), ((0, 0), (0, LANE - HEADS)))

    vt_shape = (batch, HEAD_PAIRS, n_tiles, LANE, ATT_TILE)
    vt_spec = pl.BlockSpec((None, HEAD_PAIRS, None, LANE, ROW_TILE), lambda i: (i // n_tiles, 0, i % n_tiles, 0, 0))
    vt_meta_shape = (HEAD_PAIRS, LANE, N_META)
    vt_meta_spec = pl.BlockSpec(vt_meta_shape, lambda i: (0, 0, 0))

    small, szm, fq, fk, fvt, szf, ga, gb = _in_proj(x2d, g_pre, in_w, ROW_TILE, vt_shape, vt_spec)
    small_m, _, _, fk_m, fvt_m, _, _, _ = _in_proj(meta_tokens.astype(F32), g_pre, in_w, N_META,
                                                    vt_meta_shape, vt_meta_spec)
    q, k, vt = _mla_prep(small, gq, gkv, mla_w, _rope_tables(N_META, seq), ROW_TILE, n_tiles, vt_shape, vt_spec)
    _, k_m, vt_m = _mla_prep(small_m, gq, gkv, mla_w, _rope_tables(0, N_META), N_META, 1,
                             vt_meta_shape, vt_meta_spec)
    ex, crow, beta, ex_m = _decay_prep(small, small_m, bias_row, _placement(), batch, seq)

    o_mla = _mla_attention(q.reshape(batch, seq, -1), k.reshape(batch, seq, -1), vt, k_m, vt_m)
    beta_flat = jnp.transpose(beta[:, :, :HEADS], (0, 2, 1)).reshape(-1)
    o_fox = _fox_attention(beta_flat, fq.reshape(batch, seq, -1), fk.reshape(batch, seq, -1),
                           ex.reshape(batch, seq, -1), fvt, fk_m, ex_m, fvt_m,
                           crow.reshape(batch, HEAD_PAIRS, 2, seq))

    out = _merge(o_mla.reshape(batch * seq, -1), szm, o_fox.reshape(batch * seq, -1), szf, ga, gb, x2d,
                 w_br_mla[0].astype(BF16), w_br_fox[0].astype(BF16), w_out[0].astype(BF16),
                 post_norm_g.astype(F32))
    return out.reshape(batch, seq, d)
```

```python
import functools
import math
from typing import Any, Callable, NamedTuple

import numpy as np
import jax
import jax.numpy as jnp
from jax import lax
from jax.experimental import pallas as pl
from jax.experimental.pallas import tpu as pltpu

F32 = jnp.float32
BF16 = jnp.bfloat16

N_META = 16
RMS_EPS = 1e-6
HEADS = 16
MLA_Q_RANK = 256
MLA_KV_RANK = 128
MLA_NOPE = 64
MLA_ROPE = 32
MLA_V = 64
MLA_SCALE = 1.0 / math.sqrt(MLA_NOPE + MLA_ROPE)
ROPE_THETA = 10000.0
FOX_DIM = 64
FOX_SCALE = 1.0 / math.sqrt(FOX_DIM)
LOG2E = math.log2(math.e)

LANE = 128
HEAD_PAIRS = HEADS // 2
SMALL_W = 5 * LANE
KPE_LANE = 64
ROW_TILE = 512
ATT_TILE = 512
L_ROWS = 16
NEG_INF = -1e30
VMEM_LIMIT = 56 * 1024 * 1024

NT_DIMS = (((1,), (1,)), ((), ()))


def _rms(x, g):
    return x * lax.rsqrt(jnp.mean(x * x, axis=-1, keepdims=True) + RMS_EPS) * g


def _sigmoid(x):
    return 1.0 / (1.0 + jnp.exp(-x))


def _split3(x):
    hi = x.astype(BF16)
    r1 = x - hi.astype(F32)
    mid = r1.astype(BF16)
    lo = (r1 - mid.astype(F32)).astype(BF16)
    return hi, mid, lo


def _const_spec(shape):
    nd = len(shape)
    return pl.BlockSpec(shape, lambda *_: (0,) * nd, pipeline_mode=pl.Buffered(1))


def _in_proj_kernel(x_ref, g_ref, ws_ref, wzm_ref, wfq_ref, wfk_ref, wfvt_ref, wzf_ref, wga_ref, wgb_ref,
                    small_ref, szm_ref, fq_ref, fk_ref, fvt_ref, szf_ref, ga_ref, gb_ref):
    u = _rms(x_ref[...], g_ref[...]).astype(BF16)

    def mm(w_ref):
        return jnp.dot(u, w_ref[...], preferred_element_type=F32)

    small_ref[...] = mm(ws_ref)
    z = mm(wzm_ref)
    szm_ref[...] = (z * _sigmoid(z)).astype(BF16)
    fq_ref[...] = (mm(wfq_ref) * (FOX_SCALE * LOG2E)).astype(BF16)
    fk_ref[...] = mm(wfk_ref).astype(BF16)
    vt = lax.dot_general(wfvt_ref[...], u, NT_DIMS, preferred_element_type=F32)
    for hp in range(HEAD_PAIRS):
        fvt_ref[hp] = vt[hp * LANE:(hp + 1) * LANE, :].astype(BF16)
    z = mm(wzf_ref)
    szf_ref[...] = (z * _sigmoid(z)).astype(BF16)
    ga_ref[...] = _sigmoid(mm(wga_ref)).astype(BF16)
    gb_ref[...] = _sigmoid(mm(wgb_ref)).astype(BF16)


def _in_proj(x2d, g, weights, tm, vt_shape, vt_spec):
    rows, d = x2d.shape
    width = weights[1].shape[1]
    row_spec = lambda w: pl.BlockSpec((tm, w), lambda i: (i, 0))
    out_shape = [jax.ShapeDtypeStruct((rows, SMALL_W), F32)]
    out_specs = [row_spec(SMALL_W)]
    for name in ("szm", "fq", "fk", "fvt", "szf", "ga", "gb"):
        if name == "fvt":
            out_shape.append(jax.ShapeDtypeStruct(vt_shape, BF16))
            out_specs.append(vt_spec)
        else:
            out_shape.append(jax.ShapeDtypeStruct((rows, width), BF16))
            out_specs.append(row_spec(width))
    return pl.pallas_call(
        _in_proj_kernel,
        grid=(rows // tm,),
        in_specs=[row_spec(d), _const_spec(g.shape)] + [_const_spec(w.shape) for w in weights],
        out_specs=out_specs,
        out_shape=out_shape,
        compiler_params=pltpu.CompilerParams(dimension_semantics=("parallel",), vmem_limit_bytes=VMEM_LIMIT),
        name="in_proj",
    )(x2d, g, *weights)


def _mla_prep_kernel(small_ref, gq_ref, gkv_ref, wqm_ref, wqr_ref, wkm_ref, wvt_ref,
                     cosq_ref, sinq_ref, cosk_ref, sink_ref, q_ref, k_ref, vt_ref):
    sm = small_ref[...]
    cn = _rms(sm[:, 0:MLA_Q_RANK], gq_ref[...]).astype(BF16)
    kn = _rms(sm[:, MLA_Q_RANK:MLA_Q_RANK + MLA_KV_RANK], gkv_ref[...]).astype(BF16)
    k_rope = sm[:, 3 * LANE:4 * LANE] * cosk_ref[...] + sm[:, 4 * LANE:5 * LANE] * sink_ref[...]
    qm = jnp.dot(cn, wqm_ref[...], preferred_element_type=F32)
    qr = jnp.dot(cn, wqr_ref[...], preferred_element_type=F32)
    km = jnp.dot(kn, wkm_ref[...], preferred_element_type=F32)
    cosq = cosq_ref[...]
    sinq = sinq_ref[...]
    for h in range(HEADS):
        sl = slice(h * LANE, (h + 1) * LANE)
        q_ref[:, sl] = (qm[:, sl] * cosq + qr[:, sl] * sinq).astype(BF16)
        k_ref[:, sl] = (km[:, sl] + k_rope).astype(BF16)
    vt = lax.dot_general(wvt_ref[...], kn, NT_DIMS, preferred_element_type=F32)
    for hp in range(HEAD_PAIRS):
        vt_ref[hp] = vt[hp * LANE:(hp + 1) * LANE, :].astype(BF16)


def _mla_prep(small, gq, gkv, weights, tables, tm, tiles_per_seq, vt_shape, vt_spec):
    rows = small.shape[0]
    wide = HEADS * LANE
    tab_spec = pl.BlockSpec((tm, LANE), lambda i: (i % tiles_per_seq, 0))
    return pl.pallas_call(
        _mla_prep_kernel,
        grid=(rows // tm,),
        in_specs=[pl.BlockSpec((tm, SMALL_W), lambda i: (i, 0)), _const_spec(gq.shape), _const_spec(gkv.shape)]
        + [_const_spec(w.shape) for w in weights] + [tab_spec] * 4,
        out_specs=[pl.BlockSpec((tm, wide), lambda i: (i, 0)), pl.BlockSpec((tm, wide), lambda i: (i, 0)), vt_spec],
        out_shape=[jax.ShapeDtypeStruct((rows, wide), BF16), jax.ShapeDtypeStruct((rows, wide), BF16),
                   jax.ShapeDtypeStruct(vt_shape, BF16)],
        compiler_params=pltpu.CompilerParams(dimension_semantics=("parallel",), vmem_limit_bytes=VMEM_LIMIT),
        name="mla_prep",
    )(small, gq, gkv, *weights, *tables)


def _log2_sigmoid(z):
    return (jnp.minimum(z, 0.0) - jnp.log1p(jnp.exp(-jnp.abs(z)))) * LOG2E


def _tri(n):
    row = lax.broadcasted_iota(jnp.int32, (n, n), 0)
    col = lax.broadcasted_iota(jnp.int32, (n, n), 1)
    return jnp.where(col <= row, 1.0, 0.0).astype(BF16)


def _cumsum_rows(tri, x):
    return sum(jnp.dot(tri, part, preferred_element_type=F32) for part in _split3(x))


def _place(x, place_ref):
    return sum(jnp.dot(part, place_ref[k], preferred_element_type=F32)
               for k, part in enumerate(_split3(x))).astype(BF16)


def _decay_kernel(sm_ref, bias_ref, place_ref, ex_ref, crow_ref, beta_ref, *, tile):
    n_tiles = sm_ref.shape[0] // tile
    tri = _tri(tile)
    eye = jnp.where(lax.broadcasted_iota(jnp.int32, (LANE, LANE), 0)
                    == lax.broadcasted_iota(jnp.int32, (LANE, LANE), 1), 1.0, 0.0).astype(BF16)
    carry = jnp.zeros((1, LANE), F32)
    for j in range(n_tiles):
        rows = slice(j * tile, (j + 1) * tile)
        within = _cumsum_rows(tri, _log2_sigmoid(sm_ref[rows, :] + bias_ref[...]))
        ex_ref[rows, :] = _place(-within, place_ref)
        cum = within + carry
        cum_t = sum(lax.dot_general(eye, part, NT_DIMS, preferred_element_type=F32) for part in _split3(cum))
        crow_ref[:, rows] = cum_t[0:HEADS, :]
        beta_ref[j:j + 1, :] = carry
        carry = cum[tile - 1:tile, :]


def _decay_meta_kernel(sm_ref, bias_ref, place_ref, ex_ref):
    n = sm_ref.shape[0]
    cum = _cumsum_rows(_tri(n), _log2_sigmoid(sm_ref[...] + bias_ref[...]))
    ex_ref[...] = _place(cum[n - 1:n, :] - cum, place_ref)


def _decay_prep(small, small_meta, bias_row, place, batch, seq):
    width = place.shape[2]
    ex, crow, beta = pl.pallas_call(
        functools.partial(_decay_kernel, tile=ATT_TILE),
        grid=(batch,),
        in_specs=[pl.BlockSpec((seq, LANE), lambda b: (b, 3)), _const_spec(bias_row.shape), _const_spec(place.shape)],
        out_specs=[pl.BlockSpec((seq, width), lambda b: (b, 0)),
                   pl.BlockSpec((None, HEADS, seq), lambda b: (b, 0, 0)),
                   pl.BlockSpec((None, seq // ATT_TILE, LANE), lambda b: (b, 0, 0))],
        out_shape=[jax.ShapeDtypeStruct((batch * seq, width), BF16),
                   jax.ShapeDtypeStruct((batch, HEADS, seq), F32),
                   jax.ShapeDtypeStruct((batch, seq // ATT_TILE, LANE), F32)],
        compiler_params=pltpu.CompilerParams(dimension_semantics=("parallel",), vmem_limit_bytes=VMEM_LIMIT),
        name="decay_prep",
    )(small, bias_row, place)
    ex_meta = pl.pallas_call(
        _decay_meta_kernel,
        grid=(1,),
        in_specs=[pl.BlockSpec((N_META, LANE), lambda i: (0, 3)), _const_spec(bias_row.shape),
                  _const_spec(place.shape)],
        out_specs=pl.BlockSpec((N_META, width), lambda i: (0, 0)),
        out_shape=jax.ShapeDtypeStruct((N_META, width), BF16),
        name="decay_prep_meta",
    )(small_meta, bias_row, place)
    return ex, crow, beta, ex_meta


class _Head(NamedTuple):
    q: jax.Array
    k_meta: jax.Array
    vt_meta: jax.Array
    k_tile: Callable
    vt_tile: Callable
    off_meta: Any
    off_tile: Any


def _scores(q_h, k_t):
    return lax.dot_general(k_t, q_h, NT_DIMS, preferred_element_type=F32)


def _col_max(s):
    return jnp.max(s, axis=0, keepdims=True)


def _softmax_pv(s, m_tile, vt_t, m_old, acc_old, off):
    if off is not None:
        m_tile = m_tile + off
    m_new = m_tile if m_old is None else jnp.maximum(m_old, m_tile)
    shift = m_new if off is None else m_new - off
    p = jnp.exp2(s - shift).astype(BF16)
    v_aug = jnp.concatenate([vt_t, jnp.ones((L_ROWS, vt_t.shape[1]), BF16)], axis=0)
    pv = jnp.dot(v_aug, p, preferred_element_type=F32)
    if m_old is None:
        return m_new, pv
    return m_new, jnp.exp2(m_old - m_new) * acc_old + pv


def _attend_pair(i, heads, s_ref, acc_ref, o_ref):
    n = len(heads)
    ms = []
    for h, head in enumerate(heads):
        s = _scores(head.q, head.k_meta)
        m, acc = _softmax_pv(s, _col_max(s), head.vt_meta, None, None, head.off_meta)
        acc_ref[h] = acc
        ms.append(m)
    tile_max = []
    for h, head in enumerate(heads):
        s = _scores(head.q, head.k_tile(0))
        s_ref[h] = s
        tile_max.append(_col_max(s))

    def body(j, carry):
        ms, tile_max = carry
        cur = [s_ref[h] for h in range(n)]
        nxt = [_scores(head.q, head.k_tile(j + 1)) for head in heads]
        new_ms = []
        for h, head in enumerate(heads):
            off = None if head.off_tile is None else head.off_tile(j)
            m, acc = _softmax_pv(cur[h], tile_max[h], head.vt_tile(j), ms[h], acc_ref[h], off)
            acc_ref[h] = acc
            new_ms.append(m)
        for h in range(n):
            s_ref[h] = nxt[h]
        return tuple(new_ms), tuple(_col_max(s) for s in nxt)

    ms, _ = lax.fori_loop(0, i, body, (tuple(ms), tuple(tile_max)))

    outs = []
    for h, head in enumerate(heads):
        s = s_ref[h]
        key = lax.broadcasted_iota(jnp.int32, s.shape, 0)
        qry = lax.broadcasted_iota(jnp.int32, s.shape, 1)
        s = jnp.where(key <= qry, s, NEG_INF)
        off = None if head.off_tile is None else head.off_tile(i)
        _, acc = _softmax_pv(s, _col_max(s), head.vt_tile(i), ms[h], acc_ref[h], off)
        outs.append(acc[0:FOX_DIM, :] / acc[FOX_DIM:FOX_DIM + 1, :])
    o_ref[...] = jnp.concatenate(outs, axis=0).T.astype(BF16)


def _key_rows(j):
    return pl.ds(pl.multiple_of(j * ATT_TILE, ATT_TILE), ATT_TILE)


def _mla_attn_kernel(q_ref, k_ref, vt_ref, km_ref, vmt_ref, o_ref, s_ref, acc_ref):
    heads = []
    for hh in range(2):
        lanes = slice(hh * LANE, (hh + 1) * LANE)
        rows64 = slice(hh * MLA_V, (hh + 1) * MLA_V)
        heads.append(_Head(
            q=q_ref[:, lanes], k_meta=km_ref[:, lanes], vt_meta=vmt_ref[rows64, :],
            k_tile=lambda j, lanes=lanes: k_ref[_key_rows(j), lanes],
            vt_tile=lambda j, rows64=rows64: vt_ref[j, rows64, :],
            off_meta=None, off_tile=None))
    _attend_pair(pl.program_id(2), heads, s_ref, acc_ref, o_ref)


def _fox_attn_kernel(beta_ref, q_ref, k_ref, ex_ref, vt_ref, km_ref, exm_ref, vmt_ref, crow_ref, o_ref,
                     kaug_ref, s_ref, acc_ref):
    b = pl.program_id(0)
    hp = pl.program_id(1)
    i = pl.program_id(2)
    n_tiles = k_ref.shape[0] // ATT_TILE
    lane = lax.broadcasted_iota(jnp.int32, (1, LANE), 1)
    in_a = lane < FOX_DIM

    @pl.when(i == 0)
    def _():
        def build(c, _):
            rows = _key_rows(c)
            kk = k_ref[rows, :]
            ee = ex_ref[rows, :]
            kaug_ref[0, rows, :] = jnp.where(in_a, kk, ee)
            kaug_ref[1, rows, :] = jnp.where(in_a, ee, kk)
            return 0
        lax.fori_loop(0, n_tiles, build, 0)

    q = q_ref[...]
    km = km_ref[...]
    exm = exm_ref[...]
    heads = []
    for hh in range(2):
        own = in_a if hh == 0 else jnp.logical_not(in_a)
        base = FOX_DIM if hh == 0 else 0
        ones = jnp.where((lane >= base) & (lane < base + 3), 1.0, 0.0).astype(BF16)
        rows64 = slice(hh * FOX_DIM, (hh + 1) * FOX_DIM)
        cum_q = crow_ref[hh:hh + 1, :]
        beta_base = ((b * HEAD_PAIRS + hp) * 2 + hh) * n_tiles
        heads.append(_Head(
            q=jnp.where(own, q, ones), k_meta=jnp.where(own, km, exm), vt_meta=vmt_ref[rows64, :],
            k_tile=lambda j, hh=hh: kaug_ref[hh, _key_rows(j), :],
            vt_tile=lambda j, rows64=rows64: vt_ref[j, rows64, :],
            off_meta=cum_q,
            off_tile=lambda j, cum_q=cum_q, beta_base=beta_base: cum_q - beta_ref[beta_base + j]))
    _attend_pair(i, heads, s_ref, acc_ref, o_ref)


def _attention_specs(batch, seq, width):
    n_tiles = seq // ATT_TILE
    q_spec = pl.BlockSpec((None, ATT_TILE, width), lambda b, hp, i, *_: (b, i, hp))
    k_spec = pl.BlockSpec((None, seq, width), lambda b, hp, i, *_: (b, 0, hp))
    vt_spec = pl.BlockSpec((None, None, n_tiles, LANE, ATT_TILE), lambda b, hp, i, *_: (b, hp, 0, 0, 0))
    km_spec = pl.BlockSpec((N_META, width), lambda b, hp, i, *_: (0, hp))
    vmt_spec = pl.BlockSpec((None, LANE, N_META), lambda b, hp, i, *_: (hp, 0, 0))
    o_spec = pl.BlockSpec((None, ATT_TILE, LANE), lambda b, hp, i, *_: (b, i, hp))
    return q_spec, k_spec, vt_spec, km_spec, vmt_spec, o_spec


def _pair_scratch():
    return [pltpu.VMEM((2, ATT_TILE, ATT_TILE), F32), pltpu.VMEM((2, MLA_V + L_ROWS, ATT_TILE), F32)]


def _mla_attention(q, k, vt, km, vmt):
    batch, seq, _ = q.shape
    q_spec, k_spec, vt_spec, km_spec, vmt_spec, o_spec = _attention_specs(batch, seq, 2 * LANE)
    return pl.pallas_call(
        _mla_attn_kernel,
        grid=(batch, HEAD_PAIRS, seq // ATT_TILE),
        in_specs=[q_spec, k_spec, vt_spec, km_spec, vmt_spec],
        out_specs=o_spec,
        out_shape=jax.ShapeDtypeStruct((batch, seq, HEADS * MLA_V), BF16),
        scratch_shapes=_pair_scratch(),
        compiler_params=pltpu.CompilerParams(dimension_semantics=("parallel", "parallel", "arbitrary"),
                                             vmem_limit_bytes=VMEM_LIMIT),
        name="mla_attention",
    )(q, k, vt, km, vmt)


def _fox_attention(beta, q, k, ex, vt, km, exm, vmt, crow):
    batch, seq, _ = q.shape
    q_spec, k_spec, vt_spec, km_spec, vmt_spec, o_spec = _attention_specs(batch, seq, LANE)
    crow_spec = pl.BlockSpec((None, None, 2, ATT_TILE), lambda b, hp, i, *_: (b, hp, 0, i))
    grid_spec = pltpu.PrefetchScalarGridSpec(
        num_scalar_prefetch=1,
        grid=(batch, HEAD_PAIRS, seq // ATT_TILE),
        in_specs=[q_spec, k_spec, k_spec, vt_spec, km_spec, km_spec, vmt_spec, crow_spec],
        out_specs=o_spec,
        scratch_shapes=[pltpu.VMEM((2, seq, LANE), BF16)] + _pair_scratch(),
    )
    return pl.pallas_call(
        _fox_attn_kernel,
        grid_spec=grid_spec,
        out_shape=jax.ShapeDtypeStruct((batch, seq, HEADS * FOX_DIM), BF16),
        compiler_params=pltpu.CompilerParams(dimension_semantics=("parallel", "parallel", "arbitrary"),
                                             vmem_limit_bytes=VMEM_LIMIT),
        name="fox_attention",
    )(beta, q, k, ex, vt, km, exm, vmt, crow)


def _merge_kernel(om_ref, szm_ref, of_ref, szf_ref, ga_ref, gb_ref, x_ref, wa_ref, wb_ref, wo_ref, g_ref, out_ref):
    ya = jnp.dot(om_ref[...] * szm_ref[...], wa_ref[...], preferred_element_type=F32)
    yb = jnp.dot(of_ref[...] * szf_ref[...], wb_ref[...], preferred_element_type=F32)
    mixed_in = (ga_ref[...].astype(F32) * ya + gb_ref[...].astype(F32) * yb).astype(BF16)
    mixed = jnp.dot(mixed_in, wo_ref[...], preferred_element_type=F32)
    out_ref[...] = x_ref[...] + _rms(mixed, g_ref[...])


def _merge(om, szm, of, szf, ga, gb, x2d, wa, wb, wo, g):
    rows, d = x2d.shape
    row_spec = pl.BlockSpec((ROW_TILE, d), lambda i: (i, 0))
    return pl.pallas_call(
        _merge_kernel,
        grid=(rows // ROW_TILE,),
        in_specs=[row_spec] * 7 + [_const_spec(w.shape) for w in (wa, wb, wo, g)],
        out_specs=row_spec,
        out_shape=jax.ShapeDtypeStruct((rows, d), F32),
        compiler_params=pltpu.CompilerParams(dimension_semantics=("parallel",), vmem_limit_bytes=VMEM_LIMIT),
        name="merge",
    )(om, szm, of, szf, ga, gb, x2d, wa, wb, wo, g)


def _rotate_half_cols(w):
    half = w.shape[-1] // 2
    return jnp.concatenate([-w[..., half:], w[..., :half]], axis=-1)


def _pad_cols(w, before, total):
    return jnp.pad(w, ((0, 0), (before, total - before - w.shape[1])))


def _in_proj_weights(w_in):
    widths = (MLA_Q_RANK, MLA_KV_RANK, MLA_ROPE, HEADS * MLA_V, HEADS * FOX_DIM, HEADS * FOX_DIM,
              HEADS * FOX_DIM, HEADS, HEADS * FOX_DIM, w_in.shape[0], w_in.shape[0])
    assert sum(widths) == w_in.shape[1]
    bounds = np.cumsum((0,) + widths)
    w_cq, w_ckv, w_kpe, w_zm, w_fq, w_fk, w_fv, w_fl, w_zf, w_ga, w_gb = (
        w_in[:, bounds[n]:bounds[n + 1]] for n in range(len(widths)))
    w_small = jnp.concatenate([
        w_cq, w_ckv,
        _pad_cols(w_fl, 0, KPE_LANE), _pad_cols(w_kpe, 0, LANE - KPE_LANE),
        _pad_cols(_rotate_half_cols(w_kpe), KPE_LANE, LANE)], axis=1)
    assert w_small.shape[1] == SMALL_W
    return [w.astype(BF16) for w in (w_small, w_zm, w_fq, w_fk, w_fv.T, w_zf, w_ga, w_gb)]


def _mla_weights(w_uq, w_ukv):
    dq = MLA_NOPE + MLA_ROPE
    wq = w_uq.reshape(MLA_Q_RANK, HEADS, dq)
    pad = LANE - dq
    wq_main = jnp.pad(wq, ((0, 0), (0, 0), (0, pad)))
    wq_rot = jnp.pad(_rotate_half_cols(wq[:, :, MLA_NOPE:]), ((0, 0), (0, 0), (MLA_NOPE, pad)))
    wkv = w_ukv.reshape(MLA_KV_RANK, HEADS, MLA_NOPE + MLA_V)
    wk_main = jnp.pad(wkv[:, :, :MLA_NOPE], ((0, 0), (0, 0), (0, LANE - MLA_NOPE)))
    wv_t = wkv[:, :, MLA_NOPE:].reshape(MLA_KV_RANK, HEADS * MLA_V).T
    flat = lambda w: w.reshape(w.shape[0], HEADS * LANE)
    return [w.astype(BF16) for w in (flat(wq_main), flat(wq_rot), flat(wk_main), wv_t)]


def _rope_tables(first_pos, n_pos):
    half = MLA_ROPE // 2
    inv_freq = ROPE_THETA ** (-np.arange(half, dtype=np.float64) / half)
    ang = (first_pos + np.arange(n_pos, dtype=np.float64))[:, None] * inv_freq[None, :]
    cos = np.concatenate([np.cos(ang), np.cos(ang)], axis=1)
    sin = np.concatenate([np.sin(ang), np.sin(ang)], axis=1)
    pad = np.zeros((n_pos, LANE - MLA_NOPE - MLA_ROPE))
    ones = np.ones((n_pos, MLA_NOPE))
    zeros = np.zeros((n_pos, MLA_NOPE))
    cos_q = np.concatenate([ones, cos, pad], axis=1) * (MLA_SCALE * LOG2E)
    sin_q = np.concatenate([zeros, sin, pad], axis=1) * (MLA_SCALE * LOG2E)
    cos_k = np.concatenate([zeros, cos, pad], axis=1)
    sin_k = np.concatenate([zeros, sin, pad], axis=1)
    return [jnp.asarray(t, F32) for t in (cos_q, sin_q, cos_k, sin_k)]


def _placement():
    place = np.zeros((3, LANE, HEAD_PAIRS * LANE), np.float32)
    for h in range(HEADS):
        base = (h // 2) * LANE + (FOX_DIM if h % 2 == 0 else 0)
        for part in range(3):
            place[part, h, base + part] = 1.0
    return jnp.asarray(place, BF16)


def kernel(x, meta_tokens, pre_norm_g, w_in, fox_forget_b, mla_q_norm_g, mla_kv_norm_g, w_uq, w_ukv,
           w_br_mla, w_br_fox, w_out, post_norm_g):
    batch, seq, d = x.shape
    assert pre_norm_g.shape[0] == 1, "one layer supported"
    assert meta_tokens.shape[0] == N_META and seq % ATT_TILE == 0 and ATT_TILE == ROW_TILE
    n_tiles = seq // ATT_TILE
    x2d = x.reshape(batch * seq, d)

    in_w = _in_proj_weights(w_in[0])
    mla_w = _mla_weights(w_uq[0], w_ukv[0])
    g_pre = pre_norm_g.astype(F32)
    gq = mla_q_norm_g.astype(F32)
    gkv = mla_kv_norm_g.astype(F32)
    bias_row = jnp.pad(fox_forget_b.astype(F32), ((0, 0), (0, LANE - HEADS)))

    vt_shape = (batch, HEAD_PAIRS, n_tiles, LANE, ATT_TILE)
    vt_spec = pl.BlockSpec((None, HEAD_PAIRS, None, LANE, ROW_TILE), lambda i: (i // n_tiles, 0, i % n_tiles, 0, 0))
    vt_meta_shape = (HEAD_PAIRS, LANE, N_META)
    vt_meta_spec = pl.BlockSpec(vt_meta_shape, lambda i: (0, 0, 0))

    small, szm, fq, fk, fvt, szf, ga, gb = _in_proj(x2d, g_pre, in_w, ROW_TILE, vt_shape, vt_spec)
    small_m, _, _, fk_m, fvt_m, _, _, _ = _in_proj(meta_tokens.astype(F32), g_pre, in_w, N_META,
                                                    vt_meta_shape, vt_meta_spec)
    q, k, vt = _mla_prep(small, gq, gkv, mla_w, _rope_tables(N_META, seq), ROW_TILE, n_tiles, vt_shape, vt_spec)
    _, k_m, vt_m = _mla_prep(small_m, gq, gkv, mla_w, _rope_tables(0, N_META), N_META, 1,
                             vt_meta_shape, vt_meta_spec)
    ex, crow, beta, ex_m = _decay_prep(small, small_m, bias_row, _placement(), batch, seq)

    o_mla = _mla_attention(q.reshape(batch, seq, -1), k.reshape(batch, seq, -1), vt, k_m, vt_m)
    beta_flat = jnp.transpose(beta[:, :, :HEADS], (0, 2, 1)).reshape(-1)
    o_fox = _fox_attention(beta_flat, fq.reshape(batch, seq, -1), fk.reshape(batch, seq, -1),
                           ex.reshape(batch, seq, -1), fvt, fk_m, ex_m, fvt_m,
                           crow.reshape(batch, HEAD_PAIRS, 2, seq))

    out = _merge(o_mla.reshape(batch * seq, -1), szm, o_fox.reshape(batch * seq, -1), szf, ga, gb, x2d,
                 w_br_mla[0].astype(BF16), w_br_fox[0].astype(BF16), w_out[0].astype(BF16),
                 post_norm_g.astype(F32))
    return out.reshape(batch, seq, d)
```

```python
import functools
import math
from typing import Any, Callable, NamedTuple

import numpy as np
import jax
import jax.numpy as jnp
from jax import lax
from jax.experimental import pallas as pl
from jax.experimental.pallas import tpu as pltpu

F32 = jnp.float32
BF16 = jnp.bfloat16

N_META = 16
RMS_EPS = 1e-6
HEADS = 16
MLA_Q_RANK = 256
MLA_KV_RANK = 128
MLA_NOPE = 64
MLA_ROPE = 32
MLA_V = 64
MLA_SCALE = 1.0 / math.sqrt(MLA_NOPE + MLA_ROPE)
ROPE_THETA = 10000.0
FOX_DIM = 64
FOX_SCALE = 1.0 / math.sqrt(FOX_DIM)
LOG2E = math.log2(math.e)

LANE = 128
HEAD_PAIRS = HEADS // 2
SMALL_W = 5 * LANE
KPE_LANE = 64
ROW_TILE = 512
ATT_TILE = 512
Q_SUB = 2
L_ROWS = 16
NEG_INF = -1e30
VMEM_LIMIT = 56 * 1024 * 1024

NT_DIMS = (((1,), (1,)), ((), ()))


def _rms(x, g):
    return x * lax.rsqrt(jnp.mean(x * x, axis=-1, keepdims=True) + RMS_EPS) * g


def _sigmoid(x):
    return 1.0 / (1.0 + jnp.exp(-x))


def _split3(x):
    hi = x.astype(BF16)
    r1 = x - hi.astype(F32)
    mid = r1.astype(BF16)
    lo = (r1 - mid.astype(F32)).astype(BF16)
    return hi, mid, lo


def _const_spec(shape):
    nd = len(shape)
    return pl.BlockSpec(shape, lambda *_: (0,) * nd, pipeline_mode=pl.Buffered(1))


def _in_proj_kernel(x_ref, g_ref, ws_ref, wzm_ref, wfq_ref, wfk_ref, wfvt_ref, wzf_ref, wga_ref, wgb_ref,
                    small_ref, szm_ref, fq_ref, fk_ref, fvt_ref, szf_ref, ga_ref, gb_ref):
    u = _rms(x_ref[...], g_ref[...]).astype(BF16)

    def mm(w_ref):
        return jnp.dot(u, w_ref[...], preferred_element_type=F32)

    small_ref[...] = mm(ws_ref)
    z = mm(wzm_ref)
    szm_ref[...] = (z * _sigmoid(z)).astype(BF16)
    fq_ref[...] = (mm(wfq_ref) * (FOX_SCALE * LOG2E)).astype(BF16)
    fk_ref[...] = mm(wfk_ref).astype(BF16)
    vt = lax.dot_general(wfvt_ref[...], u, NT_DIMS, preferred_element_type=F32)
    for hp in range(HEAD_PAIRS):
        fvt_ref[hp] = vt[hp * LANE:(hp + 1) * LANE, :].astype(BF16)
    z = mm(wzf_ref)
    szf_ref[...] = (z * _sigmoid(z)).astype(BF16)
    ga_ref[...] = _sigmoid(mm(wga_ref)).astype(BF16)
    gb_ref[...] = _sigmoid(mm(wgb_ref)).astype(BF16)


def _in_proj(x2d, g, weights, tm, vt_shape, vt_spec):
    rows, d = x2d.shape
    width = weights[1].shape[1]
    row_spec = lambda w: pl.BlockSpec((tm, w), lambda i: (i, 0))
    out_shape = [jax.ShapeDtypeStruct((rows, SMALL_W), F32)]
    out_specs = [row_spec(SMALL_W)]
    for name in ("szm", "fq", "fk", "fvt", "szf", "ga", "gb"):
        if name == "fvt":
            out_shape.append(jax.ShapeDtypeStruct(vt_shape, BF16))
            out_specs.append(vt_spec)
        else:
            out_shape.append(jax.ShapeDtypeStruct((rows, width), BF16))
            out_specs.append(row_spec(width))
    return pl.pallas_call(
        _in_proj_kernel,
        grid=(rows // tm,),
        in_specs=[row_spec(d), _const_spec(g.shape)] + [_const_spec(w.shape) for w in weights],
        out_specs=out_specs,
        out_shape=out_shape,
        compiler_params=pltpu.CompilerParams(dimension_semantics=("parallel",), vmem_limit_bytes=VMEM_LIMIT),
        name="in_proj",
    )(x2d, g, *weights)


def _mla_prep_kernel(small_ref, gq_ref, gkv_ref, wqm_ref, wqr_ref, wkm_ref, wvt_ref,
                     cosq_ref, sinq_ref, cosk_ref, sink_ref, q_ref, k_ref, vt_ref):
    sm = small_ref[...]
    cn = _rms(sm[:, 0:MLA_Q_RANK], gq_ref[...]).astype(BF16)
    kn = _rms(sm[:, MLA_Q_RANK:MLA_Q_RANK + MLA_KV_RANK], gkv_ref[...]).astype(BF16)
    k_rope = sm[:, 3 * LANE:4 * LANE] * cosk_ref[...] + sm[:, 4 * LANE:5 * LANE] * sink_ref[...]
    qm = jnp.dot(cn, wqm_ref[...], preferred_element_type=F32)
    qr = jnp.dot(cn, wqr_ref[...], preferred_element_type=F32)
    km = jnp.dot(kn, wkm_ref[...], preferred_element_type=F32)
    cosq = cosq_ref[...]
    sinq = sinq_ref[...]
    for h in range(HEADS):
        sl = slice(h * LANE, (h + 1) * LANE)
        q_ref[:, sl] = (qm[:, sl] * cosq + qr[:, sl] * sinq).astype(BF16)
        k_ref[:, sl] = (km[:, sl] + k_rope).astype(BF16)
    vt = lax.dot_general(wvt_ref[...], kn, NT_DIMS, preferred_element_type=F32)
    for hp in range(HEAD_PAIRS):
        vt_ref[hp] = vt[hp * LANE:(hp + 1) * LANE, :].astype(BF16)


def _mla_prep(small, gq, gkv, weights, tables, tm, tiles_per_seq, vt_shape, vt_spec):
    rows = small.shape[0]
    wide = HEADS * LANE
    tab_spec = pl.BlockSpec((tm, LANE), lambda i: (i % tiles_per_seq, 0))
    return pl.pallas_call(
        _mla_prep_kernel,
        grid=(rows // tm,),
        in_specs=[pl.BlockSpec((tm, SMALL_W), lambda i: (i, 0)), _const_spec(gq.shape), _const_spec(gkv.shape)]
        + [_const_spec(w.shape) for w in weights] + [tab_spec] * 4,
        out_specs=[pl.BlockSpec((tm, wide), lambda i: (i, 0)), pl.BlockSpec((tm, wide), lambda i: (i, 0)), vt_spec],
        out_shape=[jax.ShapeDtypeStruct((rows, wide), BF16), jax.ShapeDtypeStruct((rows, wide), BF16),
                   jax.ShapeDtypeStruct(vt_shape, BF16)],
        compiler_params=pltpu.CompilerParams(dimension_semantics=("parallel",), vmem_limit_bytes=VMEM_LIMIT),
        name="mla_prep",
    )(small, gq, gkv, *weights, *tables)


def _log2_sigmoid(z):
    return (jnp.minimum(z, 0.0) - jnp.log1p(jnp.exp(-jnp.abs(z)))) * LOG2E


def _tri(n):
    row = lax.broadcasted_iota(jnp.int32, (n, n), 0)
    col = lax.broadcasted_iota(jnp.int32, (n, n), 1)
    return jnp.where(col <= row, 1.0, 0.0).astype(BF16)


def _cumsum_rows(tri, x):
    return sum(jnp.dot(tri, part, preferred_element_type=F32) for part in _split3(x))


def _place(x, place_ref):
    return sum(jnp.dot(part, place_ref[k], preferred_element_type=F32)
               for k, part in enumerate(_split3(x))).astype(BF16)


def _decay_kernel(sm_ref, bias_ref, place_ref, ex_ref, crow_ref, beta_ref, *, tile):
    n_tiles = sm_ref.shape[0] // tile
    tri = _tri(tile)
    eye = jnp.where(lax.broadcasted_iota(jnp.int32, (LANE, LANE), 0)
                    == lax.broadcasted_iota(jnp.int32, (LANE, LANE), 1), 1.0, 0.0).astype(BF16)
    carry = jnp.zeros((1, LANE), F32)
    for j in range(n_tiles):
        rows = slice(j * tile, (j + 1) * tile)
        within = _cumsum_rows(tri, _log2_sigmoid(sm_ref[rows, :] + bias_ref[...]))
        ex_ref[rows, :] = _place(-within, place_ref)
        cum = within + carry
        cum_t = sum(lax.dot_general(eye, part, NT_DIMS, preferred_element_type=F32) for part in _split3(cum))
        crow_ref[:, rows] = cum_t[0:HEADS, :]
        beta_ref[j:j + 1, :] = carry
        carry = cum[tile - 1:tile, :]


def _decay_meta_kernel(sm_ref, bias_ref, place_ref, ex_ref):
    n = sm_ref.shape[0]
    cum = _cumsum_rows(_tri(n), _log2_sigmoid(sm_ref[...] + bias_ref[...]))
    ex_ref[...] = _place(cum[n - 1:n, :] - cum, place_ref)


def _decay_prep(small, small_meta, bias_row, place, batch, seq):
    width = place.shape[2]
    ex, crow, beta = pl.pallas_call(
        functools.partial(_decay_kernel, tile=ATT_TILE),
        grid=(batch,),
        in_specs=[pl.BlockSpec((seq, LANE), lambda b: (b, 3)), _const_spec(bias_row.shape), _const_spec(place.shape)],
        out_specs=[pl.BlockSpec((seq, width), lambda b: (b, 0)),
                   pl.BlockSpec((None, HEADS, seq), lambda b: (b, 0, 0)),
                   pl.BlockSpec((None, seq // ATT_TILE, LANE), lambda b: (b, 0, 0))],
        out_shape=[jax.ShapeDtypeStruct((batch * seq, width), BF16),
                   jax.ShapeDtypeStruct((batch, HEADS, seq), F32),
                   jax.ShapeDtypeStruct((batch, seq // ATT_TILE, LANE), F32)],
        compiler_params=pltpu.CompilerParams(dimension_semantics=("parallel",), vmem_limit_bytes=VMEM_LIMIT),
        name="decay_prep",
    )(small, bias_row, place)
    ex_meta = pl.pallas_call(
        _decay_meta_kernel,
        grid=(1,),
        in_specs=[pl.BlockSpec((N_META, LANE), lambda i: (0, 3)), _const_spec(bias_row.shape),
                  _const_spec(place.shape)],
        out_specs=pl.BlockSpec((N_META, width), lambda i: (0, 0)),
        out_shape=jax.ShapeDtypeStruct((N_META, width), BF16),
        name="decay_prep_meta",
    )(small_meta, bias_row, place)
    return ex, crow, beta, ex_meta


class _Chain(NamedTuple):
    q: jax.Array
    k_meta: jax.Array
    vt_meta: jax.Array
    k_tile: Callable
    vt_tile: Callable
    off_meta: Any
    off_tile: Any
    sub: int


def _scores(q_h, k_t):
    return lax.dot_general(k_t, q_h, NT_DIMS, preferred_element_type=F32)


def _col_max(s):
    return jnp.max(s, axis=0, keepdims=True)


def _pv(vt_t, p):
    v_aug = jnp.concatenate([vt_t, jnp.ones((L_ROWS, vt_t.shape[1]), BF16)], axis=0)
    return jnp.dot(v_aug, p, preferred_element_type=F32)


def _shifted(m_tile, off):
    return m_tile if off is None else m_tile + off


def _probs(s, m_new, off):
    return jnp.exp2(s - (m_new if off is None else m_new - off)).astype(BF16)


def _attend(step, chains, s_ref, p_ref, acc_ref, o_ref):
    n = len(chains)
    n_common = Q_SUB * step
    s_meta = [_scores(ch.q, ch.k_meta) for ch in chains]
    tile_max = []
    for c, ch in enumerate(chains):
        s = _scores(ch.q, ch.k_tile(0))
        s_ref[c] = s
        tile_max.append(_col_max(s))
        p_ref[c] = jnp.zeros(p_ref.shape[1:], BF16)
        acc_ref[c] = jnp.zeros(acc_ref.shape[1:], F32)
    ms = tuple(jnp.full((1, ATT_TILE), NEG_INF, F32) for _ in chains)
    alphas = tuple(jnp.ones((1, ATT_TILE), F32) for _ in chains)

    def pending_pv(c, ch, j, alpha):
        acc_ref[c] = alpha * acc_ref[c] + _pv(ch.vt_tile(jnp.maximum(j - 1, 0)), p_ref[c])

    def body(j, carry):
        ms, alphas, tile_max = carry
        cur = [s_ref[c] for c in range(n)]
        for c, ch in enumerate(chains):
            pending_pv(c, ch, j, alphas[c])
        new_ms, new_alphas, probs = [], [], []
        for c, ch in enumerate(chains):
            off = None if ch.off_tile is None else ch.off_tile(j)
            m_new = jnp.maximum(ms[c], _shifted(tile_max[c], off))
            new_alphas.append(jnp.exp2(ms[c] - m_new))
            probs.append(_probs(cur[c], m_new, off))
            new_ms.append(m_new)
        nxt = [_scores(ch.q, ch.k_tile(j + 1)) for ch in chains]
        for c in range(n):
            p_ref[c] = probs[c]
            s_ref[c] = nxt[c]
        return tuple(new_ms), tuple(new_alphas), tuple(_col_max(s) for s in nxt)

    ms, alphas, tile_max = lax.fori_loop(0, n_common, body, (ms, alphas, tuple(tile_max)))

    outs = {}
    for c, ch in enumerate(chains):
        pending_pv(c, ch, n_common, alphas[c])
        m, acc, s, s_max, t = ms[c], acc_ref[c], s_ref[c], tile_max[c], n_common
        for _ in range(ch.sub):
            off = None if ch.off_tile is None else ch.off_tile(t)
            m_new = jnp.maximum(m, _shifted(s_max, off))
            acc = jnp.exp2(m - m_new) * acc + _pv(ch.vt_tile(t), _probs(s, m_new, off))
            m, t = m_new, t + 1
            s = _scores(ch.q, ch.k_tile(t))
            s_max = _col_max(s)
        key = lax.broadcasted_iota(jnp.int32, s.shape, 0)
        qry = lax.broadcasted_iota(jnp.int32, s.shape, 1)
        s = jnp.where(key <= qry, s, NEG_INF)
        off = None if ch.off_tile is None else ch.off_tile(t)
        m_new = jnp.maximum(jnp.maximum(m, _shifted(_col_max(s), off)), _shifted(_col_max(s_meta[c]), ch.off_meta))
        acc = (jnp.exp2(m - m_new) * acc + _pv(ch.vt_tile(t), _probs(s, m_new, off))
               + _pv(ch.vt_meta, _probs(s_meta[c], m_new, ch.off_meta)))
        outs.setdefault(ch.sub, []).append(acc[0:FOX_DIM, :] / acc[FOX_DIM:FOX_DIM + 1, :])
    for u, out_t in outs.items():
        o_ref[u * ATT_TILE:(u + 1) * ATT_TILE, :] = jnp.concatenate(out_t, axis=0).T.astype(BF16)


def _key_rows(j):
    return pl.ds(pl.multiple_of(j * ATT_TILE, ATT_TILE), ATT_TILE)


def _mla_attn_kernel(q_ref, k_ref, vt_ref, km_ref, vmt_ref, o_ref, s_ref, p_ref, acc_ref):
    chains = []
    for u in range(Q_SUB):
        for hh in range(2):
            lanes = slice(hh * LANE, (hh + 1) * LANE)
            rows64 = slice(hh * MLA_V, (hh + 1) * MLA_V)
            chains.append(_Chain(
                q=q_ref[u * ATT_TILE:(u + 1) * ATT_TILE, lanes], k_meta=km_ref[:, lanes], vt_meta=vmt_ref[rows64, :],
                k_tile=lambda j, lanes=lanes: k_ref[_key_rows(j), lanes],
                vt_tile=lambda j, rows64=rows64: vt_ref[j, rows64, :],
                off_meta=None, off_tile=None, sub=u))
    _attend(pl.program_id(2), chains, s_ref, p_ref, acc_ref, o_ref)


def _fox_attn_kernel(beta_ref, q_ref, k_ref, ex_ref, vt_ref, km_ref, exm_ref, vmt_ref, crow_ref, o_ref,
                     kaug_ref, s_ref, p_ref, acc_ref):
    b = pl.program_id(0)
    hp = pl.program_id(1)
    step = pl.program_id(2)
    n_tiles = k_ref.shape[0] // ATT_TILE
    lane = lax.broadcasted_iota(jnp.int32, (1, LANE), 1)
    in_a = lane < FOX_DIM

    @pl.when(step == 0)
    def _():
        def build(c, _):
            rows = _key_rows(c)
            kk = k_ref[rows, :]
            ee = ex_ref[rows, :]
            kaug_ref[0, rows, :] = jnp.where(in_a, kk, ee)
            kaug_ref[1, rows, :] = jnp.where(in_a, ee, kk)
            return 0
        lax.fori_loop(0, n_tiles, build, 0)

    km = km_ref[...]
    exm = exm_ref[...]
    chains = []
    for u in range(Q_SUB):
        cols = slice(u * ATT_TILE, (u + 1) * ATT_TILE)
        q = q_ref[cols, :]
        for hh in range(2):
            own = in_a if hh == 0 else jnp.logical_not(in_a)
            base = FOX_DIM if hh == 0 else 0
            ones = jnp.where((lane >= base) & (lane < base + 3), 1.0, 0.0).astype(BF16)
            rows64 = slice(hh * FOX_DIM, (hh + 1) * FOX_DIM)
            cum_q = crow_ref[hh:hh + 1, cols]
            beta_base = ((b * HEAD_PAIRS + hp) * 2 + hh) * n_tiles
            chains.append(_Chain(
                q=jnp.where(own, q, ones), k_meta=jnp.where(own, km, exm), vt_meta=vmt_ref[rows64, :],
                k_tile=lambda j, hh=hh: kaug_ref[hh, _key_rows(j), :],
                vt_tile=lambda j, rows64=rows64: vt_ref[j, rows64, :],
                off_meta=cum_q,
                off_tile=lambda j, cum_q=cum_q, beta_base=beta_base: cum_q - beta_ref[beta_base + j],
                sub=u))
    _attend(step, chains, s_ref, p_ref, acc_ref, o_ref)


def _attention_specs(batch, seq, width):
    n_tiles = seq // ATT_TILE
    q_spec = pl.BlockSpec((None, Q_SUB * ATT_TILE, width), lambda b, hp, i, *_: (b, i, hp))
    k_spec = pl.BlockSpec((None, seq, width), lambda b, hp, i, *_: (b, 0, hp))
    vt_spec = pl.BlockSpec((None, None, n_tiles, LANE, ATT_TILE), lambda b, hp, i, *_: (b, hp, 0, 0, 0))
    km_spec = pl.BlockSpec((N_META, width), lambda b, hp, i, *_: (0, hp))
    vmt_spec = pl.BlockSpec((None, LANE, N_META), lambda b, hp, i, *_: (hp, 0, 0))
    o_spec = pl.BlockSpec((None, Q_SUB * ATT_TILE, LANE), lambda b, hp, i, *_: (b, i, hp))
    return q_spec, k_spec, vt_spec, km_spec, vmt_spec, o_spec


def _pair_scratch():
    n = 2 * Q_SUB
    return [pltpu.VMEM((n, ATT_TILE, ATT_TILE), F32), pltpu.VMEM((n, ATT_TILE, ATT_TILE), BF16),
            pltpu.VMEM((n, MLA_V + L_ROWS, ATT_TILE), F32)]


def _mla_attention(q, k, vt, km, vmt):
    batch, seq, _ = q.shape
    q_spec, k_spec, vt_spec, km_spec, vmt_spec, o_spec = _attention_specs(batch, seq, 2 * LANE)
    return pl.pallas_call(
        _mla_attn_kernel,
        grid=(batch, HEAD_PAIRS, seq // (Q_SUB * ATT_TILE)),
        in_specs=[q_spec, k_spec, vt_spec, km_spec, vmt_spec],
        out_specs=o_spec,
        out_shape=jax.ShapeDtypeStruct((batch, seq, HEADS * MLA_V), BF16),
        scratch_shapes=_pair_scratch(),
        compiler_params=pltpu.CompilerParams(dimension_semantics=("parallel", "parallel", "arbitrary"),
                                             vmem_limit_bytes=VMEM_LIMIT),
        name="mla_attention",
    )(q, k, vt, km, vmt)


def _fox_attention(beta, q, k, ex, vt, km, exm, vmt, crow):
    batch, seq, _ = q.shape
    q_spec, k_spec, vt_spec, km_spec, vmt_spec, o_spec = _attention_specs(batch, seq, LANE)
    crow_spec = pl.BlockSpec((None, None, 2, Q_SUB * ATT_TILE), lambda b, hp, i, *_: (b, hp, 0, i))
    grid_spec = pltpu.PrefetchScalarGridSpec(
        num_scalar_prefetch=1,
        grid=(batch, HEAD_PAIRS, seq // (Q_SUB * ATT_TILE)),
        in_specs=[q_spec, k_spec, k_spec, vt_spec, km_spec, km_spec, vmt_spec, crow_spec],
        out_specs=o_spec,
        scratch_shapes=[pltpu.VMEM((2, seq, LANE), BF16)] + _pair_scratch(),
    )
    return pl.pallas_call(
        _fox_attn_kernel,
        grid_spec=grid_spec,
        out_shape=jax.ShapeDtypeStruct((batch, seq, HEADS * FOX_DIM), BF16),
        compiler_params=pltpu.CompilerParams(dimension_semantics=("parallel", "parallel", "arbitrary"),
                                             vmem_limit_bytes=VMEM_LIMIT),
        name="fox_attention",
    )(beta, q, k, ex, vt, km, exm, vmt, crow)


def _merge_kernel(om_ref, szm_ref, of_ref, szf_ref, ga_ref, gb_ref, x_ref, wa_ref, wb_ref, wo_ref, g_ref, out_ref):
    ya = jnp.dot(om_ref[...] * szm_ref[...], wa_ref[...], preferred_element_type=F32)
    yb = jnp.dot(of_ref[...] * szf_ref[...], wb_ref[...], preferred_element_type=F32)
    mixed_in = (ga_ref[...].astype(F32) * ya + gb_ref[...].astype(F32) * yb).astype(BF16)
    mixed = jnp.dot(mixed_in, wo_ref[...], preferred_element_type=F32)
    out_ref[...] = x_ref[...] + _rms(mixed, g_ref[...])


def _merge(om, szm, of, szf, ga, gb, x2d, wa, wb, wo, g):
    rows, d = x2d.shape
    row_spec = pl.BlockSpec((ROW_TILE, d), lambda i: (i, 0))
    return pl.pallas_call(
        _merge_kernel,
        grid=(rows // ROW_TILE,),
        in_specs=[row_spec] * 7 + [_const_spec(w.shape) for w in (wa, wb, wo, g)],
        out_specs=row_spec,
        out_shape=jax.ShapeDtypeStruct((rows, d), F32),
        compiler_params=pltpu.CompilerParams(dimension_semantics=("parallel",), vmem_limit_bytes=VMEM_LIMIT),
        name="merge",
    )(om, szm, of, szf, ga, gb, x2d, wa, wb, wo, g)


def _rotate_half_cols(w):
    half = w.shape[-1] // 2
    return jnp.concatenate([-w[..., half:], w[..., :half]], axis=-1)


def _pad_cols(w, before, total):
    return jnp.pad(w, ((0, 0), (before, total - before - w.shape[1])))


def _in_proj_weights(w_in):
    widths = (MLA_Q_RANK, MLA_KV_RANK, MLA_ROPE, HEADS * MLA_V, HEADS * FOX_DIM, HEADS * FOX_DIM,
              HEADS * FOX_DIM, HEADS, HEADS * FOX_DIM, w_in.shape[0], w_in.shape[0])
    assert sum(widths) == w_in.shape[1]
    bounds = np.cumsum((0,) + widths)
    w_cq, w_ckv, w_kpe, w_zm, w_fq, w_fk, w_fv, w_fl, w_zf, w_ga, w_gb = (
        w_in[:, bounds[n]:bounds[n + 1]] for n in range(len(widths)))
    w_small = jnp.concatenate([
        w_cq, w_ckv,
        _pad_cols(w_fl, 0, KPE_LANE), _pad_cols(w_kpe, 0, LANE - KPE_LANE),
        _pad_cols(_rotate_half_cols(w_kpe), KPE_LANE, LANE)], axis=1)
    assert w_small.shape[1] == SMALL_W
    return [w.astype(BF16) for w in (w_small, w_zm, w_fq, w_fk, w_fv.T, w_zf, w_ga, w_gb)]


def _mla_weights(w_uq, w_ukv):
    dq = MLA_NOPE + MLA_ROPE
    wq = w_uq.reshape(MLA_Q_RANK, HEADS, dq)
    pad = LANE - dq
    wq_main = jnp.pad(wq, ((0, 0), (0, 0), (0, pad)))
    wq_rot = jnp.pad(_rotate_half_cols(wq[:, :, MLA_NOPE:]), ((0, 0), (0, 0), (MLA_NOPE, pad)))
    wkv = w_ukv.reshape(MLA_KV_RANK, HEADS, MLA_NOPE + MLA_V)
    wk_main = jnp.pad(wkv[:, :, :MLA_NOPE], ((0, 0), (0, 0), (0, LANE - MLA_NOPE)))
    wv_t = wkv[:, :, MLA_NOPE:].reshape(MLA_KV_RANK, HEADS * MLA_V).T
    flat = lambda w: w.reshape(w.shape[0], HEADS * LANE)
    return [w.astype(BF16) for w in (flat(wq_main), flat(wq_rot), flat(wk_main), wv_t)]


def _rope_tables(first_pos, n_pos):
    half = MLA_ROPE // 2
    inv_freq = ROPE_THETA ** (-np.arange(half, dtype=np.float64) / half)
    ang = (first_pos + np.arange(n_pos, dtype=np.float64))[:, None] * inv_freq[None, :]
    cos = np.concatenate([np.cos(ang), np.cos(ang)], axis=1)
    sin = np.concatenate([np.sin(ang), np.sin(ang)], axis=1)
    pad = np.zeros((n_pos, LANE - MLA_NOPE - MLA_ROPE))
    ones = np.ones((n_pos, MLA_NOPE))
    zeros = np.zeros((n_pos, MLA_NOPE))
    cos_q = np.concatenate([ones, cos, pad], axis=1) * (MLA_SCALE * LOG2E)
    sin_q = np.concatenate([zeros, sin, pad], axis=1) * (MLA_SCALE * LOG2E)
    cos_k = np.concatenate([zeros, cos, pad], axis=1)
    sin_k = np.concatenate([zeros, sin, pad], axis=1)
    return [jnp.asarray(t, F32) for t in (cos_q, sin_q, cos_k, sin_k)]


def _placement():
    place = np.zeros((3, LANE, HEAD_PAIRS * LANE), np.float32)
    for h in range(HEADS):
        base = (h // 2) * LANE + (FOX_DIM if h % 2 == 0 else 0)
        for part in range(3):
            place[part, h, base + part] = 1.0
    return jnp.asarray(place, BF16)


def kernel(x, meta_tokens, pre_norm_g, w_in, fox_forget_b, mla_q_norm_g, mla_kv_norm_g, w_uq, w_ukv,
           w_br_mla, w_br_fox, w_out, post_norm_g):
    batch, seq, d = x.shape
    assert pre_norm_g.shape[0] == 1, "one layer supported"
    assert meta_tokens.shape[0] == N_META and seq % (Q_SUB * ATT_TILE) == 0 and ATT_TILE == ROW_TILE
    n_tiles = seq // ATT_TILE
    x2d = x.reshape(batch * seq, d)

    in_w = _in_proj_weights(w_in[0])
    mla_w = _mla_weights(w_uq[0], w_ukv[0])
    g_pre = pre_norm_g.astype(F32)
    gq = mla_q_norm_g.astype(F32)
    gkv = mla_kv_norm_g.astype(F32)
    bias_row = jnp.pad(fox_forget_b.astype(F32), ((0, 0), (0, LANE - HEADS)))

    vt_shape = (batch, HEAD_PAIRS, n_tiles, LANE, ATT_TILE)
    vt_spec = pl.BlockSpec((None, HEAD_PAIRS, None, LANE, ROW_TILE), lambda i: (i // n_tiles, 0, i % n_tiles, 0, 0))
    vt_meta_shape = (HEAD_PAIRS, LANE, N_META)
    vt_meta_spec = pl.BlockSpec(vt_meta_shape, lambda i: (0, 0, 0))

    small, szm, fq, fk, fvt, szf, ga, gb = _in_proj(x2d, g_pre, in_w, ROW_TILE, vt_shape, vt_spec)
    small_m, _, _, fk_m, fvt_m, _, _, _ = _in_proj(meta_tokens.astype(F32), g_pre, in_w, N_META,
                                                    vt_meta_shape, vt_meta_spec)
    q, k, vt = _mla_prep(small, gq, gkv, mla_w, _rope_tables(N_META, seq), ROW_TILE, n_tiles, vt_shape, vt_spec)
    _, k_m, vt_m = _mla_prep(small_m, gq, gkv, mla_w, _rope_tables(0, N_META), N_META, 1,
                             vt_meta_shape, vt_meta_spec)
    ex, crow, beta, ex_m = _decay_prep(small, small_m, bias_row, _placement(), batch, seq)

    o_mla = _mla_attention(q.reshape(batch, seq, -1), k.reshape(batch, seq, -1), vt, k_m, vt_m)
    beta_flat = jnp.transpose(beta[:, :, :HEADS], (0, 2, 1)).reshape(-1)
    o_fox = _fox_attention(beta_flat, fq.reshape(batch, seq, -1), fk.reshape(batch, seq, -1),
                           ex.reshape(batch, seq, -1), fvt, fk_m, ex_m, fvt_m,
                           crow.reshape(batch, HEAD_PAIRS, 2, seq))

    out = _merge(o_mla.reshape(batch * seq, -1), szm, o_fox.reshape(batch * seq, -1), szf, ga, gb, x2d,
                 w_br_mla[0].astype(BF16), w_br_fox[0].astype(BF16), w_out[0].astype(BF16),
                 post_norm_g.astype(F32))
    return out.reshape(batch, seq, d)
```

```python
import functools
import math
from typing import Any, Callable, NamedTuple

import numpy as np
import jax
import jax.numpy as jnp
from jax import lax
from jax.experimental import pallas as pl
from jax.experimental.pallas import tpu as pltpu

F32 = jnp.float32
BF16 = jnp.bfloat16

N_META = 16
RMS_EPS = 1e-6
HEADS = 16
MLA_Q_RANK = 256
MLA_KV_RANK = 128
MLA_NOPE = 64
MLA_ROPE = 32
MLA_V = 64
MLA_SCALE = 1.0 / math.sqrt(MLA_NOPE + MLA_ROPE)
ROPE_THETA = 10000.0
FOX_DIM = 64
FOX_SCALE = 1.0 / math.sqrt(FOX_DIM)
LOG2E = math.log2(math.e)

LANE = 128
HEAD_PAIRS = HEADS // 2
SMALL_W = 5 * LANE
KPE_LANE = 64
ROW_TILE = 512
ATT_TILE = 512
Q_SUB = 2
L_ROWS = 16
NEG_INF = -1e30
VMEM_LIMIT = 56 * 1024 * 1024

NT_DIMS = (((1,), (1,)), ((), ()))


def _rms(x, g):
    return x * lax.rsqrt(jnp.mean(x * x, axis=-1, keepdims=True) + RMS_EPS) * g


def _sigmoid(x):
    return 1.0 / (1.0 + jnp.exp(-x))


def _split3(x):
    hi = x.astype(BF16)
    r1 = x - hi.astype(F32)
    mid = r1.astype(BF16)
    lo = (r1 - mid.astype(F32)).astype(BF16)
    return hi, mid, lo


def _const_spec(shape):
    nd = len(shape)
    return pl.BlockSpec(shape, lambda *_: (0,) * nd, pipeline_mode=pl.Buffered(1))


def _in_proj_kernel(x_ref, g_ref, ws_ref, wzm_ref, wfq_ref, wfk_ref, wfvt_ref, wzf_ref, wga_ref, wgb_ref,
                    small_ref, szm_ref, fq_ref, fk_ref, fvt_ref, szf_ref, ga_ref, gb_ref):
    u = _rms(x_ref[...], g_ref[...]).astype(BF16)

    def mm(w_ref):
        return lax.dot_general(u, w_ref[...], NT_DIMS, preferred_element_type=F32)

    small_ref[...] = mm(ws_ref)
    z = mm(wzm_ref)
    szm_ref[...] = (z * _sigmoid(z)).astype(BF16)
    fq_ref[...] = (mm(wfq_ref) * (FOX_SCALE * LOG2E)).astype(BF16)
    fk_ref[...] = mm(wfk_ref).astype(BF16)
    vt = lax.dot_general(wfvt_ref[...], u, NT_DIMS, preferred_element_type=F32)
    for hp in range(HEAD_PAIRS):
        fvt_ref[hp] = vt[hp * LANE:(hp + 1) * LANE, :].astype(BF16)
    z = mm(wzf_ref)
    szf_ref[...] = (z * _sigmoid(z)).astype(BF16)
    ga_ref[...] = _sigmoid(mm(wga_ref)).astype(BF16)
    gb_ref[...] = _sigmoid(mm(wgb_ref)).astype(BF16)


def _in_proj(x2d, g, weights, tm, vt_shape, vt_spec):
    rows, d = x2d.shape
    width = weights[1].shape[0]
    row_spec = lambda w: pl.BlockSpec((tm, w), lambda i: (i, 0))
    out_shape = [jax.ShapeDtypeStruct((rows, SMALL_W), F32)]
    out_specs = [row_spec(SMALL_W)]
    for name in ("szm", "fq", "fk", "fvt", "szf", "ga", "gb"):
        if name == "fvt":
            out_shape.append(jax.ShapeDtypeStruct(vt_shape, BF16))
            out_specs.append(vt_spec)
        else:
            out_shape.append(jax.ShapeDtypeStruct((rows, width), BF16))
            out_specs.append(row_spec(width))
    return pl.pallas_call(
        _in_proj_kernel,
        grid=(rows // tm,),
        in_specs=[row_spec(d), _const_spec(g.shape)] + [_const_spec(w.shape) for w in weights],
        out_specs=out_specs,
        out_shape=out_shape,
        compiler_params=pltpu.CompilerParams(dimension_semantics=("parallel",), vmem_limit_bytes=VMEM_LIMIT),
        name="in_proj",
    )(x2d, g, *weights)


def _mla_prep_kernel(small_ref, gq_ref, gkv_ref, wqm_ref, wqr_ref, wkm_ref, wvt_ref,
                     cosq_ref, sinq_ref, cosk_ref, sink_ref, q_ref, k_ref, vt_ref):
    sm = small_ref[...]
    cn = _rms(sm[:, 0:MLA_Q_RANK], gq_ref[...]).astype(BF16)
    kn = _rms(sm[:, MLA_Q_RANK:MLA_Q_RANK + MLA_KV_RANK], gkv_ref[...]).astype(BF16)
    k_rope = sm[:, 3 * LANE:4 * LANE] * cosk_ref[...] + sm[:, 4 * LANE:5 * LANE] * sink_ref[...]
    qm = jnp.dot(cn, wqm_ref[...], preferred_element_type=F32)
    qr = jnp.dot(cn, wqr_ref[...], preferred_element_type=F32)
    km = jnp.dot(kn, wkm_ref[...], preferred_element_type=F32)
    cosq = cosq_ref[...]
    sinq = sinq_ref[...]
    for h in range(HEADS):
        sl = slice(h * LANE, (h + 1) * LANE)
        q_ref[:, sl] = (qm[:, sl] * cosq + qr[:, sl] * sinq).astype(BF16)
        k_ref[:, sl] = (km[:, sl] + k_rope).astype(BF16)
    vt = lax.dot_general(wvt_ref[...], kn, NT_DIMS, preferred_element_type=F32)
    for hp in range(HEAD_PAIRS):
        vt_ref[hp] = vt[hp * LANE:(hp + 1) * LANE, :].astype(BF16)


def _mla_prep(small, gq, gkv, weights, tables, tm, tiles_per_seq, vt_shape, vt_spec):
    rows = small.shape[0]
    wide = HEADS * LANE
    tab_spec = pl.BlockSpec((tm, LANE), lambda i: (i % tiles_per_seq, 0))
    return pl.pallas_call(
        _mla_prep_kernel,
        grid=(rows // tm,),
        in_specs=[pl.BlockSpec((tm, SMALL_W), lambda i: (i, 0)), _const_spec(gq.shape), _const_spec(gkv.shape)]
        + [_const_spec(w.shape) for w in weights] + [tab_spec] * 4,
        out_specs=[pl.BlockSpec((tm, wide), lambda i: (i, 0)), pl.BlockSpec((tm, wide), lambda i: (i, 0)), vt_spec],
        out_shape=[jax.ShapeDtypeStruct((rows, wide), BF16), jax.ShapeDtypeStruct((rows, wide), BF16),
                   jax.ShapeDtypeStruct(vt_shape, BF16)],
        compiler_params=pltpu.CompilerParams(dimension_semantics=("parallel",), vmem_limit_bytes=VMEM_LIMIT),
        name="mla_prep",
    )(small, gq, gkv, *weights, *tables)


def _log2_sigmoid(z):
    return (jnp.minimum(z, 0.0) - jnp.log1p(jnp.exp(-jnp.abs(z)))) * LOG2E


def _tri(n):
    row = lax.broadcasted_iota(jnp.int32, (n, n), 0)
    col = lax.broadcasted_iota(jnp.int32, (n, n), 1)
    return jnp.where(col <= row, 1.0, 0.0).astype(BF16)


def _cumsum_rows(tri, x):
    return sum(jnp.dot(tri, part, preferred_element_type=F32) for part in _split3(x))


def _place(x, place_ref):
    return sum(jnp.dot(part, place_ref[k], preferred_element_type=F32)
               for k, part in enumerate(_split3(x))).astype(BF16)


def _decay_kernel(sm_ref, bias_ref, place_ref, ex_ref, crow_ref, beta_ref, *, tile):
    n_tiles = sm_ref.shape[0] // tile
    tri = _tri(tile)
    eye = jnp.where(lax.broadcasted_iota(jnp.int32, (LANE, LANE), 0)
                    == lax.broadcasted_iota(jnp.int32, (LANE, LANE), 1), 1.0, 0.0).astype(BF16)
    carry = jnp.zeros((1, LANE), F32)
    for j in range(n_tiles):
        rows = slice(j * tile, (j + 1) * tile)
        within = _cumsum_rows(tri, _log2_sigmoid(sm_ref[rows, :] + bias_ref[...]))
        ex_ref[rows, :] = _place(-within, place_ref)
        cum = within + carry
        cum_t = sum(lax.dot_general(eye, part, NT_DIMS, preferred_element_type=F32) for part in _split3(cum))
        crow_ref[:, rows] = cum_t[0:HEADS, :]
        beta_ref[j:j + 1, :] = carry
        carry = cum[tile - 1:tile, :]


def _decay_meta_kernel(sm_ref, bias_ref, place_ref, ex_ref):
    n = sm_ref.shape[0]
    cum = _cumsum_rows(_tri(n), _log2_sigmoid(sm_ref[...] + bias_ref[...]))
    ex_ref[...] = _place(cum[n - 1:n, :] - cum, place_ref)


def _decay_prep(small, small_meta, bias_row, place, batch, seq):
    width = place.shape[2]
    ex, crow, beta = pl.pallas_call(
        functools.partial(_decay_kernel, tile=ATT_TILE),
        grid=(batch,),
        in_specs=[pl.BlockSpec((seq, LANE), lambda b: (b, 3)), _const_spec(bias_row.shape), _const_spec(place.shape)],
        out_specs=[pl.BlockSpec((seq, width), lambda b: (b, 0)),
                   pl.BlockSpec((None, HEADS, seq), lambda b: (b, 0, 0)),
                   pl.BlockSpec((None, seq // ATT_TILE, LANE), lambda b: (b, 0, 0))],
        out_shape=[jax.ShapeDtypeStruct((batch * seq, width), BF16),
                   jax.ShapeDtypeStruct((batch, HEADS, seq), F32),
                   jax.ShapeDtypeStruct((batch, seq // ATT_TILE, LANE), F32)],
        compiler_params=pltpu.CompilerParams(dimension_semantics=("parallel",), vmem_limit_bytes=VMEM_LIMIT),
        name="decay_prep",
    )(small, bias_row, place)
    ex_meta = pl.pallas_call(
        _decay_meta_kernel,
        grid=(1,),
        in_specs=[pl.BlockSpec((N_META, LANE), lambda i: (0, 3)), _const_spec(bias_row.shape),
                  _const_spec(place.shape)],
        out_specs=pl.BlockSpec((N_META, width), lambda i: (0, 0)),
        out_shape=jax.ShapeDtypeStruct((N_META, width), BF16),
        name="decay_prep_meta",
    )(small_meta, bias_row, place)
    return ex, crow, beta, ex_meta


class _Chain(NamedTuple):
    q: jax.Array
    k_meta: jax.Array
    vt_meta: jax.Array
    k_tile: Callable
    vt_tile: Callable
    off_meta: Any
    off_tile: Any
    sub: int


def _scores(q_h, k_t):
    return lax.dot_general(k_t, q_h, NT_DIMS, preferred_element_type=F32)


def _col_max(s):
    return jnp.max(s, axis=0, keepdims=True)


def _pv(vt_t, p):
    v_aug = jnp.concatenate([vt_t, jnp.ones((L_ROWS, vt_t.shape[1]), BF16)], axis=0)
    return jnp.dot(v_aug, p, preferred_element_type=F32)


def _shifted(m_tile, off):
    return m_tile if off is None else m_tile + off


def _probs(s, m_new, off):
    return jnp.exp2(s - (m_new if off is None else m_new - off)).astype(BF16)


def _causal(shape, first_query):
    return (lax.broadcasted_iota(jnp.int32, shape, 0)
            <= lax.broadcasted_iota(jnp.int32, shape, 1) + first_query)


def _attend(step, chains, s_ref, p_ref, acc_ref, o_ref):
    n = len(chains)
    n_common = Q_SUB * step
    s_meta = [_scores(ch.q, ch.k_meta) for ch in chains]
    tile_max = []
    for c, ch in enumerate(chains):
        s = _scores(ch.q, ch.k_tile(0))
        s_ref[c] = s
        tile_max.append(_col_max(s))
        p_ref[c] = jnp.zeros(p_ref.shape[1:], BF16)
        acc_ref[c] = jnp.zeros(acc_ref.shape[1:], F32)
    ms = tuple(jnp.full((1, ATT_TILE), NEG_INF, F32) for _ in chains)
    alphas = tuple(jnp.ones((1, ATT_TILE), F32) for _ in chains)

    def pending_pv(c, ch, j, alpha):
        acc_ref[c] = alpha * acc_ref[c] + _pv(ch.vt_tile(jnp.maximum(j - 1, 0)), p_ref[c])

    def body(j, carry):
        ms, alphas, tile_max = carry
        cur = [s_ref[c] for c in range(n)]
        for c, ch in enumerate(chains):
            pending_pv(c, ch, j, alphas[c])
        new_ms, new_alphas, probs = [], [], []
        for c, ch in enumerate(chains):
            off = None if ch.off_tile is None else ch.off_tile(j)
            m_new = jnp.maximum(ms[c], _shifted(tile_max[c], off))
            new_alphas.append(jnp.exp2(ms[c] - m_new))
            probs.append(_probs(cur[c], m_new, off))
            new_ms.append(m_new)
        nxt = [_scores(ch.q, ch.k_tile(j + 1)) for ch in chains]
        for c in range(n):
            p_ref[c] = probs[c]
            s_ref[c] = nxt[c]
        return tuple(new_ms), tuple(new_alphas), tuple(_col_max(s) for s in nxt)

    def unrolled(t, carry):
        for u in range(Q_SUB):
            carry = body(Q_SUB * t + u, carry)
        return carry

    ms, alphas, tile_max = lax.fori_loop(0, step, unrolled, (ms, alphas, tuple(tile_max)))

    outs = {}
    for c, ch in enumerate(chains):
        pending_pv(c, ch, n_common, alphas[c])
        m, acc, s, s_max, t = ms[c], acc_ref[c], s_ref[c], tile_max[c], n_common
        for _ in range(ch.sub):
            off = None if ch.off_tile is None else ch.off_tile(t)
            m_new = jnp.maximum(m, _shifted(s_max, off))
            acc = jnp.exp2(m - m_new) * acc + _pv(ch.vt_tile(t), _probs(s, m_new, off))
            m, t = m_new, t + 1
            s = _scores(ch.q, ch.k_tile(t))
            s_max = _col_max(s)
        s = jnp.where(_causal(s.shape, 0), s, NEG_INF)
        off = None if ch.off_tile is None else ch.off_tile(t)
        m_new = jnp.maximum(jnp.maximum(m, _shifted(_col_max(s), off)), _shifted(_col_max(s_meta[c]), ch.off_meta))
        acc = (jnp.exp2(m - m_new) * acc + _pv(ch.vt_tile(t), _probs(s, m_new, off))
               + _pv(ch.vt_meta, _probs(s_meta[c], m_new, ch.off_meta)))
        outs.setdefault(ch.sub, []).append(acc[0:FOX_DIM, :] / acc[FOX_DIM:FOX_DIM + 1, :])
    for u, out_t in outs.items():
        o_ref[u * ATT_TILE:(u + 1) * ATT_TILE, :] = jnp.concatenate(out_t, axis=0).T.astype(BF16)


def _key_rows(j):
    return pl.ds(pl.multiple_of(j * ATT_TILE, ATT_TILE), ATT_TILE)


def _mla_attn_kernel(q_ref, k_ref, vt_ref, km_ref, vmt_ref, o_ref, *scratch):
    chains = []
    for u in range(Q_SUB):
        for hh in range(2):
            lanes = slice(hh * LANE, (hh + 1) * LANE)
            rows64 = slice(hh * MLA_V, (hh + 1) * MLA_V)
            chains.append(_Chain(
                q=q_ref[u * ATT_TILE:(u + 1) * ATT_TILE, lanes], k_meta=km_ref[:, lanes], vt_meta=vmt_ref[rows64, :],
                k_tile=lambda j, lanes=lanes: k_ref[_key_rows(j), lanes],
                vt_tile=lambda j, rows64=rows64: vt_ref[j, rows64, :],
                off_meta=None, off_tile=None, sub=u))
    _attend(pl.program_id(2), chains, *scratch, o_ref)


def _fox_attn_kernel(beta_ref, q_ref, k_ref, ex_ref, spread_ref, vt_ref, km_ref, exm_ref, vmt_ref, crow_ref, o_ref,
                     kaug_ref, *scratch):
    b = pl.program_id(0)
    hp = pl.program_id(1)
    step = pl.program_id(2)
    n_tiles = k_ref.shape[0] // ATT_TILE
    lane = lax.broadcasted_iota(jnp.int32, (1, LANE), 1)
    in_a = lane < FOX_DIM

    def spread(packed):
        return jnp.dot(packed, spread_ref[...], preferred_element_type=F32).astype(BF16)

    @pl.when(step == 0)
    def _():
        def build(c, _):
            rows = _key_rows(c)
            kk = k_ref[rows, :]
            ee = spread(ex_ref[rows, :])
            kaug_ref[0, rows, :] = jnp.where(in_a, kk, ee)
            kaug_ref[1, rows, :] = jnp.where(in_a, ee, kk)
            return 0
        lax.fori_loop(0, n_tiles, build, 0)

    km = km_ref[...]
    exm = spread(exm_ref[...])

    chains = []
    for u in range(Q_SUB):
        cols = slice(u * ATT_TILE, (u + 1) * ATT_TILE)
        q = q_ref[cols, :]
        for hh in range(2):
            own = in_a if hh == 0 else jnp.logical_not(in_a)
            base = FOX_DIM if hh == 0 else 0
            ones = jnp.where((lane >= base) & (lane < base + 3), 1.0, 0.0).astype(BF16)
            rows64 = slice(hh * FOX_DIM, (hh + 1) * FOX_DIM)
            cum_q = crow_ref[pl.ds(2 * hp + hh, 1), cols]
            beta_base = b * n_tiles * LANE + 2 * hp + hh
            chains.append(_Chain(
                q=jnp.where(own, q, ones), k_meta=jnp.where(own, km, exm), vt_meta=vmt_ref[rows64, :],
                k_tile=lambda j, hh=hh: kaug_ref[hh, _key_rows(j), :],
                vt_tile=lambda j, rows64=rows64: vt_ref[j, rows64, :],
                off_meta=cum_q,
                off_tile=lambda j, cum_q=cum_q, beta_base=beta_base: cum_q - beta_ref[beta_base + j * LANE],
                sub=u))
    _attend(step, chains, *scratch, o_ref)


def _attention_specs(batch, seq, width):
    n_tiles = seq // ATT_TILE
    q_spec = pl.BlockSpec((None, Q_SUB * ATT_TILE, width), lambda b, hp, i, *_: (b, i, hp))
    k_spec = pl.BlockSpec((None, seq, width), lambda b, hp, i, *_: (b, 0, hp))
    vt_spec = pl.BlockSpec((None, None, n_tiles, LANE, ATT_TILE), lambda b, hp, i, *_: (b, hp, 0, 0, 0))
    km_spec = pl.BlockSpec((N_META, width), lambda b, hp, i, *_: (0, hp))
    vmt_spec = pl.BlockSpec((None, LANE, N_META), lambda b, hp, i, *_: (hp, 0, 0))
    o_spec = pl.BlockSpec((None, Q_SUB * ATT_TILE, LANE), lambda b, hp, i, *_: (b, i, hp))
    return q_spec, k_spec, vt_spec, km_spec, vmt_spec, o_spec


def _pair_scratch():
    n = 2 * Q_SUB
    return [pltpu.VMEM((n, ATT_TILE, ATT_TILE), F32), pltpu.VMEM((n, ATT_TILE, ATT_TILE), BF16),
            pltpu.VMEM((n, MLA_V + L_ROWS, ATT_TILE), F32)]


def _mla_attention(q, k, vt, km, vmt):
    batch, seq, _ = q.shape
    q_spec, k_spec, vt_spec, km_spec, vmt_spec, o_spec = _attention_specs(batch, seq, 2 * LANE)
    return pl.pallas_call(
        _mla_attn_kernel,
        grid=(batch, HEAD_PAIRS, seq // (Q_SUB * ATT_TILE)),
        in_specs=[q_spec, k_spec, vt_spec, km_spec, vmt_spec],
        out_specs=o_spec,
        out_shape=jax.ShapeDtypeStruct((batch, seq, HEADS * MLA_V), BF16),
        scratch_shapes=_pair_scratch(),
        compiler_params=pltpu.CompilerParams(dimension_semantics=("parallel", "parallel", "arbitrary"),
                                             vmem_limit_bytes=VMEM_LIMIT),
        name="mla_attention",
    )(q, k, vt, km, vmt)


def _fox_attention(beta, q, k, ex, spread, vt, km, exm, vmt, crow):
    batch, seq, _ = q.shape
    q_spec, k_spec, vt_spec, km_spec, vmt_spec, o_spec = _attention_specs(batch, seq, LANE)
    crow_spec = pl.BlockSpec((None, HEADS, Q_SUB * ATT_TILE), lambda b, hp, i, *_: (b, 0, i))
    ex_spec = pl.BlockSpec((None, seq, LANE), lambda b, hp, i, *_: (b, 0, 0))
    exm_spec = pl.BlockSpec((N_META, LANE), lambda b, hp, i, *_: (0, 0))
    spread_spec = pl.BlockSpec((None, LANE, LANE), lambda b, hp, i, *_: (hp, 0, 0))
    grid_spec = pltpu.PrefetchScalarGridSpec(
        num_scalar_prefetch=1,
        grid=(batch, HEAD_PAIRS, seq // (Q_SUB * ATT_TILE)),
        in_specs=[q_spec, k_spec, ex_spec, spread_spec, vt_spec, km_spec, exm_spec, vmt_spec, crow_spec],
        out_specs=o_spec,
        scratch_shapes=[pltpu.VMEM((2, seq, LANE), BF16)] + _pair_scratch(),
    )
    return pl.pallas_call(
        _fox_attn_kernel,
        grid_spec=grid_spec,
        out_shape=jax.ShapeDtypeStruct((batch, seq, HEADS * FOX_DIM), BF16),
        compiler_params=pltpu.CompilerParams(dimension_semantics=("parallel", "parallel", "arbitrary"),
                                             vmem_limit_bytes=VMEM_LIMIT),
        name="fox_attention",
    )(beta, q, k, ex, spread, vt, km, exm, vmt, crow)


def _merge_kernel(om_ref, szm_ref, of_ref, szf_ref, ga_ref, gb_ref, x_ref, wa_ref, wb_ref, wo_ref, g_ref, out_ref):
    ya = jnp.dot(om_ref[...] * szm_ref[...], wa_ref[...], preferred_element_type=F32)
    yb = jnp.dot(of_ref[...] * szf_ref[...], wb_ref[...], preferred_element_type=F32)
    mixed_in = (ga_ref[...].astype(F32) * ya + gb_ref[...].astype(F32) * yb).astype(BF16)
    mixed = jnp.dot(mixed_in, wo_ref[...], preferred_element_type=F32)
    out_ref[...] = x_ref[...] + _rms(mixed, g_ref[...])


def _merge(om, szm, of, szf, ga, gb, x2d, wa, wb, wo, g):
    rows, d = x2d.shape
    row_spec = pl.BlockSpec((ROW_TILE, d), lambda i: (i, 0))
    return pl.pallas_call(
        _merge_kernel,
        grid=(rows // ROW_TILE,),
        in_specs=[row_spec] * 7 + [_const_spec(w.shape) for w in (wa, wb, wo, g)],
        out_specs=row_spec,
        out_shape=jax.ShapeDtypeStruct((rows, d), F32),
        compiler_params=pltpu.CompilerParams(dimension_semantics=("parallel",), vmem_limit_bytes=VMEM_LIMIT),
        name="merge",
    )(om, szm, of, szf, ga, gb, x2d, wa, wb, wo, g)


def _transpose_kernel(w_ref, o_ref):
    o_ref[...] = w_ref[...].T.astype(BF16)


def _transpose_bf16(w):
    k, n = w.shape
    cols = 2 * LANE
    return pl.pallas_call(
        _transpose_kernel,
        grid=(n // cols,),
        in_specs=[pl.BlockSpec((k, cols), lambda i: (0, i))],
        out_specs=pl.BlockSpec((cols, k), lambda i: (i, 0)),
        out_shape=jax.ShapeDtypeStruct((n, k), BF16),
        name="weight_transpose",
    )(w)


def _rotate_half_cols(w):
    half = w.shape[-1] // 2
    return jnp.concatenate([-w[..., half:], w[..., :half]], axis=-1)


def _pad_cols(w, before, total):
    return jnp.pad(w, ((0, 0), (before, total - before - w.shape[1])))


def _in_proj_weights(w_in):
    widths = (MLA_Q_RANK, MLA_KV_RANK, MLA_ROPE, HEADS * MLA_V, HEADS * FOX_DIM, HEADS * FOX_DIM,
              HEADS * FOX_DIM, HEADS, HEADS * FOX_DIM, w_in.shape[0], w_in.shape[0])
    assert sum(widths) == w_in.shape[1]
    bounds = np.cumsum((0,) + widths)
    wt = w_in.T
    w_cq, w_ckv, w_kpe, w_zm, w_fq, w_fk, w_fv, w_fl, w_zf, w_ga, w_gb = (
        wt[bounds[n]:bounds[n + 1]] for n in range(len(widths)))
    w_small = jnp.concatenate([
        w_cq, w_ckv,
        _pad_cols(w_fl.T, 0, KPE_LANE).T, _pad_cols(w_kpe.T, 0, LANE - KPE_LANE).T,
        _pad_cols(_rotate_half_cols(w_kpe.T), KPE_LANE, LANE).T], axis=0)
    assert w_small.shape[0] == SMALL_W
    return [w.astype(BF16) for w in (w_small, w_zm, w_fq, w_fk, w_fv, w_zf, w_ga, w_gb)]


def _head_blocks(w, per_head, first, last, at, width, fn=lambda blk: blk):
    blocks = [_pad_cols(fn(w[:, h * per_head + first:h * per_head + last]), at, width) for h in range(HEADS)]
    return jnp.concatenate(blocks, axis=1)


def _mla_weights(w_uq, w_ukv):
    dq = MLA_NOPE + MLA_ROPE
    dkv = MLA_NOPE + MLA_V
    wq_main = _head_blocks(w_uq, dq, 0, dq, 0, LANE)
    wq_rot = _head_blocks(w_uq, dq, MLA_NOPE, dq, MLA_NOPE, LANE, _rotate_half_cols)
    wk_main = _head_blocks(w_ukv, dkv, 0, MLA_NOPE, 0, LANE)
    wv_t = _transpose_bf16(_head_blocks(w_ukv, dkv, MLA_NOPE, dkv, 0, MLA_V))
    return [w.astype(BF16) for w in (wq_main, wq_rot, wk_main)] + [wv_t]


def _rope_tables(first_pos, n_pos):
    half = MLA_ROPE // 2
    inv_freq = ROPE_THETA ** (-np.arange(half, dtype=np.float64) / half)
    ang = (first_pos + np.arange(n_pos, dtype=np.float64))[:, None] * inv_freq[None, :]
    cos = np.concatenate([np.cos(ang), np.cos(ang)], axis=1)
    sin = np.concatenate([np.sin(ang), np.sin(ang)], axis=1)
    pad = np.zeros((n_pos, LANE - MLA_NOPE - MLA_ROPE))
    ones = np.ones((n_pos, MLA_NOPE))
    zeros = np.zeros((n_pos, MLA_NOPE))
    cos_q = np.concatenate([ones, cos, pad], axis=1) * (MLA_SCALE * LOG2E)
    sin_q = np.concatenate([zeros, sin, pad], axis=1) * (MLA_SCALE * LOG2E)
    cos_k = np.concatenate([zeros, cos, pad], axis=1)
    sin_k = np.concatenate([zeros, sin, pad], axis=1)
    return [jnp.asarray(t, F32) for t in (cos_q, sin_q, cos_k, sin_k)]


def _placement():
    pack = np.zeros((3, LANE, LANE), np.float32)
    spread = np.zeros((HEAD_PAIRS, LANE, LANE), np.float32)
    for h in range(HEADS):
        base = FOX_DIM if h % 2 == 0 else 0
        for part in range(3):
            pack[part, h, HEADS * part + h] = 1.0
            spread[h // 2, HEADS * part + h, base + part] = 1.0
    return jnp.asarray(pack, BF16), jnp.asarray(spread, BF16)


def kernel(x, meta_tokens, pre_norm_g, w_in, fox_forget_b, mla_q_norm_g, mla_kv_norm_g, w_uq, w_ukv,
           w_br_mla, w_br_fox, w_out, post_norm_g):
    batch, seq, d = x.shape
    assert pre_norm_g.shape[0] == 1, "one layer supported"
    assert meta_tokens.shape[0] == N_META and seq % (Q_SUB * ATT_TILE) == 0 and ATT_TILE == ROW_TILE
    n_tiles = seq // ATT_TILE
    x2d = x.reshape(batch * seq, d)

    in_w = _in_proj_weights(w_in[0])
    mla_w = _mla_weights(w_uq[0], w_ukv[0])
    g_pre = pre_norm_g.astype(F32)
    gq = mla_q_norm_g.astype(F32)
    gkv = mla_kv_norm_g.astype(F32)
    bias_row = jnp.pad(fox_forget_b.astype(F32), ((0, 0), (0, LANE - HEADS)))

    vt_shape = (batch, HEAD_PAIRS, n_tiles, LANE, ATT_TILE)
    vt_spec = pl.BlockSpec((None, HEAD_PAIRS, None, LANE, ROW_TILE), lambda i: (i // n_tiles, 0, i % n_tiles, 0, 0))
    vt_meta_shape = (HEAD_PAIRS, LANE, N_META)
    vt_meta_spec = pl.BlockSpec(vt_meta_shape, lambda i: (0, 0, 0))

    small, szm, fq, fk, fvt, szf, ga, gb = _in_proj(x2d, g_pre, in_w, ROW_TILE, vt_shape, vt_spec)
    small_m, _, _, fk_m, fvt_m, _, _, _ = _in_proj(meta_tokens.astype(F32), g_pre, in_w, N_META,
                                                    vt_meta_shape, vt_meta_spec)
    q, k, vt = _mla_prep(small, gq, gkv, mla_w, _rope_tables(N_META, seq), ROW_TILE, n_tiles, vt_shape, vt_spec)
    _, k_m, vt_m = _mla_prep(small_m, gq, gkv, mla_w, _rope_tables(0, N_META), N_META, 1,
                             vt_meta_shape, vt_meta_spec)
    pack, spread = _placement()
    ex, crow, beta, ex_m = _decay_prep(small, small_m, bias_row, pack, batch, seq)

    o_mla = _mla_attention(q.reshape(batch, seq, -1), k.reshape(batch, seq, -1), vt, k_m, vt_m)
    o_fox = _fox_attention(beta.reshape(-1),fq.reshape(batch, seq, -1), fk.reshape(batch, seq, -1),
                           ex.reshape(batch, seq, -1), spread, fvt, fk_m, ex_m, fvt_m,
                           crow)

    out = _merge(o_mla.reshape(batch * seq, -1), szm, o_fox.reshape(batch * seq, -1), szf, ga, gb, x2d,
                 w_br_mla[0].astype(BF16), w_br_fox[0].astype(BF16), w_out[0].astype(BF16),
                 post_norm_g.astype(F32))
    return out.reshape(batch, seq, d)
```

```python
import functools
import math
from typing import Any, Callable, NamedTuple

import numpy as np
import jax
import jax.numpy as jnp
from jax import lax
from jax.experimental import pallas as pl
from jax.experimental.pallas import tpu as pltpu

F32 = jnp.float32
BF16 = jnp.bfloat16

N_META = 16
RMS_EPS = 1e-6
HEADS = 16
MLA_Q_RANK = 256
MLA_KV_RANK = 128
MLA_NOPE = 64
MLA_ROPE = 32
MLA_V = 64
MLA_SCALE = 1.0 / math.sqrt(MLA_NOPE + MLA_ROPE)
ROPE_THETA = 10000.0
FOX_DIM = 64
FOX_SCALE = 1.0 / math.sqrt(FOX_DIM)
LOG2E = math.log2(math.e)

LANE = 128
HEAD_PAIRS = HEADS // 2
SMALL_W = 5 * LANE
KPE_LANE = 64
ROW_TILE = 512
ATT_TILE = 512
Q_SUB = 2
L_ROWS = 16
NEG_INF = -1e30
VMEM_LIMIT = 56 * 1024 * 1024

NT_DIMS = (((1,), (1,)), ((), ()))


def _rms(x, g):
    return x * lax.rsqrt(jnp.mean(x * x, axis=-1, keepdims=True) + RMS_EPS) * g


def _sigmoid(x):
    return 1.0 / (1.0 + jnp.exp(-x))


def _split3(x):
    hi = x.astype(BF16)
    r1 = x - hi.astype(F32)
    mid = r1.astype(BF16)
    lo = (r1 - mid.astype(F32)).astype(BF16)
    return hi, mid, lo


def _const_spec(shape):
    nd = len(shape)
    return pl.BlockSpec(shape, lambda *_: (0,) * nd, pipeline_mode=pl.Buffered(1))


def _in_proj_kernel(x_ref, g_ref, ws_ref, wzm_ref, wfq_ref, wfk_ref, wfvt_ref, wzf_ref, wga_ref, wgb_ref,
                    gq_ref, gkv_ref, wqm_ref, wqr_ref, wkm_ref, wvt_ref, cosq_ref, sinq_ref, cosk_ref, sink_ref,
                    fl_ref, szm_ref, fq_ref, fk_ref, fvt_ref, szf_ref, ga_ref, gb_ref, q_ref, k_ref, vt_ref):
    u = _rms(x_ref[...], g_ref[...]).astype(BF16)

    def mm(w_ref):
        return lax.dot_general(u, w_ref[...], NT_DIMS, preferred_element_type=F32)

    def store_pairs(out_ref, rows_t):
        for hp in range(HEAD_PAIRS):
            out_ref[hp] = rows_t[hp * LANE:(hp + 1) * LANE, :].astype(BF16)

    z = mm(wzm_ref)
    szm_ref[...] = (z * _sigmoid(z)).astype(BF16)
    fq_ref[...] = (mm(wfq_ref) * (FOX_SCALE * LOG2E)).astype(BF16)
    fk_ref[...] = mm(wfk_ref).astype(BF16)
    store_pairs(fvt_ref, lax.dot_general(wfvt_ref[...], u, NT_DIMS, preferred_element_type=F32))
    z = mm(wzf_ref)
    szf_ref[...] = (z * _sigmoid(z)).astype(BF16)
    ga_ref[...] = _sigmoid(mm(wga_ref)).astype(BF16)
    gb_ref[...] = _sigmoid(mm(wgb_ref)).astype(BF16)

    sm = mm(ws_ref)
    fl_ref[...] = sm[:, 3 * LANE:4 * LANE]
    cn = _rms(sm[:, 0:MLA_Q_RANK], gq_ref[...]).astype(BF16)
    kn = _rms(sm[:, MLA_Q_RANK:MLA_Q_RANK + MLA_KV_RANK], gkv_ref[...]).astype(BF16)
    k_rope = sm[:, 3 * LANE:4 * LANE] * cosk_ref[...] + sm[:, 4 * LANE:5 * LANE] * sink_ref[...]
    qm = jnp.dot(cn, wqm_ref[...], preferred_element_type=F32)
    qr = jnp.dot(cn, wqr_ref[...], preferred_element_type=F32)
    km = jnp.dot(kn, wkm_ref[...], preferred_element_type=F32)
    cosq = cosq_ref[...]
    sinq = sinq_ref[...]
    for h in range(HEADS):
        sl = slice(h * LANE, (h + 1) * LANE)
        q_ref[:, sl] = (qm[:, sl] * cosq + qr[:, sl] * sinq).astype(BF16)
        k_ref[:, sl] = (km[:, sl] + k_rope).astype(BF16)
    store_pairs(vt_ref, lax.dot_general(wvt_ref[...], kn, NT_DIMS, preferred_element_type=F32))


def _in_proj(x2d, g, weights, gq, gkv, mla_weights, tables, tm, tiles_per_seq, vt_shape, vt_spec):
    rows, d = x2d.shape
    width = weights[1].shape[0]
    row_spec = lambda w: pl.BlockSpec((tm, w), lambda i: (i, 0))
    tab_spec = pl.BlockSpec((tm, LANE), lambda i: (i % tiles_per_seq, 0))
    wide = HEADS * LANE
    outs = [((rows, LANE), F32, row_spec(LANE))]
    for name in ("szm", "fq", "fk", "fvt", "szf", "ga", "gb"):
        outs.append((vt_shape, BF16, vt_spec) if name == "fvt" else ((rows, width), BF16, row_spec(width)))
    outs += [((rows, wide), BF16, row_spec(wide)), ((rows, wide), BF16, row_spec(wide)), (vt_shape, BF16, vt_spec)]
    consts = [g] + list(weights) + [gq, gkv] + list(mla_weights)
    return pl.pallas_call(
        _in_proj_kernel,
        grid=(rows // tm,),
        in_specs=[row_spec(d)] + [_const_spec(c.shape) for c in consts] + [tab_spec] * 4,
        out_specs=[spec for _, _, spec in outs],
        out_shape=[jax.ShapeDtypeStruct(shape, dtype) for shape, dtype, _ in outs],
        compiler_params=pltpu.CompilerParams(dimension_semantics=("parallel",), vmem_limit_bytes=VMEM_LIMIT),
        name="in_proj",
    )(x2d, *consts, *tables)


def _log2_sigmoid(z):
    return (jnp.minimum(z, 0.0) - jnp.log1p(jnp.exp(-jnp.abs(z)))) * LOG2E


def _tri(n):
    row = lax.broadcasted_iota(jnp.int32, (n, n), 0)
    col = lax.broadcasted_iota(jnp.int32, (n, n), 1)
    return jnp.where(col <= row, 1.0, 0.0).astype(BF16)


def _cumsum_rows(tri, x):
    return sum(jnp.dot(tri, part, preferred_element_type=F32) for part in _split3(x))


def _place(x, place_ref):
    return sum(jnp.dot(part, place_ref[k], preferred_element_type=F32)
               for k, part in enumerate(_split3(x))).astype(BF16)


def _decay_kernel(sm_ref, bias_ref, place_ref, ex_ref, crow_ref, beta_ref, *, tile):
    n_tiles = sm_ref.shape[0] // tile
    tri = _tri(tile)
    eye = jnp.where(lax.broadcasted_iota(jnp.int32, (LANE, LANE), 0)
                    == lax.broadcasted_iota(jnp.int32, (LANE, LANE), 1), 1.0, 0.0).astype(BF16)
    carry = jnp.zeros((1, LANE), F32)
    for j in range(n_tiles):
        rows = slice(j * tile, (j + 1) * tile)
        within = _cumsum_rows(tri, _log2_sigmoid(sm_ref[rows, :] + bias_ref[...]))
        ex_ref[rows, :] = _place(-within, place_ref)
        cum = within + carry
        cum_t = sum(lax.dot_general(eye, part, NT_DIMS, preferred_element_type=F32) for part in _split3(cum))
        crow_ref[:, rows] = cum_t[0:HEADS, :]
        beta_ref[j:j + 1, :] = carry
        carry = cum[tile - 1:tile, :]


def _decay_meta_kernel(sm_ref, bias_ref, place_ref, ex_ref):
    n = sm_ref.shape[0]
    cum = _cumsum_rows(_tri(n), _log2_sigmoid(sm_ref[...] + bias_ref[...]))
    ex_ref[...] = _place(cum[n - 1:n, :] - cum, place_ref)


def _decay_prep(fl, fl_meta, bias_row, place, batch, seq):
    width = place.shape[2]
    ex, crow, beta = pl.pallas_call(
        functools.partial(_decay_kernel, tile=ATT_TILE),
        grid=(batch,),
        in_specs=[pl.BlockSpec((seq, LANE), lambda b: (b, 0)), _const_spec(bias_row.shape), _const_spec(place.shape)],
        out_specs=[pl.BlockSpec((seq, width), lambda b: (b, 0)),
                   pl.BlockSpec((None, HEADS, seq), lambda b: (b, 0, 0)),
                   pl.BlockSpec((None, seq // ATT_TILE, LANE), lambda b: (b, 0, 0))],
        out_shape=[jax.ShapeDtypeStruct((batch * seq, width), BF16),
                   jax.ShapeDtypeStruct((batch, HEADS, seq), F32),
                   jax.ShapeDtypeStruct((batch, seq // ATT_TILE, LANE), F32)],
        compiler_params=pltpu.CompilerParams(dimension_semantics=("parallel",), vmem_limit_bytes=VMEM_LIMIT),
        name="decay_prep",
    )(fl, bias_row, place)
    ex_meta = pl.pallas_call(
        _decay_meta_kernel,
        grid=(1,),
        in_specs=[pl.BlockSpec((N_META, LANE), lambda i: (0, 0)), _const_spec(bias_row.shape),
                  _const_spec(place.shape)],
        out_specs=pl.BlockSpec((N_META, width), lambda i: (0, 0)),
        out_shape=jax.ShapeDtypeStruct((N_META, width), BF16),
        name="decay_prep_meta",
    )(fl_meta, bias_row, place)
    return ex, crow, beta, ex_meta


class _Chain(NamedTuple):
    q: jax.Array
    k_meta: jax.Array
    vt_meta: jax.Array
    k_tile: Callable
    vt_tile: Callable
    off_meta: Any
    off_tile: Any
    sub: int


def _scores(q_h, k_t):
    return lax.dot_general(k_t, q_h, NT_DIMS, preferred_element_type=F32)


def _col_max(s):
    return jnp.max(s, axis=0, keepdims=True)


def _pv(vt_t, p):
    v_aug = jnp.concatenate([vt_t, jnp.ones((L_ROWS, vt_t.shape[1]), BF16)], axis=0)
    return jnp.dot(v_aug, p, preferred_element_type=F32)


def _shifted(m_tile, off):
    return m_tile if off is None else m_tile + off


def _probs(s, m_new, off):
    return jnp.exp2(s - (m_new if off is None else m_new - off)).astype(BF16)


def _causal(shape, first_query):
    return (lax.broadcasted_iota(jnp.int32, shape, 0)
            <= lax.broadcasted_iota(jnp.int32, shape, 1) + first_query)


def _attend(step, chains, s_ref, p_ref, acc_ref, o_ref):
    n = len(chains)
    n_common = Q_SUB * step

    @pl.when(step == 0)
    def _():
        p_ref[...] = jnp.zeros(p_ref.shape, BF16)

    s_meta = [_scores(ch.q, ch.k_meta) for ch in chains]
    tile_max = []
    for c, ch in enumerate(chains):
        s = _scores(ch.q, ch.k_tile(0))
        s_ref[c] = s
        tile_max.append(_col_max(s))
        acc_ref[c] = jnp.zeros(acc_ref.shape[1:], F32)
    ms = tuple(jnp.full((1, ATT_TILE), NEG_INF, F32) for _ in chains)
    alphas = tuple(jnp.ones((1, ATT_TILE), F32) for _ in chains)

    def pending_pv(c, ch, j, alpha):
        pending = jnp.where(j > 0, 1.0, 0.0)
        acc_ref[c] = alpha * acc_ref[c] + pending * _pv(ch.vt_tile(jnp.maximum(j - 1, 0)), p_ref[c])

    def body(j, carry):
        ms, alphas, tile_max = carry
        cur = [s_ref[c] for c in range(n)]
        for c, ch in enumerate(chains):
            pending_pv(c, ch, j, alphas[c])
        new_ms, new_alphas, probs = [], [], []
        for c, ch in enumerate(chains):
            off = None if ch.off_tile is None else ch.off_tile(j)
            m_new = jnp.maximum(ms[c], _shifted(tile_max[c], off))
            new_alphas.append(jnp.exp2(ms[c] - m_new))
            probs.append(_probs(cur[c], m_new, off))
            new_ms.append(m_new)
        nxt = [_scores(ch.q, ch.k_tile(j + 1)) for ch in chains]
        for c in range(n):
            p_ref[c] = probs[c]
            s_ref[c] = nxt[c]
        return tuple(new_ms), tuple(new_alphas), tuple(_col_max(s) for s in nxt)

    def unrolled(t, carry):
        for u in range(Q_SUB):
            carry = body(Q_SUB * t + u, carry)
        return carry

    ms, alphas, tile_max = lax.fori_loop(0, step, unrolled, (ms, alphas, tuple(tile_max)))

    outs = {}
    for c, ch in enumerate(chains):
        pending_pv(c, ch, n_common, alphas[c])
        m, acc, s, s_max, t = ms[c], acc_ref[c], s_ref[c], tile_max[c], n_common
        for _ in range(ch.sub):
            off = None if ch.off_tile is None else ch.off_tile(t)
            m_new = jnp.maximum(m, _shifted(s_max, off))
            acc = jnp.exp2(m - m_new) * acc + _pv(ch.vt_tile(t), _probs(s, m_new, off))
            m, t = m_new, t + 1
            s = _scores(ch.q, ch.k_tile(t))
            s_max = _col_max(s)
        s = jnp.where(_causal(s.shape, 0), s, NEG_INF)
        off = None if ch.off_tile is None else ch.off_tile(t)
        m_new = jnp.maximum(jnp.maximum(m, _shifted(_col_max(s), off)), _shifted(_col_max(s_meta[c]), ch.off_meta))
        acc = (jnp.exp2(m - m_new) * acc + _pv(ch.vt_tile(t), _probs(s, m_new, off))
               + _pv(ch.vt_meta, _probs(s_meta[c], m_new, ch.off_meta)))
        outs.setdefault(ch.sub, []).append(acc[0:FOX_DIM, :] / acc[FOX_DIM:FOX_DIM + 1, :])
    for u, out_t in outs.items():
        o_ref[u * ATT_TILE:(u + 1) * ATT_TILE, :] = jnp.concatenate(out_t, axis=0).T.astype(BF16)


def _key_rows(j):
    return pl.ds(pl.multiple_of(j * ATT_TILE, ATT_TILE), ATT_TILE)


def _mla_attn_kernel(q_ref, k_ref, vt_ref, km_ref, vmt_ref, o_ref, *scratch):
    chains = []
    for u in range(Q_SUB):
        for hh in range(2):
            lanes = slice(hh * LANE, (hh + 1) * LANE)
            rows64 = slice(hh * MLA_V, (hh + 1) * MLA_V)
            chains.append(_Chain(
                q=q_ref[u * ATT_TILE:(u + 1) * ATT_TILE, lanes], k_meta=km_ref[:, lanes], vt_meta=vmt_ref[rows64, :],
                k_tile=lambda j, lanes=lanes: k_ref[_key_rows(j), lanes],
                vt_tile=lambda j, rows64=rows64: vt_ref[j, rows64, :],
                off_meta=None, off_tile=None, sub=u))
    _attend(pl.program_id(2), chains, *scratch, o_ref)


def _fox_attn_kernel(beta_ref, q_ref, k_ref, ex_ref, spread_ref, vt_ref, km_ref, exm_ref, vmt_ref, crow_ref, o_ref,
                     kaug_ref, *scratch):
    b = pl.program_id(0)
    hp = pl.program_id(1)
    step = pl.program_id(2)
    n_tiles = k_ref.shape[0] // ATT_TILE
    lane = lax.broadcasted_iota(jnp.int32, (1, LANE), 1)
    in_a = lane < FOX_DIM

    def spread(packed):
        return jnp.dot(packed, spread_ref[...], preferred_element_type=F32).astype(BF16)

    @pl.when(step == 0)
    def _():
        for c in range(n_tiles):
            rows = slice(c * ATT_TILE, (c + 1) * ATT_TILE)
            kk = k_ref[rows, :]
            ee = spread(ex_ref[rows, :])
            kaug_ref[0, rows, :] = jnp.where(in_a, kk, ee)
            kaug_ref[1, rows, :] = jnp.where(in_a, ee, kk)

    km = km_ref[...]
    exm = spread(exm_ref[...])

    chains = []
    for u in range(Q_SUB):
        cols = slice(u * ATT_TILE, (u + 1) * ATT_TILE)
        q = q_ref[cols, :]
        for hh in range(2):
            own = in_a if hh == 0 else jnp.logical_not(in_a)
            base = FOX_DIM if hh == 0 else 0
            ones = jnp.where((lane >= base) & (lane < base + 3), 1.0, 0.0).astype(BF16)
            rows64 = slice(hh * FOX_DIM, (hh + 1) * FOX_DIM)
            cum_q = crow_ref[pl.ds(2 * hp + hh, 1), cols]
            beta_base = b * n_tiles * LANE + 2 * hp + hh
            chains.append(_Chain(
                q=jnp.where(own, q, ones), k_meta=jnp.where(own, km, exm), vt_meta=vmt_ref[rows64, :],
                k_tile=lambda j, hh=hh: kaug_ref[hh, _key_rows(j), :],
                vt_tile=lambda j, rows64=rows64: vt_ref[j, rows64, :],
                off_meta=cum_q,
                off_tile=lambda j, cum_q=cum_q, beta_base=beta_base: cum_q - beta_ref[beta_base + j * LANE],
                sub=u))
    _attend(step, chains, *scratch, o_ref)


def _attention_specs(batch, seq, width):
    n_tiles = seq // ATT_TILE
    q_spec = pl.BlockSpec((None, Q_SUB * ATT_TILE, width), lambda b, hp, i, *_: (b, i, hp))
    k_spec = pl.BlockSpec((None, seq, width), lambda b, hp, i, *_: (b, 0, hp))
    vt_spec = pl.BlockSpec((None, None, n_tiles, LANE, ATT_TILE), lambda b, hp, i, *_: (b, hp, 0, 0, 0))
    km_spec = pl.BlockSpec((N_META, width), lambda b, hp, i, *_: (0, hp))
    vmt_spec = pl.BlockSpec((None, LANE, N_META), lambda b, hp, i, *_: (hp, 0, 0))
    o_spec = pl.BlockSpec((None, Q_SUB * ATT_TILE, LANE), lambda b, hp, i, *_: (b, i, hp))
    return q_spec, k_spec, vt_spec, km_spec, vmt_spec, o_spec


def _pair_scratch():
    n = 2 * Q_SUB
    return [pltpu.VMEM((n, ATT_TILE, ATT_TILE), F32), pltpu.VMEM((n, ATT_TILE, ATT_TILE), BF16),
            pltpu.VMEM((n, MLA_V + L_ROWS, ATT_TILE), F32)]


def _mla_attention(q, k, vt, km, vmt):
    batch, seq, _ = q.shape
    q_spec, k_spec, vt_spec, km_spec, vmt_spec, o_spec = _attention_specs(batch, seq, 2 * LANE)
    return pl.pallas_call(
        _mla_attn_kernel,
        grid=(batch, HEAD_PAIRS, seq // (Q_SUB * ATT_TILE)),
        in_specs=[q_spec, k_spec, vt_spec, km_spec, vmt_spec],
        out_specs=o_spec,
        out_shape=jax.ShapeDtypeStruct((batch, seq, HEADS * MLA_V), BF16),
        scratch_shapes=_pair_scratch(),
        compiler_params=pltpu.CompilerParams(dimension_semantics=("parallel", "parallel", "arbitrary"),
                                             vmem_limit_bytes=VMEM_LIMIT),
        name="mla_attention",
    )(q, k, vt, km, vmt)


def _fox_attention(beta, q, k, ex, spread, vt, km, exm, vmt, crow):
    batch, seq, _ = q.shape
    q_spec, k_spec, vt_spec, km_spec, vmt_spec, o_spec = _attention_specs(batch, seq, LANE)
    crow_spec = pl.BlockSpec((None, HEADS, Q_SUB * ATT_TILE), lambda b, hp, i, *_: (b, 0, i))
    ex_spec = pl.BlockSpec((None, seq, LANE), lambda b, hp, i, *_: (b, 0, 0))
    exm_spec = pl.BlockSpec((N_META, LANE), lambda b, hp, i, *_: (0, 0))
    spread_spec = pl.BlockSpec((None, LANE, LANE), lambda b, hp, i, *_: (hp, 0, 0))
    grid_spec = pltpu.PrefetchScalarGridSpec(
        num_scalar_prefetch=1,
        grid=(batch, HEAD_PAIRS, seq // (Q_SUB * ATT_TILE)),
        in_specs=[q_spec, k_spec, ex_spec, spread_spec, vt_spec, km_spec, exm_spec, vmt_spec, crow_spec],
        out_specs=o_spec,
        scratch_shapes=[pltpu.VMEM((2, seq, LANE), BF16)] + _pair_scratch(),
    )
    return pl.pallas_call(
        _fox_attn_kernel,
        grid_spec=grid_spec,
        out_shape=jax.ShapeDtypeStruct((batch, seq, HEADS * FOX_DIM), BF16),
        compiler_params=pltpu.CompilerParams(dimension_semantics=("parallel", "parallel", "arbitrary"),
                                             vmem_limit_bytes=VMEM_LIMIT),
        name="fox_attention",
    )(beta, q, k, ex, spread, vt, km, exm, vmt, crow)


def _merge_kernel(om_ref, szm_ref, of_ref, szf_ref, ga_ref, gb_ref, x_ref, wa_ref, wb_ref, wo_ref, g_ref, out_ref):
    ya = jnp.dot(om_ref[...] * szm_ref[...], wa_ref[...], preferred_element_type=F32)
    yb = jnp.dot(of_ref[...] * szf_ref[...], wb_ref[...], preferred_element_type=F32)
    mixed_in = (ga_ref[...].astype(F32) * ya + gb_ref[...].astype(F32) * yb).astype(BF16)
    mixed = jnp.dot(mixed_in, wo_ref[...], preferred_element_type=F32)
    out_ref[...] = x_ref[...] + _rms(mixed, g_ref[...])


def _merge(om, szm, of, szf, ga, gb, x2d, wa, wb, wo, g):
    rows, d = x2d.shape
    row_spec = pl.BlockSpec((ROW_TILE, d), lambda i: (i, 0))
    return pl.pallas_call(
        _merge_kernel,
        grid=(rows // ROW_TILE,),
        in_specs=[row_spec] * 7 + [_const_spec(w.shape) for w in (wa, wb, wo, g)],
        out_specs=row_spec,
        out_shape=jax.ShapeDtypeStruct((rows, d), F32),
        compiler_params=pltpu.CompilerParams(dimension_semantics=("parallel",), vmem_limit_bytes=VMEM_LIMIT),
        name="merge",
    )(om, szm, of, szf, ga, gb, x2d, wa, wb, wo, g)


def _transpose_kernel(w_ref, o_ref):
    o_ref[...] = w_ref[...].T.astype(BF16)


def _transpose_bf16(w):
    k, n = w.shape
    cols = 2 * LANE
    return pl.pallas_call(
        _transpose_kernel,
        grid=(n // cols,),
        in_specs=[pl.BlockSpec((k, cols), lambda i: (0, i))],
        out_specs=pl.BlockSpec((cols, k), lambda i: (i, 0)),
        out_shape=jax.ShapeDtypeStruct((n, k), BF16),
        name="weight_transpose",
    )(w)


def _rotate_half_cols(w):
    half = w.shape[-1] // 2
    return jnp.concatenate([-w[..., half:], w[..., :half]], axis=-1)


def _pad_cols(w, before, total):
    return jnp.pad(w, ((0, 0), (before, total - before - w.shape[1])))


def _in_proj_weights(w_in):
    widths = (MLA_Q_RANK, MLA_KV_RANK, MLA_ROPE, HEADS * MLA_V, HEADS * FOX_DIM, HEADS * FOX_DIM,
              HEADS * FOX_DIM, HEADS, HEADS * FOX_DIM, w_in.shape[0], w_in.shape[0])
    assert sum(widths) == w_in.shape[1]
    bounds = np.cumsum((0,) + widths)
    wt = w_in.T
    w_cq, w_ckv, w_kpe, w_zm, w_fq, w_fk, w_fv, w_fl, w_zf, w_ga, w_gb = (
        wt[bounds[n]:bounds[n + 1]] for n in range(len(widths)))
    w_small = jnp.concatenate([
        w_cq, w_ckv,
        _pad_cols(w_fl.T, 0, KPE_LANE).T, _pad_cols(w_kpe.T, 0, LANE - KPE_LANE).T,
        _pad_cols(_rotate_half_cols(w_kpe.T), KPE_LANE, LANE).T], axis=0)
    assert w_small.shape[0] == SMALL_W
    return [w.astype(BF16) for w in (w_small, w_zm, w_fq, w_fk, w_fv, w_zf, w_ga, w_gb)]


def _head_blocks(w, per_head, first, last, at, width, fn=lambda blk: blk):
    blocks = [_pad_cols(fn(w[:, h * per_head + first:h * per_head + last]), at, width) for h in range(HEADS)]
    return jnp.concatenate(blocks, axis=1)


def _mla_weights(w_uq, w_ukv):
    dq = MLA_NOPE + MLA_ROPE
    dkv = MLA_NOPE + MLA_V
    wq_main = _head_blocks(w_uq, dq, 0, dq, 0, LANE)
    wq_rot = _head_blocks(w_uq, dq, MLA_NOPE, dq, MLA_NOPE, LANE, _rotate_half_cols)
    wk_main = _head_blocks(w_ukv, dkv, 0, MLA_NOPE, 0, LANE)
    wv_t = _transpose_bf16(_head_blocks(w_ukv, dkv, MLA_NOPE, dkv, 0, MLA_V))
    return [w.astype(BF16) for w in (wq_main, wq_rot, wk_main)] + [wv_t]


def _rope_tables(first_pos, n_pos):
    half = MLA_ROPE // 2
    inv_freq = ROPE_THETA ** (-np.arange(half, dtype=np.float64) / half)
    ang = (first_pos + np.arange(n_pos, dtype=np.float64))[:, None] * inv_freq[None, :]
    cos = np.concatenate([np.cos(ang), np.cos(ang)], axis=1)
    sin = np.concatenate([np.sin(ang), np.sin(ang)], axis=1)
    pad = np.zeros((n_pos, LANE - MLA_NOPE - MLA_ROPE))
    ones = np.ones((n_pos, MLA_NOPE))
    zeros = np.zeros((n_pos, MLA_NOPE))
    cos_q = np.concatenate([ones, cos, pad], axis=1) * (MLA_SCALE * LOG2E)
    sin_q = np.concatenate([zeros, sin, pad], axis=1) * (MLA_SCALE * LOG2E)
    cos_k = np.concatenate([zeros, cos, pad], axis=1)
    sin_k = np.concatenate([zeros, sin, pad], axis=1)
    return [jnp.asarray(t, F32) for t in (cos_q, sin_q, cos_k, sin_k)]


def _placement():
    pack = np.zeros((3, LANE, LANE), np.float32)
    spread = np.zeros((HEAD_PAIRS, LANE, LANE), np.float32)
    for h in range(HEADS):
        base = FOX_DIM if h % 2 == 0 else 0
        for part in range(3):
            pack[part, h, HEADS * part + h] = 1.0
            spread[h // 2, HEADS * part + h, base + part] = 1.0
    return jnp.asarray(pack, BF16), jnp.asarray(spread, BF16)


def kernel(x, meta_tokens, pre_norm_g, w_in, fox_forget_b, mla_q_norm_g, mla_kv_norm_g, w_uq, w_ukv,
           w_br_mla, w_br_fox, w_out, post_norm_g):
    batch, seq, d = x.shape
    assert pre_norm_g.shape[0] == 1, "one layer supported"
    assert meta_tokens.shape[0] == N_META and seq % (Q_SUB * ATT_TILE) == 0 and ATT_TILE == ROW_TILE
    n_tiles = seq // ATT_TILE
    x2d = x.reshape(batch * seq, d)

    in_w = _in_proj_weights(w_in[0])
    mla_w = _mla_weights(w_uq[0], w_ukv[0])
    g_pre = pre_norm_g.astype(F32)
    gq = mla_q_norm_g.astype(F32)
    gkv = mla_kv_norm_g.astype(F32)
    bias_row = jnp.pad(fox_forget_b.astype(F32), ((0, 0), (0, LANE - HEADS)))

    vt_shape = (batch, HEAD_PAIRS, n_tiles, LANE, ATT_TILE)
    vt_spec = pl.BlockSpec((None, HEAD_PAIRS, None, LANE, ROW_TILE), lambda i: (i // n_tiles, 0, i % n_tiles, 0, 0))
    vt_meta_shape = (HEAD_PAIRS, LANE, N_META)
    vt_meta_spec = pl.BlockSpec(vt_meta_shape, lambda i: (0, 0, 0))

    fl, szm, fq, fk, fvt, szf, ga, gb, q, k, vt = _in_proj(
        x2d, g_pre, in_w, gq, gkv, mla_w, _rope_tables(N_META, seq), ROW_TILE, n_tiles, vt_shape, vt_spec)
    fl_m, _, _, fk_m, fvt_m, _, _, _, _, k_m, vt_m = _in_proj(
        meta_tokens.astype(F32), g_pre, in_w, gq, gkv, mla_w, _rope_tables(0, N_META), N_META, 1,
        vt_meta_shape, vt_meta_spec)
    pack, spread = _placement()
    ex, crow, beta, ex_m = _decay_prep(fl, fl_m, bias_row, pack, batch, seq)

    o_mla = _mla_attention(q.reshape(batch, seq, -1), k.reshape(batch, seq, -1), vt, k_m, vt_m)
    o_fox = _fox_attention(beta.reshape(-1),fq.reshape(batch, seq, -1), fk.reshape(batch, seq, -1),
                           ex.reshape(batch, seq, -1), spread, fvt, fk_m, ex_m, fvt_m,
                           crow)

    out = _merge(o_mla.reshape(batch * seq, -1), szm, o_fox.reshape(batch * seq, -1), szf, ga, gb, x2d,
                 w_br_mla[0].astype(BF16), w_br_fox[0].astype(BF16), w_out[0].astype(BF16),
                 post_norm_g.astype(F32))
    return out.reshape(batch, seq, d)
```

```python
import functools
import math
from typing import Any, Callable, NamedTuple

import numpy as np
import jax
import jax.numpy as jnp
from jax import lax
from jax.experimental import pallas as pl
from jax.experimental.pallas import tpu as pltpu

F32 = jnp.float32
BF16 = jnp.bfloat16

N_META = 16
RMS_EPS = 1e-6
HEADS = 16
MLA_Q_RANK = 256
MLA_KV_RANK = 128
MLA_NOPE = 64
MLA_ROPE = 32
MLA_V = 64
MLA_SCALE = 1.0 / math.sqrt(MLA_NOPE + MLA_ROPE)
ROPE_THETA = 10000.0
FOX_DIM = 64
FOX_SCALE = 1.0 / math.sqrt(FOX_DIM)
LOG2E = math.log2(math.e)

LANE = 128
HEAD_PAIRS = HEADS // 2
SMALL_W = 5 * LANE
KPE_LANE = 64
ROW_TILE = 512
ATT_TILE = 512
Q_SUB = 2
L_ROWS = 16
NEG_INF = -1e30
VMEM_LIMIT = 56 * 1024 * 1024

NT_DIMS = (((1,), (1,)), ((), ()))


def _rms(x, g):
    return x * lax.rsqrt(jnp.mean(x * x, axis=-1, keepdims=True) + RMS_EPS) * g


def _sigmoid(x):
    return 1.0 / (1.0 + jnp.exp(-x))


def _split3(x):
    hi = x.astype(BF16)
    r1 = x - hi.astype(F32)
    mid = r1.astype(BF16)
    lo = (r1 - mid.astype(F32)).astype(BF16)
    return hi, mid, lo


def _const_spec(shape):
    nd = len(shape)
    return pl.BlockSpec(shape, lambda *_: (0,) * nd, pipeline_mode=pl.Buffered(1))


def _in_proj_kernel(x_ref, g_ref, ws_ref, wzm_ref, wfq_ref, wfk_ref, wfvt_ref, wzf_ref, wga_ref, wgb_ref,
                    gq_ref, gkv_ref, wqm_ref, wqr_ref, wkm_ref, wvt_ref, cosq_ref, sinq_ref, cosk_ref, sink_ref,
                    fl_ref, szm_ref, fq_ref, fk_ref, fvt_ref, szf_ref, ga_ref, gb_ref, q_ref, k_ref, vt_ref):
    u = _rms(x_ref[...], g_ref[...]).astype(BF16)

    def mm(w_ref):
        return lax.dot_general(u, w_ref[...], NT_DIMS, preferred_element_type=F32)

    def store_pairs(out_ref, rows_t):
        for hp in range(HEAD_PAIRS):
            out_ref[hp] = rows_t[hp * LANE:(hp + 1) * LANE, :].astype(BF16)

    z = mm(wzm_ref)
    szm_ref[...] = (z * _sigmoid(z)).astype(BF16)
    fq_ref[...] = (mm(wfq_ref) * (FOX_SCALE * LOG2E)).astype(BF16)
    fk_ref[...] = mm(wfk_ref).astype(BF16)
    store_pairs(fvt_ref, lax.dot_general(wfvt_ref[...], u, NT_DIMS, preferred_element_type=F32))
    z = mm(wzf_ref)
    szf_ref[...] = (z * _sigmoid(z)).astype(BF16)
    ga_ref[...] = _sigmoid(mm(wga_ref)).astype(BF16)
    gb_ref[...] = _sigmoid(mm(wgb_ref)).astype(BF16)

    sm = mm(ws_ref)
    fl_ref[...] = sm[:, 3 * LANE:4 * LANE]
    cn = _rms(sm[:, 0:MLA_Q_RANK], gq_ref[...]).astype(BF16)
    kn = _rms(sm[:, MLA_Q_RANK:MLA_Q_RANK + MLA_KV_RANK], gkv_ref[...]).astype(BF16)
    k_rope = sm[:, 3 * LANE:4 * LANE] * cosk_ref[...] + sm[:, 4 * LANE:5 * LANE] * sink_ref[...]
    qm = jnp.dot(cn, wqm_ref[...], preferred_element_type=F32)
    qr = jnp.dot(cn, wqr_ref[...], preferred_element_type=F32)
    km = jnp.dot(kn, wkm_ref[...], preferred_element_type=F32)
    cosq = cosq_ref[...]
    sinq = sinq_ref[...]
    for h in range(HEADS):
        sl = slice(h * LANE, (h + 1) * LANE)
        q_ref[:, sl] = (qm[:, sl] * cosq + qr[:, sl] * sinq).astype(BF16)
        k_ref[:, sl] = (km[:, sl] + k_rope).astype(BF16)
    store_pairs(vt_ref, lax.dot_general(wvt_ref[...], kn, NT_DIMS, preferred_element_type=F32))


def _in_proj(x2d, g, weights, gq, gkv, mla_weights, tables, tm, tiles_per_seq, vt_shape, vt_spec):
    rows, d = x2d.shape
    width = weights[1].shape[0]
    row_spec = lambda w: pl.BlockSpec((tm, w), lambda i: (i, 0))
    tab_spec = pl.BlockSpec((tm, LANE), lambda i: (i % tiles_per_seq, 0))
    wide = HEADS * LANE
    outs = [((rows, LANE), F32, row_spec(LANE))]
    for name in ("szm", "fq", "fk", "fvt", "szf", "ga", "gb"):
        outs.append((vt_shape, BF16, vt_spec) if name == "fvt" else ((rows, width), BF16, row_spec(width)))
    outs += [((rows, wide), BF16, row_spec(wide)), ((rows, wide), BF16, row_spec(wide)), (vt_shape, BF16, vt_spec)]
    consts = [g] + list(weights) + [gq, gkv] + list(mla_weights)
    return pl.pallas_call(
        _in_proj_kernel,
        grid=(rows // tm,),
        in_specs=[row_spec(d)] + [_const_spec(c.shape) for c in consts] + [tab_spec] * 4,
        out_specs=[spec for _, _, spec in outs],
        out_shape=[jax.ShapeDtypeStruct(shape, dtype) for shape, dtype, _ in outs],
        compiler_params=pltpu.CompilerParams(dimension_semantics=("parallel",), vmem_limit_bytes=VMEM_LIMIT),
        name="in_proj",
    )(x2d, *consts, *tables)


def _log2_sigmoid(z):
    return (jnp.minimum(z, 0.0) - jnp.log1p(jnp.exp(-jnp.abs(z)))) * LOG2E


def _tri(n):
    row = lax.broadcasted_iota(jnp.int32, (n, n), 0)
    col = lax.broadcasted_iota(jnp.int32, (n, n), 1)
    return jnp.where(col <= row, 1.0, 0.0).astype(BF16)


def _cumsum_rows(tri, x):
    return sum(jnp.dot(tri, part, preferred_element_type=F32) for part in _split3(x))


def _place(x, place_ref):
    return sum(jnp.dot(part, place_ref[k], preferred_element_type=F32)
               for k, part in enumerate(_split3(x))).astype(BF16)


def _decay_kernel(sm_ref, bias_ref, place_ref, ex_ref, crow_ref, beta_ref, *, tile):
    n_tiles = sm_ref.shape[0] // tile
    tri = _tri(tile)
    eye = jnp.where(lax.broadcasted_iota(jnp.int32, (LANE, LANE), 0)
                    == lax.broadcasted_iota(jnp.int32, (LANE, LANE), 1), 1.0, 0.0).astype(BF16)
    carry = jnp.zeros((1, LANE), F32)
    for j in range(n_tiles):
        rows = slice(j * tile, (j + 1) * tile)
        within = _cumsum_rows(tri, _log2_sigmoid(sm_ref[rows, :] + bias_ref[...]))
        ex_ref[rows, :] = _place(-within, place_ref)
        cum = within + carry
        cum_t = sum(lax.dot_general(eye, part, NT_DIMS, preferred_element_type=F32) for part in _split3(cum))
        crow_ref[:, rows] = cum_t[0:HEADS, :]
        beta_ref[j:j + 1, :] = carry
        carry = cum[tile - 1:tile, :]


def _decay_meta_kernel(sm_ref, bias_ref, place_ref, ex_ref):
    n = sm_ref.shape[0]
    cum = _cumsum_rows(_tri(n), _log2_sigmoid(sm_ref[...] + bias_ref[...]))
    ex_ref[...] = _place(cum[n - 1:n, :] - cum, place_ref)


def _decay_prep(fl, fl_meta, bias_row, place, batch, seq):
    width = place.shape[2]
    ex, crow, beta = pl.pallas_call(
        functools.partial(_decay_kernel, tile=ATT_TILE),
        grid=(batch,),
        in_specs=[pl.BlockSpec((seq, LANE), lambda b: (b, 0)), _const_spec(bias_row.shape), _const_spec(place.shape)],
        out_specs=[pl.BlockSpec((seq, width), lambda b: (b, 0)),
                   pl.BlockSpec((None, HEADS, seq), lambda b: (b, 0, 0)),
                   pl.BlockSpec((None, seq // ATT_TILE, LANE), lambda b: (b, 0, 0))],
        out_shape=[jax.ShapeDtypeStruct((batch * seq, width), BF16),
                   jax.ShapeDtypeStruct((batch, HEADS, seq), F32),
                   jax.ShapeDtypeStruct((batch, seq // ATT_TILE, LANE), F32)],
        compiler_params=pltpu.CompilerParams(dimension_semantics=("parallel",), vmem_limit_bytes=VMEM_LIMIT),
        name="decay_prep",
    )(fl, bias_row, place)
    ex_meta = pl.pallas_call(
        _decay_meta_kernel,
        grid=(1,),
        in_specs=[pl.BlockSpec((N_META, LANE), lambda i: (0, 0)), _const_spec(bias_row.shape),
                  _const_spec(place.shape)],
        out_specs=pl.BlockSpec((N_META, width), lambda i: (0, 0)),
        out_shape=jax.ShapeDtypeStruct((N_META, width), BF16),
        name="decay_prep_meta",
    )(fl_meta, bias_row, place)
    return ex, crow, beta, ex_meta


class _Chain(NamedTuple):
    q: jax.Array
    k_meta: jax.Array
    vt_meta: jax.Array
    k_tile: Callable
    vt_tile: Callable
    off_meta: Any
    off_tile: Any
    sub: int


def _scores(q_h, k_t):
    return lax.dot_general(k_t, q_h, NT_DIMS, preferred_element_type=F32)


def _col_max(s):
    return jnp.max(s, axis=0, keepdims=True)


def _pv(vt_t, p):
    v_aug = jnp.concatenate([vt_t, jnp.ones((L_ROWS, vt_t.shape[1]), BF16)], axis=0)
    return jnp.dot(v_aug, p, preferred_element_type=F32)


def _shifted(m_tile, off):
    return m_tile if off is None else m_tile + off


def _probs(s, m_new, off):
    return jnp.exp2(s - (m_new if off is None else m_new - off)).astype(BF16)


def _causal(shape, first_query):
    return (lax.broadcasted_iota(jnp.int32, shape, 0)
            <= lax.broadcasted_iota(jnp.int32, shape, 1) + first_query)


def _attend(step, chains, s_ref, p_ref, acc_ref, o_ref):
    n = len(chains)
    n_common = Q_SUB * step

    s_meta = [_scores(ch.q, ch.k_meta) for ch in chains]
    tile_max = []
    for c, ch in enumerate(chains):
        s = _scores(ch.q, ch.k_tile(0))
        s_ref[c] = s
        tile_max.append(_col_max(s))
        p_ref[c] = jnp.zeros(p_ref.shape[1:], BF16)
        acc_ref[c] = jnp.zeros(acc_ref.shape[1:], F32)
    ms = tuple(jnp.full((1, ATT_TILE), NEG_INF, F32) for _ in chains)
    alphas = tuple(jnp.ones((1, ATT_TILE), F32) for _ in chains)

    def pending_pv(c, ch, j, alpha):
        acc_ref[c] = alpha * acc_ref[c] + _pv(ch.vt_tile(jnp.maximum(j - 1, 0)), p_ref[c])

    def body(j, carry):
        ms, alphas, tile_max = carry
        cur = [s_ref[c] for c in range(n)]
        for c, ch in enumerate(chains):
            pending_pv(c, ch, j, alphas[c])
        new_ms, new_alphas, probs = [], [], []
        for c, ch in enumerate(chains):
            off = None if ch.off_tile is None else ch.off_tile(j)
            m_new = jnp.maximum(ms[c], _shifted(tile_max[c], off))
            new_alphas.append(jnp.exp2(ms[c] - m_new))
            probs.append(_probs(cur[c], m_new, off))
            new_ms.append(m_new)
        nxt = [_scores(ch.q, ch.k_tile(j + 1)) for ch in chains]
        for c in range(n):
            p_ref[c] = probs[c]
            s_ref[c] = nxt[c]
        return tuple(new_ms), tuple(new_alphas), tuple(_col_max(s) for s in nxt)

    def unrolled(t, carry):
        for u in range(Q_SUB):
            carry = body(Q_SUB * t + u, carry)
        return carry

    ms, alphas, tile_max = lax.fori_loop(0, step, unrolled, (ms, alphas, tuple(tile_max)))

    outs = {}
    for c, ch in enumerate(chains):
        pending_pv(c, ch, n_common, alphas[c])
        m, acc, s, s_max, t = ms[c], acc_ref[c], s_ref[c], tile_max[c], n_common
        for _ in range(ch.sub):
            off = None if ch.off_tile is None else ch.off_tile(t)
            m_new = jnp.maximum(m, _shifted(s_max, off))
            acc = jnp.exp2(m - m_new) * acc + _pv(ch.vt_tile(t), _probs(s, m_new, off))
            m, t = m_new, t + 1
            s = _scores(ch.q, ch.k_tile(t))
            s_max = _col_max(s)
        s = jnp.where(_causal(s.shape, 0), s, NEG_INF)
        off = None if ch.off_tile is None else ch.off_tile(t)
        m_new = jnp.maximum(jnp.maximum(m, _shifted(_col_max(s), off)), _shifted(_col_max(s_meta[c]), ch.off_meta))
        acc = (jnp.exp2(m - m_new) * acc + _pv(ch.vt_tile(t), _probs(s, m_new, off))
               + _pv(ch.vt_meta, _probs(s_meta[c], m_new, ch.off_meta)))
        outs.setdefault(ch.sub, []).append(acc[0:FOX_DIM, :] / acc[FOX_DIM:FOX_DIM + 1, :])
    for u, out_t in outs.items():
        o_ref[u * ATT_TILE:(u + 1) * ATT_TILE, :] = jnp.concatenate(out_t, axis=0).T.astype(BF16)


def _key_rows(j):
    return pl.ds(pl.multiple_of(j * ATT_TILE, ATT_TILE), ATT_TILE)


def _mla_attn_kernel(q_ref, k_ref, vt_ref, km_ref, vmt_ref, o_ref, *scratch):
    chains = []
    for u in range(Q_SUB):
        for hh in range(2):
            lanes = slice(hh * LANE, (hh + 1) * LANE)
            rows64 = slice(hh * MLA_V, (hh + 1) * MLA_V)
            chains.append(_Chain(
                q=q_ref[u * ATT_TILE:(u + 1) * ATT_TILE, lanes], k_meta=km_ref[:, lanes], vt_meta=vmt_ref[rows64, :],
                k_tile=lambda j, lanes=lanes: k_ref[_key_rows(j), lanes],
                vt_tile=lambda j, rows64=rows64: vt_ref[j, rows64, :],
                off_meta=None, off_tile=None, sub=u))
    _attend(pl.program_id(2), chains, *scratch, o_ref)


def _fox_attn_kernel(beta_ref, q_ref, k_ref, ex_ref, spread_ref, vt_ref, km_ref, exm_ref, vmt_ref, crow_ref, o_ref,
                     kaug_ref, *scratch):
    b = pl.program_id(0)
    hp = pl.program_id(1)
    step = pl.program_id(2)
    n_tiles = k_ref.shape[0] // ATT_TILE
    lane = lax.broadcasted_iota(jnp.int32, (1, LANE), 1)
    in_a = lane < FOX_DIM

    def spread(packed):
        return jnp.dot(packed, spread_ref[...], preferred_element_type=F32).astype(BF16)

    @pl.when(step == 0)
    def _():
        for c in range(n_tiles):
            rows = slice(c * ATT_TILE, (c + 1) * ATT_TILE)
            kk = k_ref[rows, :]
            ee = spread(ex_ref[rows, :])
            kaug_ref[0, rows, :] = jnp.where(in_a, kk, ee)
            kaug_ref[1, rows, :] = jnp.where(in_a, ee, kk)

    km = km_ref[...]
    exm = spread(exm_ref[...])

    chains = []
    for u in range(Q_SUB):
        cols = slice(u * ATT_TILE, (u + 1) * ATT_TILE)
        q = q_ref[cols, :]
        for hh in range(2):
            own = in_a if hh == 0 else jnp.logical_not(in_a)
            base = FOX_DIM if hh == 0 else 0
            ones = jnp.where((lane >= base) & (lane < base + 3), 1.0, 0.0).astype(BF16)
            rows64 = slice(hh * FOX_DIM, (hh + 1) * FOX_DIM)
            cum_q = crow_ref[pl.ds(2 * hp + hh, 1), cols]
            beta_base = b * n_tiles * LANE + 2 * hp + hh
            chains.append(_Chain(
                q=jnp.where(own, q, ones), k_meta=jnp.where(own, km, exm), vt_meta=vmt_ref[rows64, :],
                k_tile=lambda j, hh=hh: kaug_ref[hh, _key_rows(j), :],
                vt_tile=lambda j, rows64=rows64: vt_ref[j, rows64, :],
                off_meta=cum_q,
                off_tile=lambda j, cum_q=cum_q, beta_base=beta_base: cum_q - beta_ref[beta_base + j * LANE],
                sub=u))
    _attend(step, chains, *scratch, o_ref)


def _attention_specs(batch, seq, width):
    n_tiles = seq // ATT_TILE
    q_spec = pl.BlockSpec((None, Q_SUB * ATT_TILE, width), lambda b, hp, i, *_: (b, i, hp))
    k_spec = pl.BlockSpec((None, seq, width), lambda b, hp, i, *_: (b, 0, hp))
    vt_spec = pl.BlockSpec((None, None, n_tiles, LANE, ATT_TILE), lambda b, hp, i, *_: (b, hp, 0, 0, 0))
    km_spec = pl.BlockSpec((N_META, width), lambda b, hp, i, *_: (0, hp))
    vmt_spec = pl.BlockSpec((None, LANE, N_META), lambda b, hp, i, *_: (hp, 0, 0))
    o_spec = pl.BlockSpec((None, Q_SUB * ATT_TILE, LANE), lambda b, hp, i, *_: (b, i, hp))
    return q_spec, k_spec, vt_spec, km_spec, vmt_spec, o_spec


def _pair_scratch():
    n = 2 * Q_SUB
    return [pltpu.VMEM((n, ATT_TILE, ATT_TILE), F32), pltpu.VMEM((n, ATT_TILE, ATT_TILE), BF16),
            pltpu.VMEM((n, MLA_V + L_ROWS, ATT_TILE), F32)]


def _mla_attention(q, k, vt, km, vmt):
    batch, seq, _ = q.shape
    q_spec, k_spec, vt_spec, km_spec, vmt_spec, o_spec = _attention_specs(batch, seq, 2 * LANE)
    return pl.pallas_call(
        _mla_attn_kernel,
        grid=(batch, HEAD_PAIRS, seq // (Q_SUB * ATT_TILE)),
        in_specs=[q_spec, k_spec, vt_spec, km_spec, vmt_spec],
        out_specs=o_spec,
        out_shape=jax.ShapeDtypeStruct((batch, seq, HEADS * MLA_V), BF16),
        scratch_shapes=_pair_scratch(),
        compiler_params=pltpu.CompilerParams(dimension_semantics=("parallel", "parallel", "arbitrary"),
                                             vmem_limit_bytes=VMEM_LIMIT),
        name="mla_attention",
    )(q, k, vt, km, vmt)


def _fox_attention(beta, q, k, ex, spread, vt, km, exm, vmt, crow):
    batch, seq, _ = q.shape
    q_spec, k_spec, vt_spec, km_spec, vmt_spec, o_spec = _attention_specs(batch, seq, LANE)
    crow_spec = pl.BlockSpec((None, HEADS, Q_SUB * ATT_TILE), lambda b, hp, i, *_: (b, 0, i))
    ex_spec = pl.BlockSpec((None, seq, LANE), lambda b, hp, i, *_: (b, 0, 0))
    exm_spec = pl.BlockSpec((N_META, LANE), lambda b, hp, i, *_: (0, 0))
    spread_spec = pl.BlockSpec((None, LANE, LANE), lambda b, hp, i, *_: (hp, 0, 0))
    grid_spec = pltpu.PrefetchScalarGridSpec(
        num_scalar_prefetch=1,
        grid=(batch, HEAD_PAIRS, seq // (Q_SUB * ATT_TILE)),
        in_specs=[q_spec, k_spec, ex_spec, spread_spec, vt_spec, km_spec, exm_spec, vmt_spec, crow_spec],
        out_specs=o_spec,
        scratch_shapes=[pltpu.VMEM((2, seq, LANE), BF16)] + _pair_scratch(),
    )
    return pl.pallas_call(
        _fox_attn_kernel,
        grid_spec=grid_spec,
        out_shape=jax.ShapeDtypeStruct((batch, seq, HEADS * FOX_DIM), BF16),
        compiler_params=pltpu.CompilerParams(dimension_semantics=("parallel", "parallel", "arbitrary"),
                                             vmem_limit_bytes=VMEM_LIMIT),
        name="fox_attention",
    )(beta, q, k, ex, spread, vt, km, exm, vmt, crow)


def _merge_kernel(om_ref, szm_ref, of_ref, szf_ref, ga_ref, gb_ref, x_ref, wa_ref, wb_ref, wo_ref, g_ref, out_ref):
    ya = jnp.dot(om_ref[...] * szm_ref[...], wa_ref[...], preferred_element_type=F32)
    yb = jnp.dot(of_ref[...] * szf_ref[...], wb_ref[...], preferred_element_type=F32)
    mixed_in = (ga_ref[...].astype(F32) * ya + gb_ref[...].astype(F32) * yb).astype(BF16)
    mixed = jnp.dot(mixed_in, wo_ref[...], preferred_element_type=F32)
    out_ref[...] = x_ref[...] + _rms(mixed, g_ref[...])


def _merge(om, szm, of, szf, ga, gb, x2d, wa, wb, wo, g):
    rows, d = x2d.shape
    row_spec = pl.BlockSpec((ROW_TILE, d), lambda i: (i, 0))
    return pl.pallas_call(
        _merge_kernel,
        grid=(rows // ROW_TILE,),
        in_specs=[row_spec] * 7 + [_const_spec(w.shape) for w in (wa, wb, wo, g)],
        out_specs=row_spec,
        out_shape=jax.ShapeDtypeStruct((rows, d), F32),
        compiler_params=pltpu.CompilerParams(dimension_semantics=("parallel",), vmem_limit_bytes=VMEM_LIMIT),
        name="merge",
    )(om, szm, of, szf, ga, gb, x2d, wa, wb, wo, g)


def _transpose_kernel(w_ref, o_ref):
    o_ref[...] = w_ref[...].T.astype(BF16)


def _transpose_bf16(w):
    k, n = w.shape
    cols = 2 * LANE
    return pl.pallas_call(
        _transpose_kernel,
        grid=(n // cols,),
        in_specs=[pl.BlockSpec((k, cols), lambda i: (0, i))],
        out_specs=pl.BlockSpec((cols, k), lambda i: (i, 0)),
        out_shape=jax.ShapeDtypeStruct((n, k), BF16),
        name="weight_transpose",
    )(w)


def _rotate_half_cols(w):
    half = w.shape[-1] // 2
    return jnp.concatenate([-w[..., half:], w[..., :half]], axis=-1)


def _pad_cols(w, before, total):
    return jnp.pad(w, ((0, 0), (before, total - before - w.shape[1])))


def _in_proj_weights(w_in):
    widths = (MLA_Q_RANK, MLA_KV_RANK, MLA_ROPE, HEADS * MLA_V, HEADS * FOX_DIM, HEADS * FOX_DIM,
              HEADS * FOX_DIM, HEADS, HEADS * FOX_DIM, w_in.shape[0], w_in.shape[0])
    assert sum(widths) == w_in.shape[1]
    bounds = np.cumsum((0,) + widths)
    wt = w_in.T
    w_cq, w_ckv, w_kpe, w_zm, w_fq, w_fk, w_fv, w_fl, w_zf, w_ga, w_gb = (
        wt[bounds[n]:bounds[n + 1]] for n in range(len(widths)))
    w_small = jnp.concatenate([
        w_cq, w_ckv,
        _pad_cols(w_fl.T, 0, KPE_LANE).T, _pad_cols(w_kpe.T, 0, LANE - KPE_LANE).T,
        _pad_cols(_rotate_half_cols(w_kpe.T), KPE_LANE, LANE).T], axis=0)
    assert w_small.shape[0] == SMALL_W
    return [w.astype(BF16) for w in (w_small, w_zm, w_fq, w_fk, w_fv, w_zf, w_ga, w_gb)]


def _head_blocks(w, per_head, first, last, at, width, fn=lambda blk: blk):
    blocks = [_pad_cols(fn(w[:, h * per_head + first:h * per_head + last]), at, width) for h in range(HEADS)]
    return jnp.concatenate(blocks, axis=1)


def _mla_weights(w_uq, w_ukv):
    dq = MLA_NOPE + MLA_ROPE
    dkv = MLA_NOPE + MLA_V
    wq_main = _head_blocks(w_uq, dq, 0, dq, 0, LANE)
    wq_rot = _head_blocks(w_uq, dq, MLA_NOPE, dq, MLA_NOPE, LANE, _rotate_half_cols)
    wk_main = _head_blocks(w_ukv, dkv, 0, MLA_NOPE, 0, LANE)
    wv_t = _transpose_bf16(_head_blocks(w_ukv, dkv, MLA_NOPE, dkv, 0, MLA_V))
    return [w.astype(BF16) for w in (wq_main, wq_rot, wk_main)] + [wv_t]


def _rope_tables(first_pos, n_pos):
    half = MLA_ROPE // 2
    inv_freq = ROPE_THETA ** (-np.arange(half, dtype=np.float64) / half)
    ang = (first_pos + np.arange(n_pos, dtype=np.float64))[:, None] * inv_freq[None, :]
    cos = np.concatenate([np.cos(ang), np.cos(ang)], axis=1)
    sin = np.concatenate([np.sin(ang), np.sin(ang)], axis=1)
    pad = np.zeros((n_pos, LANE - MLA_NOPE - MLA_ROPE))
    ones = np.ones((n_pos, MLA_NOPE))
    zeros = np.zeros((n_pos, MLA_NOPE))
    cos_q = np.concatenate([ones, cos, pad], axis=1) * (MLA_SCALE * LOG2E)
    sin_q = np.concatenate([zeros, sin, pad], axis=1) * (MLA_SCALE * LOG2E)
    cos_k = np.concatenate([zeros, cos, pad], axis=1)
    sin_k = np.concatenate([zeros, sin, pad], axis=1)
    return [jnp.asarray(t, F32) for t in (cos_q, sin_q, cos_k, sin_k)]


def _placement():
    pack = np.zeros((3, LANE, LANE), np.float32)
    spread = np.zeros((HEAD_PAIRS, LANE, LANE), np.float32)
    for h in range(HEADS):
        base = FOX_DIM if h % 2 == 0 else 0
        for part in range(3):
            pack[part, h, HEADS * part + h] = 1.0
            spread[h // 2, HEADS * part + h, base + part] = 1.0
    return jnp.asarray(pack, BF16), jnp.asarray(spread, BF16)


def kernel(x, meta_tokens, pre_norm_g, w_in, fox_forget_b, mla_q_norm_g, mla_kv_norm_g, w_uq, w_ukv,
           w_br_mla, w_br_fox, w_out, post_norm_g):
    batch, seq, d = x.shape
    assert pre_norm_g.shape[0] == 1, "one layer supported"
    assert meta_tokens.shape[0] == N_META and seq % (Q_SUB * ATT_TILE) == 0 and ATT_TILE == ROW_TILE
    n_tiles = seq // ATT_TILE
    x2d = x.reshape(batch * seq, d)

    in_w = _in_proj_weights(w_in[0])
    mla_w = _mla_weights(w_uq[0], w_ukv[0])
    g_pre = pre_norm_g.astype(F32)
    gq = mla_q_norm_g.astype(F32)
    gkv = mla_kv_norm_g.astype(F32)
    bias_row = jnp.pad(fox_forget_b.astype(F32), ((0, 0), (0, LANE - HEADS)))

    vt_shape = (batch, HEAD_PAIRS, n_tiles, LANE, ATT_TILE)
    vt_spec = pl.BlockSpec((None, HEAD_PAIRS, None, LANE, ROW_TILE), lambda i: (i // n_tiles, 0, i % n_tiles, 0, 0))
    vt_meta_shape = (HEAD_PAIRS, LANE, N_META)
    vt_meta_spec = pl.BlockSpec(vt_meta_shape, lambda i: (0, 0, 0))

    fl, szm, fq, fk, fvt, szf, ga, gb, q, k, vt = _in_proj(
        x2d, g_pre, in_w, gq, gkv, mla_w, _rope_tables(N_META, seq), ROW_TILE, n_tiles, vt_shape, vt_spec)
    fl_m, _, _, fk_m, fvt_m, _, _, _, _, k_m, vt_m = _in_proj(
        meta_tokens.astype(F32), g_pre, in_w, gq, gkv, mla_w, _rope_tables(0, N_META), N_META, 1,
        vt_meta_shape, vt_meta_spec)
    pack, spread = _placement()
    ex, crow, beta, ex_m = _decay_prep(fl, fl_m, bias_row, pack, batch, seq)

    o_mla = _mla_attention(q.reshape(batch, seq, -1), k.reshape(batch, seq, -1), vt, k_m, vt_m)
    o_fox = _fox_attention(beta.reshape(-1),fq.reshape(batch, seq, -1), fk.reshape(batch, seq, -1),
                           ex.reshape(batch, seq, -1), spread, fvt, fk_m, ex_m, fvt_m,
                           crow)

    out = _merge(o_mla.reshape(batch * seq, -1), szm, o_fox.reshape(batch * seq, -1), szf, ga, gb, x2d,
                 w_br_mla[0].astype(BF16), w_br_fox[0].astype(BF16), w_out[0].astype(BF16),
                 post_norm_g.astype(F32))
    return out.reshape(batch, seq, d)
```

```python
import functools
import math
from typing import Any, Callable, NamedTuple

import numpy as np
import jax
import jax.numpy as jnp
from jax import lax
from jax.experimental import pallas as pl
from jax.experimental.pallas import tpu as pltpu

F32 = jnp.float32
BF16 = jnp.bfloat16

N_META = 16
RMS_EPS = 1e-6
HEADS = 16
MLA_Q_RANK = 256
MLA_KV_RANK = 128
MLA_NOPE = 64
MLA_ROPE = 32
MLA_V = 64
MLA_SCALE = 1.0 / math.sqrt(MLA_NOPE + MLA_ROPE)
ROPE_THETA = 10000.0
FOX_DIM = 64
FOX_SCALE = 1.0 / math.sqrt(FOX_DIM)
LOG2E = math.log2(math.e)

LANE = 128
HEAD_PAIRS = HEADS // 2
SMALL_W = 5 * LANE
KPE_LANE = 64
ROW_TILE = 512
ATT_TILE = 512
Q_SUB = 2
L_ROWS = 16
NEG_INF = -1e30
VMEM_LIMIT = 56 * 1024 * 1024

NT_DIMS = (((1,), (1,)), ((), ()))


def _rms(x, g):
    return x * lax.rsqrt(jnp.mean(x * x, axis=-1, keepdims=True) + RMS_EPS) * g


def _sigmoid(x):
    return 1.0 / (1.0 + jnp.exp(-x))


def _split3(x):
    hi = x.astype(BF16)
    r1 = x - hi.astype(F32)
    mid = r1.astype(BF16)
    lo = (r1 - mid.astype(F32)).astype(BF16)
    return hi, mid, lo


def _const_spec(shape):
    nd = len(shape)
    return pl.BlockSpec(shape, lambda *_: (0,) * nd, pipeline_mode=pl.Buffered(1))


def _in_proj_kernel(x_ref, g_ref, ws_ref, wzm_ref, wfq_ref, wfk_ref, wfvt_ref, wzf_ref, wga_ref, wgb_ref,
                    gq_ref, gkv_ref, wqm_ref, wqr_ref, wkm_ref, wvt_ref, cosq_ref, sinq_ref, cosk_ref, sink_ref,
                    fl_ref, szm_ref, fq_ref, fk_ref, fvt_ref, szf_ref, ga_ref, gb_ref, q_ref, k_ref, vt_ref):
    u = _rms(x_ref[...], g_ref[...]).astype(BF16)

    def mm(w_ref):
        return lax.dot_general(u, w_ref[...], NT_DIMS, preferred_element_type=F32)

    def store_pairs(out_ref, rows_t):
        for hp in range(HEAD_PAIRS):
            out_ref[hp] = rows_t[hp * LANE:(hp + 1) * LANE, :].astype(BF16)

    z = mm(wzm_ref)
    szm_ref[...] = (z * _sigmoid(z)).astype(BF16)
    fq_ref[...] = (mm(wfq_ref) * (FOX_SCALE * LOG2E)).astype(BF16)
    fk_ref[...] = mm(wfk_ref).astype(BF16)
    store_pairs(fvt_ref, lax.dot_general(wfvt_ref[...], u, NT_DIMS, preferred_element_type=F32))
    z = mm(wzf_ref)
    szf_ref[...] = (z * _sigmoid(z)).astype(BF16)
    ga_ref[...] = _sigmoid(mm(wga_ref)).astype(BF16)
    gb_ref[...] = _sigmoid(mm(wgb_ref)).astype(BF16)

    sm = mm(ws_ref)
    fl_ref[...] = sm[:, 3 * LANE:4 * LANE]
    cn = _rms(sm[:, 0:MLA_Q_RANK], gq_ref[...]).astype(BF16)
    kn = _rms(sm[:, MLA_Q_RANK:MLA_Q_RANK + MLA_KV_RANK], gkv_ref[...]).astype(BF16)
    k_rope = sm[:, 3 * LANE:4 * LANE] * cosk_ref[...] + sm[:, 4 * LANE:5 * LANE] * sink_ref[...]
    qm = jnp.dot(cn, wqm_ref[...], preferred_element_type=F32)
    qr = jnp.dot(cn, wqr_ref[...], preferred_element_type=F32)
    km = jnp.dot(kn, wkm_ref[...], preferred_element_type=F32)
    cosq = cosq_ref[...]
    sinq = sinq_ref[...]
    for h in range(HEADS):
        sl = slice(h * LANE, (h + 1) * LANE)
        q_ref[:, sl] = (qm[:, sl] * cosq + qr[:, sl] * sinq).astype(BF16)
        k_ref[:, sl] = (km[:, sl] + k_rope).astype(BF16)
    store_pairs(vt_ref, lax.dot_general(wvt_ref[...], kn, NT_DIMS, preferred_element_type=F32))


def _in_proj(x2d, g, weights, gq, gkv, mla_weights, tables, tm, tiles_per_seq, vt_shape, vt_spec):
    rows, d = x2d.shape
    width = weights[1].shape[0]
    row_spec = lambda w: pl.BlockSpec((tm, w), lambda i: (i, 0))
    tab_spec = pl.BlockSpec((tm, LANE), lambda i: (i % tiles_per_seq, 0))
    wide = HEADS * LANE
    outs = [((rows, LANE), F32, row_spec(LANE))]
    for name in ("szm", "fq", "fk", "fvt", "szf", "ga", "gb"):
        outs.append((vt_shape, BF16, vt_spec) if name == "fvt" else ((rows, width), BF16, row_spec(width)))
    outs += [((rows, wide), BF16, row_spec(wide)), ((rows, wide), BF16, row_spec(wide)), (vt_shape, BF16, vt_spec)]
    consts = [g] + list(weights) + [gq, gkv] + list(mla_weights)
    return pl.pallas_call(
        _in_proj_kernel,
        grid=(rows // tm,),
        in_specs=[row_spec(d)] + [_const_spec(c.shape) for c in consts] + [tab_spec] * 4,
        out_specs=[spec for _, _, spec in outs],
        out_shape=[jax.ShapeDtypeStruct(shape, dtype) for shape, dtype, _ in outs],
        compiler_params=pltpu.CompilerParams(dimension_semantics=("parallel",), vmem_limit_bytes=VMEM_LIMIT),
        name="in_proj",
    )(x2d, *consts, *tables)


def _log2_sigmoid(z):
    return (jnp.minimum(z, 0.0) - jnp.log1p(jnp.exp(-jnp.abs(z)))) * LOG2E


def _tri(n):
    row = lax.broadcasted_iota(jnp.int32, (n, n), 0)
    col = lax.broadcasted_iota(jnp.int32, (n, n), 1)
    return jnp.where(col <= row, 1.0, 0.0).astype(BF16)


def _cumsum_rows(tri, x):
    return sum(jnp.dot(tri, part, preferred_element_type=F32) for part in _split3(x))


def _place(x, place_ref):
    return sum(jnp.dot(part, place_ref[k], preferred_element_type=F32)
               for k, part in enumerate(_split3(x))).astype(BF16)


def _decay_kernel(sm_ref, bias_ref, place_ref, ex_ref, crow_ref, beta_ref, *, tile):
    n_tiles = sm_ref.shape[0] // tile
    tri = _tri(tile)
    eye = jnp.where(lax.broadcasted_iota(jnp.int32, (LANE, LANE), 0)
                    == lax.broadcasted_iota(jnp.int32, (LANE, LANE), 1), 1.0, 0.0).astype(BF16)
    carry = jnp.zeros((1, LANE), F32)
    for j in range(n_tiles):
        rows = slice(j * tile, (j + 1) * tile)
        within = _cumsum_rows(tri, _log2_sigmoid(sm_ref[rows, :] + bias_ref[...]))
        ex_ref[rows, :] = _place(-within, place_ref)
        cum = within + carry
        cum_t = sum(lax.dot_general(eye, part, NT_DIMS, preferred_element_type=F32) for part in _split3(cum))
        crow_ref[:, j, :] = cum_t[0:HEADS, :]
        beta_ref[j:j + 1, :] = carry
        carry = cum[tile - 1:tile, :]


def _decay_meta_kernel(sm_ref, bias_ref, place_ref, ex_ref):
    n = sm_ref.shape[0]
    cum = _cumsum_rows(_tri(n), _log2_sigmoid(sm_ref[...] + bias_ref[...]))
    ex_ref[...] = _place(cum[n - 1:n, :] - cum, place_ref)


def _decay_prep(fl, fl_meta, bias_row, place, batch, seq):
    width = place.shape[2]
    ex, crow, beta = pl.pallas_call(
        functools.partial(_decay_kernel, tile=ATT_TILE),
        grid=(batch,),
        in_specs=[pl.BlockSpec((seq, LANE), lambda b: (b, 0)), _const_spec(bias_row.shape), _const_spec(place.shape)],
        out_specs=[pl.BlockSpec((seq, width), lambda b: (b, 0)),
                   pl.BlockSpec((None, HEADS, seq // ATT_TILE, ATT_TILE), lambda b: (b, 0, 0, 0)),
                   pl.BlockSpec((None, seq // ATT_TILE, LANE), lambda b: (b, 0, 0))],
        out_shape=[jax.ShapeDtypeStruct((batch * seq, width), BF16),
                   jax.ShapeDtypeStruct((batch, HEADS, seq // ATT_TILE, ATT_TILE), F32),
                   jax.ShapeDtypeStruct((batch, seq // ATT_TILE, LANE), F32)],
        compiler_params=pltpu.CompilerParams(dimension_semantics=("parallel",), vmem_limit_bytes=VMEM_LIMIT),
        name="decay_prep",
    )(fl, bias_row, place)
    ex_meta = pl.pallas_call(
        _decay_meta_kernel,
        grid=(1,),
        in_specs=[pl.BlockSpec((N_META, LANE), lambda i: (0, 0)), _const_spec(bias_row.shape),
                  _const_spec(place.shape)],
        out_specs=pl.BlockSpec((N_META, width), lambda i: (0, 0)),
        out_shape=jax.ShapeDtypeStruct((N_META, width), BF16),
        name="decay_prep_meta",
    )(fl_meta, bias_row, place)
    return ex, crow, beta, ex_meta


class _Chain(NamedTuple):
    q: jax.Array
    k_meta: jax.Array
    vt_meta: jax.Array
    k_tile: Callable
    vt_tile: Callable
    off_meta: Any
    off_tile: Any
    sub: int


def _scores(q_h, k_t):
    return lax.dot_general(k_t, q_h, NT_DIMS, preferred_element_type=F32)


def _col_max(s):
    return jnp.max(s, axis=0, keepdims=True)


def _pv(vt_t, p):
    v_aug = jnp.concatenate([vt_t, jnp.ones((L_ROWS, vt_t.shape[1]), BF16)], axis=0)
    return jnp.dot(v_aug, p, preferred_element_type=F32)


def _shifted(m_tile, off):
    return m_tile if off is None else m_tile + off


def _probs(s, m_new, off):
    return jnp.exp2(s - (m_new if off is None else m_new - off)).astype(BF16)


def _causal(shape, first_query):
    return (lax.broadcasted_iota(jnp.int32, shape, 0)
            <= lax.broadcasted_iota(jnp.int32, shape, 1) + first_query)


def _attend(step, chains, s_ref, p_ref, acc_ref, o_ref):
    n = len(chains)
    n_common = Q_SUB * step

    s_meta = [_scores(ch.q, ch.k_meta) for ch in chains]
    tile_max = []
    for c, ch in enumerate(chains):
        s = _scores(ch.q, ch.k_tile(0))
        s_ref[c] = s
        tile_max.append(_col_max(s))
        p_ref[c] = jnp.zeros(p_ref.shape[1:], BF16)
        acc_ref[c] = jnp.zeros(acc_ref.shape[1:], F32)
    ms = tuple(jnp.full((1, ATT_TILE), NEG_INF, F32) for _ in chains)
    alphas = tuple(jnp.ones((1, ATT_TILE), F32) for _ in chains)

    def pending_pv(c, ch, j, alpha):
        acc_ref[c] = alpha * acc_ref[c] + _pv(ch.vt_tile(jnp.maximum(j - 1, 0)), p_ref[c])

    def body(j, carry):
        ms, alphas, tile_max = carry
        cur = [s_ref[c] for c in range(n)]
        for c, ch in enumerate(chains):
            pending_pv(c, ch, j, alphas[c])
        new_ms, new_alphas, probs = [], [], []
        for c, ch in enumerate(chains):
            off = None if ch.off_tile is None else ch.off_tile(j)
            m_new = jnp.maximum(ms[c], _shifted(tile_max[c], off))
            new_alphas.append(jnp.exp2(ms[c] - m_new))
            probs.append(_probs(cur[c], m_new, off))
            new_ms.append(m_new)
        nxt = [_scores(ch.q, ch.k_tile(j + 1)) for ch in chains]
        for c in range(n):
            p_ref[c] = probs[c]
            s_ref[c] = nxt[c]
        return tuple(new_ms), tuple(new_alphas), tuple(_col_max(s) for s in nxt)

    def unrolled(t, carry):
        for u in range(Q_SUB):
            carry = body(Q_SUB * t + u, carry)
        return carry

    ms, alphas, tile_max = lax.fori_loop(0, step, unrolled, (ms, alphas, tuple(tile_max)))

    outs = {}
    for c, ch in enumerate(chains):
        pending_pv(c, ch, n_common, alphas[c])
        m, acc, s, s_max, t = ms[c], acc_ref[c], s_ref[c], tile_max[c], n_common
        for _ in range(ch.sub):
            off = None if ch.off_tile is None else ch.off_tile(t)
            m_new = jnp.maximum(m, _shifted(s_max, off))
            acc = jnp.exp2(m - m_new) * acc + _pv(ch.vt_tile(t), _probs(s, m_new, off))
            m, t = m_new, t + 1
            s = _scores(ch.q, ch.k_tile(t))
            s_max = _col_max(s)
        s = jnp.where(_causal(s.shape, 0), s, NEG_INF)
        off = None if ch.off_tile is None else ch.off_tile(t)
        m_new = jnp.maximum(jnp.maximum(m, _shifted(_col_max(s), off)), _shifted(_col_max(s_meta[c]), ch.off_meta))
        acc = (jnp.exp2(m - m_new) * acc + _pv(ch.vt_tile(t), _probs(s, m_new, off))
               + _pv(ch.vt_meta, _probs(s_meta[c], m_new, ch.off_meta)))
        outs.setdefault(ch.sub, []).append(acc[0:FOX_DIM, :] / acc[FOX_DIM:FOX_DIM + 1, :])
    for u, out_t in outs.items():
        o_ref[_key_rows(Q_SUB * step + u), :] = jnp.concatenate(out_t, axis=0).T.astype(BF16)


def _key_rows(j):
    return pl.ds(pl.multiple_of(j * ATT_TILE, ATT_TILE), ATT_TILE)


def _query_steps(n_keys, one_step):
    def body(step, carry):
        one_step(step)
        return carry
    lax.fori_loop(0, n_keys // (Q_SUB * ATT_TILE), body, 0)


def _mla_attn_kernel(q_ref, k_ref, vt_ref, km_ref, vmt_ref, o_ref, *scratch):
    def one_step(step):
        chains = []
        for u in range(Q_SUB):
            for hh in range(2):
                lanes = slice(hh * LANE, (hh + 1) * LANE)
                rows64 = slice(hh * MLA_V, (hh + 1) * MLA_V)
                chains.append(_Chain(
                    q=q_ref[_key_rows(Q_SUB * step + u), lanes], k_meta=km_ref[:, lanes], vt_meta=vmt_ref[rows64, :],
                    k_tile=lambda j, lanes=lanes: k_ref[_key_rows(j), lanes],
                    vt_tile=lambda j, rows64=rows64: vt_ref[j, rows64, :],
                    off_meta=None, off_tile=None, sub=u))
        _attend(step, chains, *scratch, o_ref)

    _query_steps(k_ref.shape[0], one_step)


def _fox_attn_kernel(beta_ref, q_ref, k_ref, ex_ref, spread_ref, vt_ref, km_ref, exm_ref, vmt_ref, crow_ref, o_ref,
                     kaug_ref, *scratch):
    b = pl.program_id(0)
    hp = pl.program_id(1)
    n_tiles = k_ref.shape[0] // ATT_TILE
    lane = lax.broadcasted_iota(jnp.int32, (1, LANE), 1)
    in_a = lane < FOX_DIM

    def spread(packed):
        return jnp.dot(packed, spread_ref[...], preferred_element_type=F32).astype(BF16)

    for c in range(n_tiles):
        rows = slice(c * ATT_TILE, (c + 1) * ATT_TILE)
        kk = k_ref[rows, :]
        ee = spread(ex_ref[rows, :])
        kaug_ref[0, rows, :] = jnp.where(in_a, kk, ee)
        kaug_ref[1, rows, :] = jnp.where(in_a, ee, kk)

    km = km_ref[...]
    exm = spread(exm_ref[...])

    def one_step(step):
        chains = []
        for u in range(Q_SUB):
            tile = Q_SUB * step + u
            q = q_ref[_key_rows(tile), :]
            for hh in range(2):
                own = in_a if hh == 0 else jnp.logical_not(in_a)
                base = FOX_DIM if hh == 0 else 0
                ones = jnp.where((lane >= base) & (lane < base + 3), 1.0, 0.0).astype(BF16)
                rows64 = slice(hh * FOX_DIM, (hh + 1) * FOX_DIM)
                cum_q = crow_ref[2 * hp + hh, pl.ds(tile, 1), :]
                beta_base = b * n_tiles * LANE + 2 * hp + hh
                chains.append(_Chain(
                    q=jnp.where(own, q, ones), k_meta=jnp.where(own, km, exm), vt_meta=vmt_ref[rows64, :],
                    k_tile=lambda j, hh=hh: kaug_ref[hh, _key_rows(j), :],
                    vt_tile=lambda j, rows64=rows64: vt_ref[j, rows64, :],
                    off_meta=cum_q,
                    off_tile=lambda j, cum_q=cum_q, beta_base=beta_base: cum_q - beta_ref[beta_base + j * LANE],
                    sub=u))
        _attend(step, chains, *scratch, o_ref)

    _query_steps(k_ref.shape[0], one_step)


def _attention_specs(batch, seq, width):
    n_tiles = seq // ATT_TILE
    q_spec = pl.BlockSpec((None, seq, width), lambda b, hp, *_: (b, 0, hp))
    k_spec = pl.BlockSpec((None, seq, width), lambda b, hp, *_: (b, 0, hp))
    vt_spec = pl.BlockSpec((None, None, n_tiles, LANE, ATT_TILE), lambda b, hp, *_: (b, hp, 0, 0, 0))
    km_spec = pl.BlockSpec((N_META, width), lambda b, hp, *_: (0, hp))
    vmt_spec = pl.BlockSpec((None, LANE, N_META), lambda b, hp, *_: (hp, 0, 0))
    o_spec = pl.BlockSpec((None, seq, LANE), lambda b, hp, *_: (b, 0, hp))
    return q_spec, k_spec, vt_spec, km_spec, vmt_spec, o_spec


def _pair_scratch():
    n = 2 * Q_SUB
    return [pltpu.VMEM((n, ATT_TILE, ATT_TILE), F32), pltpu.VMEM((n, ATT_TILE, ATT_TILE), BF16),
            pltpu.VMEM((n, MLA_V + L_ROWS, ATT_TILE), F32)]


def _mla_attention(q, k, vt, km, vmt):
    batch, seq, _ = q.shape
    q_spec, k_spec, vt_spec, km_spec, vmt_spec, o_spec = _attention_specs(batch, seq, 2 * LANE)
    return pl.pallas_call(
        _mla_attn_kernel,
        grid=(batch, HEAD_PAIRS),
        in_specs=[q_spec, k_spec, vt_spec, km_spec, vmt_spec],
        out_specs=o_spec,
        out_shape=jax.ShapeDtypeStruct((batch, seq, HEADS * MLA_V), BF16),
        scratch_shapes=_pair_scratch(),
        compiler_params=pltpu.CompilerParams(dimension_semantics=("parallel", "parallel"),
                                             vmem_limit_bytes=VMEM_LIMIT),
        name="mla_attention",
    )(q, k, vt, km, vmt)


def _fox_attention(beta, q, k, ex, spread, vt, km, exm, vmt, crow):
    batch, seq, _ = q.shape
    q_spec, k_spec, vt_spec, km_spec, vmt_spec, o_spec = _attention_specs(batch, seq, LANE)
    crow_spec = pl.BlockSpec((None, HEADS, seq // ATT_TILE, ATT_TILE), lambda b, hp, *_: (b, 0, 0, 0))
    ex_spec = pl.BlockSpec((None, seq, LANE), lambda b, hp, *_: (b, 0, 0))
    exm_spec = pl.BlockSpec((N_META, LANE), lambda b, hp, *_: (0, 0))
    spread_spec = pl.BlockSpec((None, LANE, LANE), lambda b, hp, *_: (hp, 0, 0))
    grid_spec = pltpu.PrefetchScalarGridSpec(
        num_scalar_prefetch=1,
        grid=(batch, HEAD_PAIRS),
        in_specs=[q_spec, k_spec, ex_spec, spread_spec, vt_spec, km_spec, exm_spec, vmt_spec, crow_spec],
        out_specs=o_spec,
        scratch_shapes=[pltpu.VMEM((2, seq, LANE), BF16)] + _pair_scratch(),
    )
    return pl.pallas_call(
        _fox_attn_kernel,
        grid_spec=grid_spec,
        out_shape=jax.ShapeDtypeStruct((batch, seq, HEADS * FOX_DIM), BF16),
        compiler_params=pltpu.CompilerParams(dimension_semantics=("parallel", "parallel"),
                                             vmem_limit_bytes=VMEM_LIMIT),
        name="fox_attention",
    )(beta, q, k, ex, spread, vt, km, exm, vmt, crow)


def _merge_kernel(om_ref, szm_ref, of_ref, szf_ref, ga_ref, gb_ref, x_ref, wa_ref, wb_ref, wo_ref, g_ref, out_ref):
    ya = jnp.dot(om_ref[...] * szm_ref[...], wa_ref[...], preferred_element_type=F32)
    yb = jnp.dot(of_ref[...] * szf_ref[...], wb_ref[...], preferred_element_type=F32)
    mixed_in = (ga_ref[...].astype(F32) * ya + gb_ref[...].astype(F32) * yb).astype(BF16)
    mixed = jnp.dot(mixed_in, wo_ref[...], preferred_element_type=F32)
    out_ref[...] = x_ref[...] + _rms(mixed, g_ref[...])


def _merge(om, szm, of, szf, ga, gb, x2d, wa, wb, wo, g):
    rows, d = x2d.shape
    row_spec = pl.BlockSpec((ROW_TILE, d), lambda i: (i, 0))
    return pl.pallas_call(
        _merge_kernel,
        grid=(rows // ROW_TILE,),
        in_specs=[row_spec] * 7 + [_const_spec(w.shape) for w in (wa, wb, wo, g)],
        out_specs=row_spec,
        out_shape=jax.ShapeDtypeStruct((rows, d), F32),
        compiler_params=pltpu.CompilerParams(dimension_semantics=("parallel",), vmem_limit_bytes=VMEM_LIMIT),
        name="merge",
    )(om, szm, of, szf, ga, gb, x2d, wa, wb, wo, g)


def _transpose_kernel(w_ref, o_ref):
    o_ref[...] = w_ref[...].T.astype(BF16)


def _transpose_bf16(w):
    k, n = w.shape
    cols = 2 * LANE
    return pl.pallas_call(
        _transpose_kernel,
        grid=(n // cols,),
        in_specs=[pl.BlockSpec((k, cols), lambda i: (0, i))],
        out_specs=pl.BlockSpec((cols, k), lambda i: (i, 0)),
        out_shape=jax.ShapeDtypeStruct((n, k), BF16),
        name="weight_transpose",
    )(w)


def _rotate_half_cols(w):
    half = w.shape[-1] // 2
    return jnp.concatenate([-w[..., half:], w[..., :half]], axis=-1)


def _pad_cols(w, before, total):
    return jnp.pad(w, ((0, 0), (before, total - before - w.shape[1])))


def _in_proj_weights(w_in):
    widths = (MLA_Q_RANK, MLA_KV_RANK, MLA_ROPE, HEADS * MLA_V, HEADS * FOX_DIM, HEADS * FOX_DIM,
              HEADS * FOX_DIM, HEADS, HEADS * FOX_DIM, w_in.shape[0], w_in.shape[0])
    assert sum(widths) == w_in.shape[1]
    bounds = np.cumsum((0,) + widths)
    wt = w_in.T
    w_cq, w_ckv, w_kpe, w_zm, w_fq, w_fk, w_fv, w_fl, w_zf, w_ga, w_gb = (
        wt[bounds[n]:bounds[n + 1]] for n in range(len(widths)))
    w_small = jnp.concatenate([
        w_cq, w_ckv,
        _pad_cols(w_fl.T, 0, KPE_LANE).T, _pad_cols(w_kpe.T, 0, LANE - KPE_LANE).T,
        _pad_cols(_rotate_half_cols(w_kpe.T), KPE_LANE, LANE).T], axis=0)
    assert w_small.shape[0] == SMALL_W
    return [w.astype(BF16) for w in (w_small, w_zm, w_fq, w_fk, w_fv, w_zf, w_ga, w_gb)]


def _head_blocks(w, per_head, first, last, at, width, fn=lambda blk: blk):
    blocks = [_pad_cols(fn(w[:, h * per_head + first:h * per_head + last]), at, width) for h in range(HEADS)]
    return jnp.concatenate(blocks, axis=1)


def _mla_weights(w_uq, w_ukv):
    dq = MLA_NOPE + MLA_ROPE
    dkv = MLA_NOPE + MLA_V
    wq_main = _head_blocks(w_uq, dq, 0, dq, 0, LANE)
    wq_rot = _head_blocks(w_uq, dq, MLA_NOPE, dq, MLA_NOPE, LANE, _rotate_half_cols)
    wk_main = _head_blocks(w_ukv, dkv, 0, MLA_NOPE, 0, LANE)
    wv_t = _transpose_bf16(_head_blocks(w_ukv, dkv, MLA_NOPE, dkv, 0, MLA_V))
    return [w.astype(BF16) for w in (wq_main, wq_rot, wk_main)] + [wv_t]


def _rope_tables(first_pos, n_pos):
    half = MLA_ROPE // 2
    inv_freq = ROPE_THETA ** (-np.arange(half, dtype=np.float64) / half)
    ang = (first_pos + np.arange(n_pos, dtype=np.float64))[:, None] * inv_freq[None, :]
    cos = np.concatenate([np.cos(ang), np.cos(ang)], axis=1)
    sin = np.concatenate([np.sin(ang), np.sin(ang)], axis=1)
    pad = np.zeros((n_pos, LANE - MLA_NOPE - MLA_ROPE))
    ones = np.ones((n_pos, MLA_NOPE))
    zeros = np.zeros((n_pos, MLA_NOPE))
    cos_q = np.concatenate([ones, cos, pad], axis=1) * (MLA_SCALE * LOG2E)
    sin_q = np.concatenate([zeros, sin, pad], axis=1) * (MLA_SCALE * LOG2E)
    cos_k = np.concatenate([zeros, cos, pad], axis=1)
    sin_k = np.concatenate([zeros, sin, pad], axis=1)
    return [jnp.asarray(t, F32) for t in (cos_q, sin_q, cos_k, sin_k)]


def _placement():
    pack = np.zeros((3, LANE, LANE), np.float32)
    spread = np.zeros((HEAD_PAIRS, LANE, LANE), np.float32)
    for h in range(HEADS):
        base = FOX_DIM if h % 2 == 0 else 0
        for part in range(3):
            pack[part, h, HEADS * part + h] = 1.0
            spread[h // 2, HEADS * part + h, base + part] = 1.0
    return jnp.asarray(pack, BF16), jnp.asarray(spread, BF16)


def kernel(x, meta_tokens, pre_norm_g, w_in, fox_forget_b, mla_q_norm_g, mla_kv_norm_g, w_uq, w_ukv,
           w_br_mla, w_br_fox, w_out, post_norm_g):
    batch, seq, d = x.shape
    assert pre_norm_g.shape[0] == 1, "one layer supported"
    assert meta_tokens.shape[0] == N_META and seq % (Q_SUB * ATT_TILE) == 0 and ATT_TILE == ROW_TILE
    n_tiles = seq // ATT_TILE
    x2d = x.reshape(batch * seq, d)

    in_w = _in_proj_weights(w_in[0])
    mla_w = _mla_weights(w_uq[0], w_ukv[0])
    g_pre = pre_norm_g.astype(F32)
    gq = mla_q_norm_g.astype(F32)
    gkv = mla_kv_norm_g.astype(F32)
    bias_row = jnp.pad(fox_forget_b.astype(F32), ((0, 0), (0, LANE - HEADS)))

    vt_shape = (batch, HEAD_PAIRS, n_tiles, LANE, ATT_TILE)
    vt_spec = pl.BlockSpec((None, HEAD_PAIRS, None, LANE, ROW_TILE), lambda i: (i // n_tiles, 0, i % n_tiles, 0, 0))
    vt_meta_shape = (HEAD_PAIRS, LANE, N_META)
    vt_meta_spec = pl.BlockSpec(vt_meta_shape, lambda i: (0, 0, 0))

    fl, szm, fq, fk, fvt, szf, ga, gb, q, k, vt = _in_proj(
        x2d, g_pre, in_w, gq, gkv, mla_w, _rope_tables(N_META, seq), ROW_TILE, n_tiles, vt_shape, vt_spec)
    fl_m, _, _, fk_m, fvt_m, _, _, _, _, k_m, vt_m = _in_proj(
        meta_tokens.astype(F32), g_pre, in_w, gq, gkv, mla_w, _rope_tables(0, N_META), N_META, 1,
        vt_meta_shape, vt_meta_spec)
    pack, spread = _placement()
    ex, crow, beta, ex_m = _decay_prep(fl, fl_m, bias_row, pack, batch, seq)

    o_mla = _mla_attention(q.reshape(batch, seq, -1), k.reshape(batch, seq, -1), vt, k_m, vt_m)
    o_fox = _fox_attention(beta.reshape(-1),fq.reshape(batch, seq, -1), fk.reshape(batch, seq, -1),
                           ex.reshape(batch, seq, -1), spread, fvt, fk_m, ex_m, fvt_m,
                           crow)

    out = _merge(o_mla.reshape(batch * seq, -1), szm, o_fox.reshape(batch * seq, -1), szf, ga, gb, x2d,
                 w_br_mla[0].astype(BF16), w_br_fox[0].astype(BF16), w_out[0].astype(BF16),
                 post_norm_g.astype(F32))
    return out.reshape(batch, seq, d)
```

```python
import functools
import math
from typing import Any, Callable, NamedTuple

import numpy as np
import jax
import jax.numpy as jnp
from jax import lax
from jax.experimental import pallas as pl
from jax.experimental.pallas import tpu as pltpu

F32 = jnp.float32
BF16 = jnp.bfloat16

N_META = 16
RMS_EPS = 1e-6
HEADS = 16
MLA_Q_RANK = 256
MLA_KV_RANK = 128
MLA_NOPE = 64
MLA_ROPE = 32
MLA_V = 64
MLA_SCALE = 1.0 / math.sqrt(MLA_NOPE + MLA_ROPE)
ROPE_THETA = 10000.0
FOX_DIM = 64
FOX_SCALE = 1.0 / math.sqrt(FOX_DIM)
LOG2E = math.log2(math.e)

LANE = 128
HEAD_PAIRS = HEADS // 2
SMALL_W = 4 * LANE
KPE_LANE = 64
ROW_TILE = 512
ATT_TILE = 512
Q_SUB = 2
L_ROWS = 16
NEG_INF = -1e30
VMEM_LIMIT = 56 * 1024 * 1024

NT_DIMS = (((1,), (1,)), ((), ()))


def _rms(x, g):
    return x * lax.rsqrt(jnp.mean(x * x, axis=-1, keepdims=True) + RMS_EPS) * g


def _sigmoid(x):
    return 1.0 / (1.0 + jnp.exp(-x))


def _split3(x):
    hi = x.astype(BF16)
    r1 = x - hi.astype(F32)
    mid = r1.astype(BF16)
    lo = (r1 - mid.astype(F32)).astype(BF16)
    return hi, mid, lo


def _const_spec(shape):
    nd = len(shape)
    return pl.BlockSpec(shape, lambda *_: (0,) * nd, pipeline_mode=pl.Buffered(1))


def _in_proj_kernel(x_ref, g_ref, ws_ref, wzm_ref, wfq_ref, wfk_ref, wfvt_ref, wzf_ref, wga_ref, wgb_ref,
                    gq_ref, gkv_ref, wq_ref, wkm_ref, wvt_ref, cosq_ref, sinq_ref, cosk_ref, sink_ref,
                    fl_ref, szm_ref, fq_ref, fk_ref, fvt_ref, szf_ref, ga_ref, gb_ref, q_ref, k_ref, vt_ref):
    u = _rms(x_ref[...], g_ref[...]).astype(BF16)

    def mm(w_ref):
        return lax.dot_general(u, w_ref[...], NT_DIMS, preferred_element_type=F32)

    def store_pairs(out_ref, rows_t):
        for hp in range(HEAD_PAIRS):
            out_ref[hp] = rows_t[hp * LANE:(hp + 1) * LANE, :].astype(BF16)

    z = mm(wzm_ref)
    szm_ref[...] = (z * _sigmoid(z)).astype(BF16)
    fq_ref[...] = (mm(wfq_ref) * (FOX_SCALE * LOG2E)).astype(BF16)
    fk_ref[...] = mm(wfk_ref).astype(BF16)
    store_pairs(fvt_ref, lax.dot_general(wfvt_ref[...], u, NT_DIMS, preferred_element_type=F32))
    z = mm(wzf_ref)
    szf_ref[...] = (z * _sigmoid(z)).astype(BF16)
    ga_ref[...] = _sigmoid(mm(wga_ref)).astype(BF16)
    gb_ref[...] = _sigmoid(mm(wgb_ref)).astype(BF16)

    sm = mm(ws_ref)
    mixed = sm[:, 3 * LANE:4 * LANE]
    fl_ref[...] = mixed
    cn = _rms(sm[:, 0:MLA_Q_RANK], gq_ref[...]).astype(BF16)
    kn = _rms(sm[:, MLA_Q_RANK:MLA_Q_RANK + MLA_KV_RANK], gkv_ref[...]).astype(BF16)
    to_rope_lanes = LANE - MLA_ROPE
    k_rope = mixed * cosk_ref[...] + pltpu.roll(mixed, to_rope_lanes, axis=1) * sink_ref[...]
    qa = jnp.dot(cn, wq_ref[...], preferred_element_type=F32)
    km = jnp.dot(kn, wkm_ref[...], preferred_element_type=F32)
    cosq = cosq_ref[...]
    sinq = sinq_ref[...]
    for h in range(HEADS):
        sl = slice(h * LANE, (h + 1) * LANE)
        rotated = pltpu.roll(qa[:, sl], to_rope_lanes, axis=1)
        q_ref[:, sl] = (qa[:, sl] * cosq + rotated * sinq).astype(BF16)
        k_ref[:, sl] = (km[:, sl] + k_rope).astype(BF16)
    store_pairs(vt_ref, lax.dot_general(wvt_ref[...], kn, NT_DIMS, preferred_element_type=F32))


def _in_proj(x2d, g, weights, gq, gkv, mla_weights, tables, tm, tiles_per_seq, vt_shape, vt_spec):
    rows, d = x2d.shape
    width = weights[1].shape[0]
    row_spec = lambda w: pl.BlockSpec((tm, w), lambda i: (i, 0))
    tab_spec = pl.BlockSpec((tm, LANE), lambda i: (i % tiles_per_seq, 0))
    wide = HEADS * LANE
    outs = [((rows, LANE), F32, row_spec(LANE))]
    for name in ("szm", "fq", "fk", "fvt", "szf", "ga", "gb"):
        outs.append((vt_shape, BF16, vt_spec) if name == "fvt" else ((rows, width), BF16, row_spec(width)))
    outs += [((rows, wide), BF16, row_spec(wide)), ((rows, wide), BF16, row_spec(wide)), (vt_shape, BF16, vt_spec)]
    consts = [g] + list(weights) + [gq, gkv] + list(mla_weights)
    return pl.pallas_call(
        _in_proj_kernel,
        grid=(rows // tm,),
        in_specs=[row_spec(d)] + [_const_spec(c.shape) for c in consts] + [tab_spec] * 4,
        out_specs=[spec for _, _, spec in outs],
        out_shape=[jax.ShapeDtypeStruct(shape, dtype) for shape, dtype, _ in outs],
        compiler_params=pltpu.CompilerParams(dimension_semantics=("parallel",), vmem_limit_bytes=VMEM_LIMIT),
        name="in_proj",
    )(x2d, *consts, *tables)


def _log2_sigmoid(z):
    return (jnp.minimum(z, 0.0) - jnp.log1p(jnp.exp(-jnp.abs(z)))) * LOG2E


def _tri(n):
    row = lax.broadcasted_iota(jnp.int32, (n, n), 0)
    col = lax.broadcasted_iota(jnp.int32, (n, n), 1)
    return jnp.where(col <= row, 1.0, 0.0).astype(BF16)


def _cumsum_rows(tri, x):
    return sum(jnp.dot(tri, part, preferred_element_type=F32) for part in _split3(x))


def _place(x, place_ref):
    return sum(jnp.dot(part, place_ref[k], preferred_element_type=F32)
               for k, part in enumerate(_split3(x))).astype(BF16)


def _decay_kernel(sm_ref, bias_ref, place_ref, ex_ref, crow_ref, beta_ref, *, tile):
    n_tiles = sm_ref.shape[0] // tile
    tri = _tri(tile)
    eye = jnp.where(lax.broadcasted_iota(jnp.int32, (LANE, LANE), 0)
                    == lax.broadcasted_iota(jnp.int32, (LANE, LANE), 1), 1.0, 0.0).astype(BF16)
    carry = jnp.zeros((1, LANE), F32)
    for j in range(n_tiles):
        rows = slice(j * tile, (j + 1) * tile)
        within = _cumsum_rows(tri, _log2_sigmoid(sm_ref[rows, :] + bias_ref[...]))
        ex_ref[rows, :] = _place(-within, place_ref)
        cum = within + carry
        cum_t = sum(lax.dot_general(eye, part, NT_DIMS, preferred_element_type=F32) for part in _split3(cum))
        crow_ref[:, j, :] = cum_t[0:HEADS, :]
        beta_ref[j:j + 1, :] = carry
        carry = cum[tile - 1:tile, :]


def _decay_meta_kernel(sm_ref, bias_ref, place_ref, ex_ref):
    n = sm_ref.shape[0]
    cum = _cumsum_rows(_tri(n), _log2_sigmoid(sm_ref[...] + bias_ref[...]))
    ex_ref[...] = _place(cum[n - 1:n, :] - cum, place_ref)


def _decay_prep(fl, fl_meta, bias_row, place, batch, seq):
    width = place.shape[2]
    ex, crow, beta = pl.pallas_call(
        functools.partial(_decay_kernel, tile=ATT_TILE),
        grid=(batch,),
        in_specs=[pl.BlockSpec((seq, LANE), lambda b: (b, 0)), _const_spec(bias_row.shape), _const_spec(place.shape)],
        out_specs=[pl.BlockSpec((seq, width), lambda b: (b, 0)),
                   pl.BlockSpec((None, HEADS, seq // ATT_TILE, ATT_TILE), lambda b: (b, 0, 0, 0)),
                   pl.BlockSpec((None, seq // ATT_TILE, LANE), lambda b: (b, 0, 0))],
        out_shape=[jax.ShapeDtypeStruct((batch * seq, width), BF16),
                   jax.ShapeDtypeStruct((batch, HEADS, seq // ATT_TILE, ATT_TILE), F32),
                   jax.ShapeDtypeStruct((batch, seq // ATT_TILE, LANE), F32)],
        compiler_params=pltpu.CompilerParams(dimension_semantics=("parallel",), vmem_limit_bytes=VMEM_LIMIT),
        name="decay_prep",
    )(fl, bias_row, place)
    ex_meta = pl.pallas_call(
        _decay_meta_kernel,
        grid=(1,),
        in_specs=[pl.BlockSpec((N_META, LANE), lambda i: (0, 0)), _const_spec(bias_row.shape),
                  _const_spec(place.shape)],
        out_specs=pl.BlockSpec((N_META, width), lambda i: (0, 0)),
        out_shape=jax.ShapeDtypeStruct((N_META, width), BF16),
        name="decay_prep_meta",
    )(fl_meta, bias_row, place)
    return ex, crow, beta, ex_meta


class _Chain(NamedTuple):
    q: jax.Array
    k_meta: jax.Array
    vt_meta: jax.Array
    k_tile: Callable
    vt_tile: Callable
    off_meta: Any
    off_tile: Any
    sub: int


def _scores(q_h, k_t):
    return lax.dot_general(k_t, q_h, NT_DIMS, preferred_element_type=F32)


def _col_max(s):
    return jnp.max(s, axis=0, keepdims=True)


def _pv(vt_t, p):
    v_aug = jnp.concatenate([vt_t, jnp.ones((L_ROWS, vt_t.shape[1]), BF16)], axis=0)
    return jnp.dot(v_aug, p, preferred_element_type=F32)


def _shifted(m_tile, off):
    return m_tile if off is None else m_tile + off


def _probs(s, m_new, off):
    return jnp.exp2(s - (m_new if off is None else m_new - off)).astype(BF16)


def _causal(shape, first_query):
    return (lax.broadcasted_iota(jnp.int32, shape, 0)
            <= lax.broadcasted_iota(jnp.int32, shape, 1) + first_query)


def _attend(step, chains, s_ref, p_ref, acc_ref, o_ref):
    n = len(chains)
    n_common = Q_SUB * step

    s_meta = [_scores(ch.q, ch.k_meta) for ch in chains]
    tile_max = []
    for c, ch in enumerate(chains):
        s = _scores(ch.q, ch.k_tile(0))
        s_ref[c] = s
        tile_max.append(_col_max(s))
        p_ref[c] = jnp.zeros(p_ref.shape[1:], BF16)
        acc_ref[c] = jnp.zeros(acc_ref.shape[1:], F32)
    ms = tuple(jnp.full((1, ATT_TILE), NEG_INF, F32) for _ in chains)
    alphas = tuple(jnp.ones((1, ATT_TILE), F32) for _ in chains)

    def pending_pv(c, ch, j, alpha):
        acc_ref[c] = alpha * acc_ref[c] + _pv(ch.vt_tile(jnp.maximum(j - 1, 0)), p_ref[c])

    def body(j, carry):
        ms, alphas, tile_max = carry
        cur = [s_ref[c] for c in range(n)]
        for c, ch in enumerate(chains):
            pending_pv(c, ch, j, alphas[c])
        new_ms, new_alphas, probs = [], [], []
        for c, ch in enumerate(chains):
            off = None if ch.off_tile is None else ch.off_tile(j)
            m_new = jnp.maximum(ms[c], _shifted(tile_max[c], off))
            new_alphas.append(jnp.exp2(ms[c] - m_new))
            probs.append(_probs(cur[c], m_new, off))
            new_ms.append(m_new)
        nxt = [_scores(ch.q, ch.k_tile(j + 1)) for ch in chains]
        for c in range(n):
            p_ref[c] = probs[c]
            s_ref[c] = nxt[c]
        return tuple(new_ms), tuple(new_alphas), tuple(_col_max(s) for s in nxt)

    def unrolled(t, carry):
        for u in range(Q_SUB):
            carry = body(Q_SUB * t + u, carry)
        return carry

    ms, alphas, tile_max = lax.fori_loop(0, step, unrolled, (ms, alphas, tuple(tile_max)))

    outs = {}
    for c, ch in enumerate(chains):
        pending_pv(c, ch, n_common, alphas[c])
        m, acc, s, s_max, t = ms[c], acc_ref[c], s_ref[c], tile_max[c], n_common
        for _ in range(ch.sub):
            off = None if ch.off_tile is None else ch.off_tile(t)
            m_new = jnp.maximum(m, _shifted(s_max, off))
            acc = jnp.exp2(m - m_new) * acc + _pv(ch.vt_tile(t), _probs(s, m_new, off))
            m, t = m_new, t + 1
            s = _scores(ch.q, ch.k_tile(t))
            s_max = _col_max(s)
        s = jnp.where(_causal(s.shape, 0), s, NEG_INF)
        off = None if ch.off_tile is None else ch.off_tile(t)
        m_new = jnp.maximum(jnp.maximum(m, _shifted(_col_max(s), off)), _shifted(_col_max(s_meta[c]), ch.off_meta))
        acc = (jnp.exp2(m - m_new) * acc + _pv(ch.vt_tile(t), _probs(s, m_new, off))
               + _pv(ch.vt_meta, _probs(s_meta[c], m_new, ch.off_meta)))
        outs.setdefault(ch.sub, []).append(acc[0:FOX_DIM, :] / acc[FOX_DIM:FOX_DIM + 1, :])
    for u, out_t in outs.items():
        o_ref[_key_rows(Q_SUB * step + u), :] = jnp.concatenate(out_t, axis=0).T.astype(BF16)


def _key_rows(j):
    return pl.ds(pl.multiple_of(j * ATT_TILE, ATT_TILE), ATT_TILE)


def _query_steps(n_keys, one_step):
    def body(step, carry):
        one_step(step)
        return carry
    lax.fori_loop(0, n_keys // (Q_SUB * ATT_TILE), body, 0)


def _mla_attn_kernel(q_ref, k_ref, vt_ref, km_ref, vmt_ref, o_ref, *scratch):
    def one_step(step):
        chains = []
        for u in range(Q_SUB):
            for hh in range(2):
                lanes = slice(hh * LANE, (hh + 1) * LANE)
                rows64 = slice(hh * MLA_V, (hh + 1) * MLA_V)
                chains.append(_Chain(
                    q=q_ref[_key_rows(Q_SUB * step + u), lanes], k_meta=km_ref[:, lanes], vt_meta=vmt_ref[rows64, :],
                    k_tile=lambda j, lanes=lanes: k_ref[_key_rows(j), lanes],
                    vt_tile=lambda j, rows64=rows64: vt_ref[j, rows64, :],
                    off_meta=None, off_tile=None, sub=u))
        _attend(step, chains, *scratch, o_ref)

    _query_steps(k_ref.shape[0], one_step)


def _fox_attn_kernel(beta_ref, q_ref, k_ref, ex_ref, spread_ref, vt_ref, km_ref, exm_ref, vmt_ref, crow_ref, o_ref,
                     kaug_ref, *scratch):
    b = pl.program_id(0)
    hp = pl.program_id(1)
    n_tiles = k_ref.shape[0] // ATT_TILE
    lane = lax.broadcasted_iota(jnp.int32, (1, LANE), 1)
    in_a = lane < FOX_DIM

    def spread(packed):
        return jnp.dot(packed, spread_ref[...], preferred_element_type=F32).astype(BF16)

    for c in range(n_tiles):
        rows = slice(c * ATT_TILE, (c + 1) * ATT_TILE)
        kk = k_ref[rows, :]
        ee = spread(ex_ref[rows, :])
        kaug_ref[0, rows, :] = jnp.where(in_a, kk, ee)
        kaug_ref[1, rows, :] = jnp.where(in_a, ee, kk)

    km = km_ref[...]
    exm = spread(exm_ref[...])

    def one_step(step):
        chains = []
        for u in range(Q_SUB):
            tile = Q_SUB * step + u
            q = q_ref[_key_rows(tile), :]
            for hh in range(2):
                own = in_a if hh == 0 else jnp.logical_not(in_a)
                base = FOX_DIM if hh == 0 else 0
                ones = jnp.where((lane >= base) & (lane < base + 3), 1.0, 0.0).astype(BF16)
                rows64 = slice(hh * FOX_DIM, (hh + 1) * FOX_DIM)
                cum_q = crow_ref[2 * hp + hh, pl.ds(tile, 1), :]
                beta_base = b * n_tiles * LANE + 2 * hp + hh
                chains.append(_Chain(
                    q=jnp.where(own, q, ones), k_meta=jnp.where(own, km, exm), vt_meta=vmt_ref[rows64, :],
                    k_tile=lambda j, hh=hh: kaug_ref[hh, _key_rows(j), :],
                    vt_tile=lambda j, rows64=rows64: vt_ref[j, rows64, :],
                    off_meta=cum_q,
                    off_tile=lambda j, cum_q=cum_q, beta_base=beta_base: cum_q - beta_ref[beta_base + j * LANE],
                    sub=u))
        _attend(step, chains, *scratch, o_ref)

    _query_steps(k_ref.shape[0], one_step)


def _attention_specs(batch, seq, width):
    n_tiles = seq // ATT_TILE
    q_spec = pl.BlockSpec((None, seq, width), lambda b, hp, *_: (b, 0, hp))
    k_spec = pl.BlockSpec((None, seq, width), lambda b, hp, *_: (b, 0, hp))
    vt_spec = pl.BlockSpec((None, None, n_tiles, LANE, ATT_TILE), lambda b, hp, *_: (b, hp, 0, 0, 0))
    km_spec = pl.BlockSpec((N_META, width), lambda b, hp, *_: (0, hp))
    vmt_spec = pl.BlockSpec((None, LANE, N_META), lambda b, hp, *_: (hp, 0, 0))
    o_spec = pl.BlockSpec((None, seq, LANE), lambda b, hp, *_: (b, 0, hp))
    return q_spec, k_spec, vt_spec, km_spec, vmt_spec, o_spec


def _pair_scratch():
    n = 2 * Q_SUB
    return [pltpu.VMEM((n, ATT_TILE, ATT_TILE), F32), pltpu.VMEM((n, ATT_TILE, ATT_TILE), BF16),
            pltpu.VMEM((n, MLA_V + L_ROWS, ATT_TILE), F32)]


def _mla_attention(q, k, vt, km, vmt):
    batch, seq, _ = q.shape
    q_spec, k_spec, vt_spec, km_spec, vmt_spec, o_spec = _attention_specs(batch, seq, 2 * LANE)
    return pl.pallas_call(
        _mla_attn_kernel,
        grid=(batch, HEAD_PAIRS),
        in_specs=[q_spec, k_spec, vt_spec, km_spec, vmt_spec],
        out_specs=o_spec,
        out_shape=jax.ShapeDtypeStruct((batch, seq, HEADS * MLA_V), BF16),
        scratch_shapes=_pair_scratch(),
        compiler_params=pltpu.CompilerParams(dimension_semantics=("parallel", "parallel"),
                                             vmem_limit_bytes=VMEM_LIMIT),
        name="mla_attention",
    )(q, k, vt, km, vmt)


def _fox_attention(beta, q, k, ex, spread, vt, km, exm, vmt, crow):
    batch, seq, _ = q.shape
    q_spec, k_spec, vt_spec, km_spec, vmt_spec, o_spec = _attention_specs(batch, seq, LANE)
    crow_spec = pl.BlockSpec((None, HEADS, seq // ATT_TILE, ATT_TILE), lambda b, hp, *_: (b, 0, 0, 0))
    ex_spec = pl.BlockSpec((None, seq, LANE), lambda b, hp, *_: (b, 0, 0))
    exm_spec = pl.BlockSpec((N_META, LANE), lambda b, hp, *_: (0, 0))
    spread_spec = pl.BlockSpec((None, LANE, LANE), lambda b, hp, *_: (hp, 0, 0))
    grid_spec = pltpu.PrefetchScalarGridSpec(
        num_scalar_prefetch=1,
        grid=(batch, HEAD_PAIRS),
        in_specs=[q_spec, k_spec, ex_spec, spread_spec, vt_spec, km_spec, exm_spec, vmt_spec, crow_spec],
        out_specs=o_spec,
        scratch_shapes=[pltpu.VMEM((2, seq, LANE), BF16)] + _pair_scratch(),
    )
    return pl.pallas_call(
        _fox_attn_kernel,
        grid_spec=grid_spec,
        out_shape=jax.ShapeDtypeStruct((batch, seq, HEADS * FOX_DIM), BF16),
        compiler_params=pltpu.CompilerParams(dimension_semantics=("parallel", "parallel"),
                                             vmem_limit_bytes=VMEM_LIMIT),
        name="fox_attention",
    )(beta, q, k, ex, spread, vt, km, exm, vmt, crow)


def _merge_kernel(om_ref, szm_ref, of_ref, szf_ref, ga_ref, gb_ref, x_ref, wa_ref, wb_ref, wo_ref, g_ref, out_ref):
    ya = jnp.dot(om_ref[...] * szm_ref[...], wa_ref[...], preferred_element_type=F32)
    yb = jnp.dot(of_ref[...] * szf_ref[...], wb_ref[...], preferred_element_type=F32)
    mixed_in = (ga_ref[...].astype(F32) * ya + gb_ref[...].astype(F32) * yb).astype(BF16)
    mixed = jnp.dot(mixed_in, wo_ref[...], preferred_element_type=F32)
    out_ref[...] = x_ref[...] + _rms(mixed, g_ref[...])


def _merge(om, szm, of, szf, ga, gb, x2d, wa, wb, wo, g):
    rows, d = x2d.shape
    row_spec = pl.BlockSpec((ROW_TILE, d), lambda i: (i, 0))
    return pl.pallas_call(
        _merge_kernel,
        grid=(rows // ROW_TILE,),
        in_specs=[row_spec] * 7 + [_const_spec(w.shape) for w in (wa, wb, wo, g)],
        out_specs=row_spec,
        out_shape=jax.ShapeDtypeStruct((rows, d), F32),
        compiler_params=pltpu.CompilerParams(dimension_semantics=("parallel",), vmem_limit_bytes=VMEM_LIMIT),
        name="merge",
    )(om, szm, of, szf, ga, gb, x2d, wa, wb, wo, g)


def _transpose_kernel(w_ref, o_ref):
    o_ref[...] = w_ref[...].T.astype(BF16)


def _transpose_bf16(w):
    k, n = w.shape
    cols = 2 * LANE
    return pl.pallas_call(
        _transpose_kernel,
        grid=(n // cols,),
        in_specs=[pl.BlockSpec((k, cols), lambda i: (0, i))],
        out_specs=pl.BlockSpec((cols, k), lambda i: (i, 0)),
        out_shape=jax.ShapeDtypeStruct((n, k), BF16),
        name="weight_transpose",
    )(w)


def _rotate_half_cols(w):
    half = w.shape[-1] // 2
    return jnp.concatenate([-w[..., half:], w[..., :half]], axis=-1)


def _pad_cols(w, before, total):
    return jnp.pad(w, ((0, 0), (before, total - before - w.shape[1])))


def _in_proj_weights(w_in):
    widths = (MLA_Q_RANK, MLA_KV_RANK, MLA_ROPE, HEADS * MLA_V, HEADS * FOX_DIM, HEADS * FOX_DIM,
              HEADS * FOX_DIM, HEADS, HEADS * FOX_DIM, w_in.shape[0], w_in.shape[0])
    assert sum(widths) == w_in.shape[1]
    bounds = np.cumsum((0,) + widths)
    wt = w_in.T
    w_cq, w_ckv, w_kpe, w_zm, w_fq, w_fk, w_fv, w_fl, w_zf, w_ga, w_gb = (
        wt[bounds[n]:bounds[n + 1]] for n in range(len(widths)))
    w_small = jnp.concatenate([
        w_cq, w_ckv,
        _pad_cols(w_fl.T, 0, KPE_LANE).T, w_kpe, _rotate_half_cols(w_kpe.T).T], axis=0)
    assert w_small.shape[0] == SMALL_W
    return [w.astype(BF16) for w in (w_small, w_zm, w_fq, w_fk, w_fv, w_zf, w_ga, w_gb)]


def _head_blocks(w, per_head, first, last, at, width, fn=lambda blk: blk):
    blocks = [_pad_cols(fn(w[:, h * per_head + first:h * per_head + last]), at, width) for h in range(HEADS)]
    return jnp.concatenate(blocks, axis=1)


def _mla_weights(w_uq, w_ukv):
    dq = MLA_NOPE + MLA_ROPE
    dkv = MLA_NOPE + MLA_V
    assert dq + MLA_ROPE == LANE
    wq = jnp.concatenate([w for h in range(HEADS) for w in (
        w_uq[:, h * dq:(h + 1) * dq], _rotate_half_cols(w_uq[:, h * dq + MLA_NOPE:(h + 1) * dq]))], axis=1)
    wk_main = _head_blocks(w_ukv, dkv, 0, MLA_NOPE, 0, LANE)
    wv_t = _transpose_bf16(_head_blocks(w_ukv, dkv, MLA_NOPE, dkv, 0, MLA_V))
    return [w.astype(BF16) for w in (wq, wk_main)] + [wv_t]


def _rope_tables(first_pos, n_pos):
    half = MLA_ROPE // 2
    inv_freq = ROPE_THETA ** (-np.arange(half, dtype=np.float64) / half)
    ang = (first_pos + np.arange(n_pos, dtype=np.float64))[:, None] * inv_freq[None, :]
    cos = np.concatenate([np.cos(ang), np.cos(ang)], axis=1)
    sin = np.concatenate([np.sin(ang), np.sin(ang)], axis=1)
    pad = np.zeros((n_pos, LANE - MLA_NOPE - MLA_ROPE))
    ones = np.ones((n_pos, MLA_NOPE))
    zeros = np.zeros((n_pos, MLA_NOPE))
    cos_q = np.concatenate([ones, cos, pad], axis=1) * (MLA_SCALE * LOG2E)
    sin_q = np.concatenate([zeros, sin, pad], axis=1) * (MLA_SCALE * LOG2E)
    cos_k = np.concatenate([zeros, cos, pad], axis=1)
    sin_k = np.concatenate([zeros, sin, pad], axis=1)
    return [jnp.asarray(t, F32) for t in (cos_q, sin_q, cos_k, sin_k)]


def _placement():
    pack = np.zeros((3, LANE, LANE), np.float32)
    spread = np.zeros((HEAD_PAIRS, LANE, LANE), np.float32)
    for h in range(HEADS):
        base = FOX_DIM if h % 2 == 0 else 0
        for part in range(3):
            pack[part, h, HEADS * part + h] = 1.0
            spread[h // 2, HEADS * part + h, base + part] = 1.0
    return jnp.asarray(pack, BF16), jnp.asarray(spread, BF16)


def kernel(x, meta_tokens, pre_norm_g, w_in, fox_forget_b, mla_q_norm_g, mla_kv_norm_g, w_uq, w_ukv,
           w_br_mla, w_br_fox, w_out, post_norm_g):
    batch, seq, d = x.shape
    assert pre_norm_g.shape[0] == 1, "one layer supported"
    assert meta_tokens.shape[0] == N_META and seq % (Q_SUB * ATT_TILE) == 0 and ATT_TILE == ROW_TILE
    n_tiles = seq // ATT_TILE
    x2d = x.reshape(batch * seq, d)

    in_w = _in_proj_weights(w_in[0])
    mla_w = _mla_weights(w_uq[0], w_ukv[0])
    g_pre = pre_norm_g.astype(F32)
    gq = mla_q_norm_g.astype(F32)
    gkv = mla_kv_norm_g.astype(F32)
    bias_row = jnp.pad(fox_forget_b.astype(F32), ((0, 0), (0, LANE - HEADS)))

    vt_shape = (batch, HEAD_PAIRS, n_tiles, LANE, ATT_TILE)
    vt_spec = pl.BlockSpec((None, HEAD_PAIRS, None, LANE, ROW_TILE), lambda i: (i // n_tiles, 0, i % n_tiles, 0, 0))
    vt_meta_shape = (HEAD_PAIRS, LANE, N_META)
    vt_meta_spec = pl.BlockSpec(vt_meta_shape, lambda i: (0, 0, 0))

    fl, szm, fq, fk, fvt, szf, ga, gb, q, k, vt = _in_proj(
        x2d, g_pre, in_w, gq, gkv, mla_w, _rope_tables(N_META, seq), ROW_TILE, n_tiles, vt_shape, vt_spec)
    fl_m, _, _, fk_m, fvt_m, _, _, _, _, k_m, vt_m = _in_proj(
        meta_tokens.astype(F32), g_pre, in_w, gq, gkv, mla_w, _rope_tables(0, N_META), N_META, 1,
        vt_meta_shape, vt_meta_spec)
    pack, spread = _placement()
    ex, crow, beta, ex_m = _decay_prep(fl, fl_m, bias_row, pack, batch, seq)

    o_mla = _mla_attention(q.reshape(batch, seq, -1), k.reshape(batch, seq, -1), vt, k_m, vt_m)
    o_fox = _fox_attention(beta.reshape(-1),fq.reshape(batch, seq, -1), fk.reshape(batch, seq, -1),
                           ex.reshape(batch, seq, -1), spread, fvt, fk_m, ex_m, fvt_m,
                           crow)

    out = _merge(o_mla.reshape(batch * seq, -1), szm, o_fox.reshape(batch * seq, -1), szf, ga, gb, x2d,
                 w_br_mla[0].astype(BF16), w_br_fox[0].astype(BF16), w_out[0].astype(BF16),
                 post_norm_g.astype(F32))
    return out.reshape(batch, seq, d)
```

```python
import functools
import math
from typing import Any, Callable, NamedTuple

import numpy as np
import jax
import jax.numpy as jnp
from jax import lax
from jax.experimental import pallas as pl
from jax.experimental.pallas import tpu as pltpu

F32 = jnp.float32
BF16 = jnp.bfloat16

N_META = 16
RMS_EPS = 1e-6
HEADS = 16
MLA_Q_RANK = 256
MLA_KV_RANK = 128
MLA_NOPE = 64
MLA_ROPE = 32
MLA_V = 64
MLA_SCALE = 1.0 / math.sqrt(MLA_NOPE + MLA_ROPE)
ROPE_THETA = 10000.0
FOX_DIM = 64
FOX_SCALE = 1.0 / math.sqrt(FOX_DIM)
LOG2E = math.log2(math.e)

LANE = 128
HEAD_PAIRS = HEADS // 2
SMALL_W = 4 * LANE
KPE_LANE = 64
ROW_TILE = 512
ATT_TILE = 512
Q_SUB = 2
L_ROWS = 16
NEG_INF = -1e30
VMEM_LIMIT = 56 * 1024 * 1024

NT_DIMS = (((1,), (1,)), ((), ()))


def _rms(x, g):
    return x * lax.rsqrt(jnp.mean(x * x, axis=-1, keepdims=True) + RMS_EPS) * g


def _sigmoid(x):
    return 1.0 / (1.0 + jnp.exp(-x))


def _split3(x):
    hi = x.astype(BF16)
    r1 = x - hi.astype(F32)
    mid = r1.astype(BF16)
    lo = (r1 - mid.astype(F32)).astype(BF16)
    return hi, mid, lo


def _const_spec(shape):
    nd = len(shape)
    return pl.BlockSpec(shape, lambda *_: (0,) * nd, pipeline_mode=pl.Buffered(1))


def _in_proj_kernel(x_ref, g_ref, ws_ref, wzm_ref, wfq_ref, wfk_ref, wfvt_ref, wzf_ref, wga_ref, wgb_ref,
                    gq_ref, gkv_ref, wq_ref, wkm_ref, wvt_ref, cosq_ref, sinq_ref, cosk_ref, sink_ref,
                    fl_ref, szm_ref, fq_ref, fk_ref, fvt_ref, szf_ref, ga_ref, gb_ref, q_ref, k_ref, vt_ref):
    u = _rms(x_ref[...], g_ref[...]).astype(BF16)

    def mm(w_ref):
        return lax.dot_general(u, w_ref[...], NT_DIMS, preferred_element_type=F32)

    def store_pairs(out_ref, rows_t):
        for hp in range(HEAD_PAIRS):
            out_ref[hp] = rows_t[hp * LANE:(hp + 1) * LANE, :].astype(BF16)

    z = mm(wzm_ref)
    szm_ref[...] = (z * _sigmoid(z)).astype(BF16)
    fq_ref[...] = (mm(wfq_ref) * (FOX_SCALE * LOG2E)).astype(BF16)
    fk_ref[...] = mm(wfk_ref).astype(BF16)
    store_pairs(fvt_ref, lax.dot_general(wfvt_ref[...], u, NT_DIMS, preferred_element_type=F32))
    z = mm(wzf_ref)
    szf_ref[...] = (z * _sigmoid(z)).astype(BF16)
    ga_ref[...] = _sigmoid(mm(wga_ref)).astype(BF16)
    gb_ref[...] = _sigmoid(mm(wgb_ref)).astype(BF16)

    sm = mm(ws_ref)
    mixed = sm[:, 3 * LANE:4 * LANE]
    fl_ref[...] = mixed
    cn = _rms(sm[:, 0:MLA_Q_RANK], gq_ref[...]).astype(BF16)
    kn = _rms(sm[:, MLA_Q_RANK:MLA_Q_RANK + MLA_KV_RANK], gkv_ref[...]).astype(BF16)
    to_rope_lanes = LANE - MLA_ROPE
    k_rope = mixed * cosk_ref[...] + pltpu.roll(mixed, to_rope_lanes, axis=1) * sink_ref[...]
    qa = jnp.dot(cn, wq_ref[...], preferred_element_type=F32)
    km = jnp.dot(kn, wkm_ref[...], preferred_element_type=F32)
    cosq = cosq_ref[...]
    sinq = sinq_ref[...]
    for h in range(HEADS):
        sl = slice(h * LANE, (h + 1) * LANE)
        rotated = pltpu.roll(qa[:, sl], to_rope_lanes, axis=1)
        q_ref[:, sl] = (qa[:, sl] * cosq + rotated * sinq).astype(BF16)
        k_ref[:, sl] = (km[:, sl] + k_rope).astype(BF16)
    store_pairs(vt_ref, lax.dot_general(wvt_ref[...], kn, NT_DIMS, preferred_element_type=F32))


def _in_proj(x2d, g, weights, gq, gkv, mla_weights, tables, tm, tiles_per_seq, vt_shape, vt_spec):
    rows, d = x2d.shape
    width = weights[1].shape[0]
    row_spec = lambda w: pl.BlockSpec((tm, w), lambda i: (i, 0))
    tab_spec = pl.BlockSpec((tm, LANE), lambda i: (i % tiles_per_seq, 0))
    wide = HEADS * LANE
    outs = [((rows, LANE), F32, row_spec(LANE))]
    for name in ("szm", "fq", "fk", "fvt", "szf", "ga", "gb"):
        outs.append((vt_shape, BF16, vt_spec) if name == "fvt" else ((rows, width), BF16, row_spec(width)))
    outs += [((rows, wide), BF16, row_spec(wide)), ((rows, wide), BF16, row_spec(wide)), (vt_shape, BF16, vt_spec)]
    consts = [g] + list(weights) + [gq, gkv] + list(mla_weights)
    return pl.pallas_call(
        _in_proj_kernel,
        grid=(rows // tm,),
        in_specs=[row_spec(d)] + [_const_spec(c.shape) for c in consts] + [tab_spec] * 4,
        out_specs=[spec for _, _, spec in outs],
        out_shape=[jax.ShapeDtypeStruct(shape, dtype) for shape, dtype, _ in outs],
        compiler_params=pltpu.CompilerParams(dimension_semantics=("parallel",), vmem_limit_bytes=VMEM_LIMIT),
        name="in_proj",
    )(x2d, *consts, *tables)


def _log2_sigmoid(z):
    return (jnp.minimum(z, 0.0) - jnp.log1p(jnp.exp(-jnp.abs(z)))) * LOG2E


def _tri(n):
    row = lax.broadcasted_iota(jnp.int32, (n, n), 0)
    col = lax.broadcasted_iota(jnp.int32, (n, n), 1)
    return jnp.where(col <= row, 1.0, 0.0).astype(BF16)


def _cumsum_rows(tri, x):
    return sum(jnp.dot(tri, part, preferred_element_type=F32) for part in _split3(x))


def _place(x, place_ref):
    return sum(jnp.dot(part, place_ref[k], preferred_element_type=F32)
               for k, part in enumerate(_split3(x))).astype(BF16)


def _decay_kernel(sm_ref, bias_ref, place_ref, ex_ref, crow_ref, beta_ref, *, tile):
    n_tiles = sm_ref.shape[0] // tile
    tri = _tri(tile)
    eye = jnp.where(lax.broadcasted_iota(jnp.int32, (LANE, LANE), 0)
                    == lax.broadcasted_iota(jnp.int32, (LANE, LANE), 1), 1.0, 0.0).astype(BF16)
    carry = jnp.zeros((1, LANE), F32)
    withins = []
    for j in range(0, n_tiles, 2):
        pair = [_log2_sigmoid(sm_ref[(j + e) * tile:(j + e + 1) * tile, :] + bias_ref[...]) for e in range(2)]
        both = _cumsum_rows(tri, jnp.concatenate(pair, axis=1))
        withins += [both[:, 0:LANE], both[:, LANE:]]
    for j in range(n_tiles):
        rows = slice(j * tile, (j + 1) * tile)
        within = withins[j]
        ex_ref[rows, :] = _place(-within, place_ref)
        cum = within + carry
        cum_t = sum(lax.dot_general(eye, part, NT_DIMS, preferred_element_type=F32) for part in _split3(cum))
        crow_ref[:, j, :] = cum_t[0:HEADS, :]
        beta_ref[j:j + 1, :] = carry
        carry = cum[tile - 1:tile, :]


def _decay_meta_kernel(sm_ref, bias_ref, place_ref, ex_ref):
    n = sm_ref.shape[0]
    cum = _cumsum_rows(_tri(n), _log2_sigmoid(sm_ref[...] + bias_ref[...]))
    ex_ref[...] = _place(cum[n - 1:n, :] - cum, place_ref)


def _decay_prep(fl, fl_meta, bias_row, place, batch, seq):
    width = place.shape[2]
    ex, crow, beta = pl.pallas_call(
        functools.partial(_decay_kernel, tile=ATT_TILE),
        grid=(batch,),
        in_specs=[pl.BlockSpec((seq, LANE), lambda b: (b, 0)), _const_spec(bias_row.shape), _const_spec(place.shape)],
        out_specs=[pl.BlockSpec((seq, width), lambda b: (b, 0)),
                   pl.BlockSpec((None, HEADS, seq // ATT_TILE, ATT_TILE), lambda b: (b, 0, 0, 0)),
                   pl.BlockSpec((None, seq // ATT_TILE, LANE), lambda b: (b, 0, 0))],
        out_shape=[jax.ShapeDtypeStruct((batch * seq, width), BF16),
                   jax.ShapeDtypeStruct((batch, HEADS, seq // ATT_TILE, ATT_TILE), F32),
                   jax.ShapeDtypeStruct((batch, seq // ATT_TILE, LANE), F32)],
        compiler_params=pltpu.CompilerParams(dimension_semantics=("parallel",), vmem_limit_bytes=VMEM_LIMIT),
        name="decay_prep",
    )(fl, bias_row, place)
    ex_meta = pl.pallas_call(
        _decay_meta_kernel,
        grid=(1,),
        in_specs=[pl.BlockSpec((N_META, LANE), lambda i: (0, 0)), _const_spec(bias_row.shape),
                  _const_spec(place.shape)],
        out_specs=pl.BlockSpec((N_META, width), lambda i: (0, 0)),
        out_shape=jax.ShapeDtypeStruct((N_META, width), BF16),
        name="decay_prep_meta",
    )(fl_meta, bias_row, place)
    return ex, crow, beta, ex_meta


class _Chain(NamedTuple):
    q: jax.Array
    k_meta: jax.Array
    vt_meta: jax.Array
    k_tile: Callable
    vt_tile: Callable
    off_meta: Any
    off_tile: Any
    sub: int


def _scores(q_h, k_t):
    return lax.dot_general(k_t, q_h, NT_DIMS, preferred_element_type=F32)


def _col_max(s):
    return jnp.max(s, axis=0, keepdims=True)


def _pv(vt_t, p):
    v_aug = jnp.concatenate([vt_t, jnp.ones((L_ROWS, vt_t.shape[1]), BF16)], axis=0)
    return jnp.dot(v_aug, p, preferred_element_type=F32)


def _shifted(m_tile, off):
    return m_tile if off is None else m_tile + off


def _probs(s, m_new, off):
    return jnp.exp2(s - (m_new if off is None else m_new - off)).astype(BF16)


def _causal(shape, first_query):
    return (lax.broadcasted_iota(jnp.int32, shape, 0)
            <= lax.broadcasted_iota(jnp.int32, shape, 1) + first_query)


def _attend(step, chains, s_ref, p_ref, acc_ref, o_ref):
    n = len(chains)
    n_common = Q_SUB * step
    assert all(ch.sub in (0, 1) for ch in chains)

    def off_of(ch, t):
        return None if ch.off_tile is None else ch.off_tile(t)

    s_meta = [_scores(ch.q, ch.k_meta) for ch in chains]
    ms, alphas, tile_max = [], [], []
    for c, ch in enumerate(chains):
        s = _scores(ch.q, ch.k_tile(0))
        m = jnp.full((1, ATT_TILE), NEG_INF, F32)
        acc_ref[c] = jnp.zeros(acc_ref.shape[1:], F32)
        if ch.sub == 0:
            p_ref[c] = jnp.zeros(p_ref.shape[1:], BF16)
        else:
            m = _shifted(_col_max(s), off_of(ch, 0))
            p_ref[c] = _probs(s, m, off_of(ch, 0))
            s = _scores(ch.q, ch.k_tile(1))
        s_ref[c] = s
        tile_max.append(_col_max(s))
        ms.append(m)
        alphas.append(jnp.ones((1, ATT_TILE), F32))

    def pending_pv(c, ch, j, alpha):
        acc_ref[c] = alpha * acc_ref[c] + _pv(ch.vt_tile(jnp.maximum(j - 1 + ch.sub, 0)), p_ref[c])

    def body(j, carry):
        ms, alphas, tile_max = carry
        cur = [s_ref[c] for c in range(n)]
        for c, ch in enumerate(chains):
            pending_pv(c, ch, j, alphas[c])
        new_ms, new_alphas, probs = [], [], []
        for c, ch in enumerate(chains):
            off = off_of(ch, j + ch.sub)
            m_new = jnp.maximum(ms[c], _shifted(tile_max[c], off))
            new_alphas.append(jnp.exp2(ms[c] - m_new))
            probs.append(_probs(cur[c], m_new, off))
            new_ms.append(m_new)
        nxt = [_scores(ch.q, ch.k_tile(j + 1 + ch.sub)) for ch in chains]
        for c in range(n):
            p_ref[c] = probs[c]
            s_ref[c] = nxt[c]
        return tuple(new_ms), tuple(new_alphas), tuple(_col_max(s) for s in nxt)

    def unrolled(t, carry):
        for u in range(Q_SUB):
            carry = body(Q_SUB * t + u, carry)
        return carry

    ms, alphas, _ = lax.fori_loop(0, step, unrolled, (tuple(ms), tuple(alphas), tuple(tile_max)))

    outs = {}
    for c, ch in enumerate(chains):
        pending_pv(c, ch, n_common, alphas[c])
        t = n_common + ch.sub
        s = jnp.where(_causal((ATT_TILE, ATT_TILE), 0), s_ref[c], NEG_INF)
        off = off_of(ch, t)
        m_new = jnp.maximum(jnp.maximum(ms[c], _shifted(_col_max(s), off)),
                            _shifted(_col_max(s_meta[c]), ch.off_meta))
        acc = (jnp.exp2(ms[c] - m_new) * acc_ref[c] + _pv(ch.vt_tile(t), _probs(s, m_new, off))
               + _pv(ch.vt_meta, _probs(s_meta[c], m_new, ch.off_meta)))
        outs.setdefault(ch.sub, []).append(acc[0:FOX_DIM, :] / acc[FOX_DIM:FOX_DIM + 1, :])
    for u, out_t in outs.items():
        o_ref[_key_rows(Q_SUB * step + u), :] = jnp.concatenate(out_t, axis=0).T.astype(BF16)


def _key_rows(j):
    return pl.ds(pl.multiple_of(j * ATT_TILE, ATT_TILE), ATT_TILE)


def _query_steps(n_keys, one_step):
    def body(step, carry):
        one_step(step)
        return carry
    lax.fori_loop(0, n_keys // (Q_SUB * ATT_TILE), body, 0)


def _mla_attn_kernel(q_ref, k_ref, vt_ref, km_ref, vmt_ref, o_ref, *scratch):
    def one_step(step):
        chains = []
        for u in range(Q_SUB):
            for hh in range(2):
                lanes = slice(hh * LANE, (hh + 1) * LANE)
                rows64 = slice(hh * MLA_V, (hh + 1) * MLA_V)
                chains.append(_Chain(
                    q=q_ref[_key_rows(Q_SUB * step + u), lanes], k_meta=km_ref[:, lanes], vt_meta=vmt_ref[rows64, :],
                    k_tile=lambda j, lanes=lanes: k_ref[_key_rows(j), lanes],
                    vt_tile=lambda j, rows64=rows64: vt_ref[j, rows64, :],
                    off_meta=None, off_tile=None, sub=u))
        _attend(step, chains, *scratch, o_ref)

    _query_steps(k_ref.shape[0], one_step)


def _fox_attn_kernel(beta_ref, q_ref, k_ref, ex_ref, spread_ref, vt_ref, km_ref, exm_ref, vmt_ref, crow_ref, o_ref,
                     kaug_ref, *scratch):
    b = pl.program_id(0)
    hp = pl.program_id(1)
    n_tiles = k_ref.shape[0] // ATT_TILE
    lane = lax.broadcasted_iota(jnp.int32, (1, LANE), 1)
    in_a = lane < FOX_DIM

    def spread(packed):
        return jnp.dot(packed, spread_ref[...], preferred_element_type=F32).astype(BF16)

    for c in range(n_tiles):
        rows = slice(c * ATT_TILE, (c + 1) * ATT_TILE)
        kk = k_ref[rows, :]
        ee = spread(ex_ref[rows, :])
        kaug_ref[0, rows, :] = jnp.where(in_a, kk, ee)
        kaug_ref[1, rows, :] = jnp.where(in_a, ee, kk)

    km = km_ref[...]
    exm = spread(exm_ref[...])

    def one_step(step):
        chains = []
        for u in range(Q_SUB):
            tile = Q_SUB * step + u
            q = q_ref[_key_rows(tile), :]
            for hh in range(2):
                own = in_a if hh == 0 else jnp.logical_not(in_a)
                base = FOX_DIM if hh == 0 else 0
                ones = jnp.where((lane >= base) & (lane < base + 3), 1.0, 0.0).astype(BF16)
                rows64 = slice(hh * FOX_DIM, (hh + 1) * FOX_DIM)
                cum_q = crow_ref[2 * hp + hh, pl.ds(tile, 1), :]
                beta_base = b * n_tiles * LANE + 2 * hp + hh
                chains.append(_Chain(
                    q=jnp.where(own, q, ones), k_meta=jnp.where(own, km, exm), vt_meta=vmt_ref[rows64, :],
                    k_tile=lambda j, hh=hh: kaug_ref[hh, _key_rows(j), :],
                    vt_tile=lambda j, rows64=rows64: vt_ref[j, rows64, :],
                    off_meta=cum_q,
                    off_tile=lambda j, cum_q=cum_q, beta_base=beta_base: cum_q - beta_ref[beta_base + j * LANE],
                    sub=u))
        _attend(step, chains, *scratch, o_ref)

    _query_steps(k_ref.shape[0], one_step)


def _attention_specs(batch, seq, width):
    n_tiles = seq // ATT_TILE
    q_spec = pl.BlockSpec((None, seq, width), lambda b, hp, *_: (b, 0, hp))
    k_spec = pl.BlockSpec((None, seq, width), lambda b, hp, *_: (b, 0, hp))
    vt_spec = pl.BlockSpec((None, None, n_tiles, LANE, ATT_TILE), lambda b, hp, *_: (b, hp, 0, 0, 0))
    km_spec = pl.BlockSpec((N_META, width), lambda b, hp, *_: (0, hp))
    vmt_spec = pl.BlockSpec((None, LANE, N_META), lambda b, hp, *_: (hp, 0, 0))
    o_spec = pl.BlockSpec((None, seq, LANE), lambda b, hp, *_: (b, 0, hp))
    return q_spec, k_spec, vt_spec, km_spec, vmt_spec, o_spec


def _pair_scratch():
    n = 2 * Q_SUB
    return [pltpu.VMEM((n, ATT_TILE, ATT_TILE), F32), pltpu.VMEM((n, ATT_TILE, ATT_TILE), BF16),
            pltpu.VMEM((n, MLA_V + L_ROWS, ATT_TILE), F32)]


def _mla_attention(q, k, vt, km, vmt):
    batch, seq, _ = q.shape
    q_spec, k_spec, vt_spec, km_spec, vmt_spec, o_spec = _attention_specs(batch, seq, 2 * LANE)
    return pl.pallas_call(
        _mla_attn_kernel,
        grid=(batch, HEAD_PAIRS),
        in_specs=[q_spec, k_spec, vt_spec, km_spec, vmt_spec],
        out_specs=o_spec,
        out_shape=jax.ShapeDtypeStruct((batch, seq, HEADS * MLA_V), BF16),
        scratch_shapes=_pair_scratch(),
        compiler_params=pltpu.CompilerParams(dimension_semantics=("parallel", "parallel"),
                                             vmem_limit_bytes=VMEM_LIMIT),
        name="mla_attention",
    )(q, k, vt, km, vmt)


def _fox_attention(beta, q, k, ex, spread, vt, km, exm, vmt, crow):
    batch, seq, _ = q.shape
    q_spec, k_spec, vt_spec, km_spec, vmt_spec, o_spec = _attention_specs(batch, seq, LANE)
    crow_spec = pl.BlockSpec((None, HEADS, seq // ATT_TILE, ATT_TILE), lambda b, hp, *_: (b, 0, 0, 0))
    ex_spec = pl.BlockSpec((None, seq, LANE), lambda b, hp, *_: (b, 0, 0))
    exm_spec = pl.BlockSpec((N_META, LANE), lambda b, hp, *_: (0, 0))
    spread_spec = pl.BlockSpec((None, LANE, LANE), lambda b, hp, *_: (hp, 0, 0))
    grid_spec = pltpu.PrefetchScalarGridSpec(
        num_scalar_prefetch=1,
        grid=(batch, HEAD_PAIRS),
        in_specs=[q_spec, k_spec, ex_spec, spread_spec, vt_spec, km_spec, exm_spec, vmt_spec, crow_spec],
        out_specs=o_spec,
        scratch_shapes=[pltpu.VMEM((2, seq, LANE), BF16)] + _pair_scratch(),
    )
    return pl.pallas_call(
        _fox_attn_kernel,
        grid_spec=grid_spec,
        out_shape=jax.ShapeDtypeStruct((batch, seq, HEADS * FOX_DIM), BF16),
        compiler_params=pltpu.CompilerParams(dimension_semantics=("parallel", "parallel"),
                                             vmem_limit_bytes=VMEM_LIMIT),
        name="fox_attention",
    )(beta, q, k, ex, spread, vt, km, exm, vmt, crow)


def _merge_kernel(om_ref, szm_ref, of_ref, szf_ref, ga_ref, gb_ref, x_ref, wa_ref, wb_ref, wo_ref, g_ref, out_ref):
    ya = jnp.dot(om_ref[...] * szm_ref[...], wa_ref[...], preferred_element_type=F32)
    yb = jnp.dot(of_ref[...] * szf_ref[...], wb_ref[...], preferred_element_type=F32)
    mixed_in = (ga_ref[...].astype(F32) * ya + gb_ref[...].astype(F32) * yb).astype(BF16)
    mixed = jnp.dot(mixed_in, wo_ref[...], preferred_element_type=F32)
    out_ref[...] = x_ref[...] + _rms(mixed, g_ref[...])


def _merge(om, szm, of, szf, ga, gb, x2d, wa, wb, wo, g):
    rows, d = x2d.shape
    row_spec = pl.BlockSpec((ROW_TILE, d), lambda i: (i, 0))
    return pl.pallas_call(
        _merge_kernel,
        grid=(rows // ROW_TILE,),
        in_specs=[row_spec] * 7 + [_const_spec(w.shape) for w in (wa, wb, wo, g)],
        out_specs=row_spec,
        out_shape=jax.ShapeDtypeStruct((rows, d), F32),
        compiler_params=pltpu.CompilerParams(dimension_semantics=("parallel",), vmem_limit_bytes=VMEM_LIMIT),
        name="merge",
    )(om, szm, of, szf, ga, gb, x2d, wa, wb, wo, g)


def _transpose_kernel(w_ref, o_ref):
    o_ref[...] = w_ref[...].T.astype(BF16)


def _transpose_bf16(w):
    k, n = w.shape
    cols = 2 * LANE
    return pl.pallas_call(
        _transpose_kernel,
        grid=(n // cols,),
        in_specs=[pl.BlockSpec((k, cols), lambda i: (0, i))],
        out_specs=pl.BlockSpec((cols, k), lambda i: (i, 0)),
        out_shape=jax.ShapeDtypeStruct((n, k), BF16),
        name="weight_transpose",
    )(w)


def _rotate_half_cols(w):
    half = w.shape[-1] // 2
    return jnp.concatenate([-w[..., half:], w[..., :half]], axis=-1)


def _pad_cols(w, before, total):
    return jnp.pad(w, ((0, 0), (before, total - before - w.shape[1])))


def _in_proj_weights(w_in):
    widths = (MLA_Q_RANK, MLA_KV_RANK, MLA_ROPE, HEADS * MLA_V, HEADS * FOX_DIM, HEADS * FOX_DIM,
              HEADS * FOX_DIM, HEADS, HEADS * FOX_DIM, w_in.shape[0], w_in.shape[0])
    assert sum(widths) == w_in.shape[1]
    bounds = np.cumsum((0,) + widths)
    wt = w_in.T
    w_cq, w_ckv, w_kpe, w_zm, w_fq, w_fk, w_fv, w_fl, w_zf, w_ga, w_gb = (
        wt[bounds[n]:bounds[n + 1]] for n in range(len(widths)))
    w_small = jnp.concatenate([
        w_cq, w_ckv,
        _pad_cols(w_fl.T, 0, KPE_LANE).T, w_kpe, _rotate_half_cols(w_kpe.T).T], axis=0)
    assert w_small.shape[0] == SMALL_W
    return [w.astype(BF16) for w in (w_small, w_zm, w_fq, w_fk, w_fv, w_zf, w_ga, w_gb)]


def _head_blocks(w, per_head, first, last, at, width, fn=lambda blk: blk):
    blocks = [_pad_cols(fn(w[:, h * per_head + first:h * per_head + last]), at, width) for h in range(HEADS)]
    return jnp.concatenate(blocks, axis=1)


def _mla_weights(w_uq, w_ukv):
    dq = MLA_NOPE + MLA_ROPE
    dkv = MLA_NOPE + MLA_V
    assert dq + MLA_ROPE == LANE
    wq = jnp.concatenate([w for h in range(HEADS) for w in (
        w_uq[:, h * dq:(h + 1) * dq], _rotate_half_cols(w_uq[:, h * dq + MLA_NOPE:(h + 1) * dq]))], axis=1)
    wk_main = _head_blocks(w_ukv, dkv, 0, MLA_NOPE, 0, LANE)
    wv_t = _transpose_bf16(_head_blocks(w_ukv, dkv, MLA_NOPE, dkv, 0, MLA_V))
    return [w.astype(BF16) for w in (wq, wk_main)] + [wv_t]


def _rope_tables(first_pos, n_pos):
    half = MLA_ROPE // 2
    inv_freq = ROPE_THETA ** (-np.arange(half, dtype=np.float64) / half)
    ang = (first_pos + np.arange(n_pos, dtype=np.float64))[:, None] * inv_freq[None, :]
    cos = np.concatenate([np.cos(ang), np.cos(ang)], axis=1)
    sin = np.concatenate([np.sin(ang), np.sin(ang)], axis=1)
    pad = np.zeros((n_pos, LANE - MLA_NOPE - MLA_ROPE))
    ones = np.ones((n_pos, MLA_NOPE))
    zeros = np.zeros((n_pos, MLA_NOPE))
    cos_q = np.concatenate([ones, cos, pad], axis=1) * (MLA_SCALE * LOG2E)
    sin_q = np.concatenate([zeros, sin, pad], axis=1) * (MLA_SCALE * LOG2E)
    cos_k = np.concatenate([zeros, cos, pad], axis=1)
    sin_k = np.concatenate([zeros, sin, pad], axis=1)
    return [jnp.asarray(t, F32) for t in (cos_q, sin_q, cos_k, sin_k)]


def _placement():
    pack = np.zeros((3, LANE, LANE), np.float32)
    spread = np.zeros((HEAD_PAIRS, LANE, LANE), np.float32)
    for h in range(HEADS):
        base = FOX_DIM if h % 2 == 0 else 0
        for part in range(3):
            pack[part, h, HEADS * part + h] = 1.0
            spread[h // 2, HEADS * part + h, base + part] = 1.0
    return jnp.asarray(pack, BF16), jnp.asarray(spread, BF16)


def kernel(x, meta_tokens, pre_norm_g, w_in, fox_forget_b, mla_q_norm_g, mla_kv_norm_g, w_uq, w_ukv,
           w_br_mla, w_br_fox, w_out, post_norm_g):
    batch, seq, d = x.shape
    assert pre_norm_g.shape[0] == 1, "one layer supported"
    assert meta_tokens.shape[0] == N_META and seq % (Q_SUB * ATT_TILE) == 0 and ATT_TILE == ROW_TILE
    n_tiles = seq // ATT_TILE
    x2d = x.reshape(batch * seq, d)

    in_w = _in_proj_weights(w_in[0])
    mla_w = _mla_weights(w_uq[0], w_ukv[0])
    g_pre = pre_norm_g.astype(F32)
    gq = mla_q_norm_g.astype(F32)
    gkv = mla_kv_norm_g.astype(F32)
    bias_row = jnp.pad(fox_forget_b.astype(F32), ((0, 0), (0, LANE - HEADS)))

    vt_shape = (batch, HEAD_PAIRS, n_tiles, LANE, ATT_TILE)
    vt_spec = pl.BlockSpec((None, HEAD_PAIRS, None, LANE, ROW_TILE), lambda i: (i // n_tiles, 0, i % n_tiles, 0, 0))
    vt_meta_shape = (HEAD_PAIRS, LANE, N_META)
    vt_meta_spec = pl.BlockSpec(vt_meta_shape, lambda i: (0, 0, 0))

    fl, szm, fq, fk, fvt, szf, ga, gb, q, k, vt = _in_proj(
        x2d, g_pre, in_w, gq, gkv, mla_w, _rope_tables(N_META, seq), ROW_TILE, n_tiles, vt_shape, vt_spec)
    fl_m, _, _, fk_m, fvt_m, _, _, _, _, k_m, vt_m = _in_proj(
        meta_tokens.astype(F32), g_pre, in_w, gq, gkv, mla_w, _rope_tables(0, N_META), N_META, 1,
        vt_meta_shape, vt_meta_spec)
    pack, spread = _placement()
    ex, crow, beta, ex_m = _decay_prep(fl, fl_m, bias_row, pack, batch, seq)

    o_mla = _mla_attention(q.reshape(batch, seq, -1), k.reshape(batch, seq, -1), vt, k_m, vt_m)
    o_fox = _fox_attention(beta.reshape(-1),fq.reshape(batch, seq, -1), fk.reshape(batch, seq, -1),
                           ex.reshape(batch, seq, -1), spread, fvt, fk_m, ex_m, fvt_m,
                           crow)

    out = _merge(o_mla.reshape(batch * seq, -1), szm, o_fox.reshape(batch * seq, -1), szf, ga, gb, x2d,
                 w_br_mla[0].astype(BF16), w_br_fox[0].astype(BF16), w_out[0].astype(BF16),
                 post_norm_g.astype(F32))
    return out.reshape(batch, seq, d)
```

```python
import functools
import math
from typing import Any, Callable, NamedTuple

import numpy as np
import jax
import jax.numpy as jnp
from jax import lax
from jax.experimental import pallas as pl
from jax.experimental.pallas import tpu as pltpu

F32 = jnp.float32
BF16 = jnp.bfloat16

N_META = 16
RMS_EPS = 1e-6
HEADS = 16
MLA_Q_RANK = 256
MLA_KV_RANK = 128
MLA_NOPE = 64
MLA_ROPE = 32
MLA_V = 64
MLA_SCALE = 1.0 / math.sqrt(MLA_NOPE + MLA_ROPE)
ROPE_THETA = 10000.0
FOX_DIM = 64
FOX_SCALE = 1.0 / math.sqrt(FOX_DIM)
LOG2E = math.log2(math.e)

LANE = 128
HEAD_PAIRS = HEADS // 2
SMALL_W = 4 * LANE
KPE_LANE = 64
ROW_TILE = 512
ATT_TILE = 512
Q_SUB = 2
L_ROWS = 16
NEG_INF = -1e30
VMEM_LIMIT = 56 * 1024 * 1024

NT_DIMS = (((1,), (1,)), ((), ()))


def _rms(x, g):
    return x * lax.rsqrt(jnp.mean(x * x, axis=-1, keepdims=True) + RMS_EPS) * g


def _sigmoid(x):
    return 1.0 / (1.0 + jnp.exp(-x))


def _split3(x):
    hi = x.astype(BF16)
    r1 = x - hi.astype(F32)
    mid = r1.astype(BF16)
    lo = (r1 - mid.astype(F32)).astype(BF16)
    return hi, mid, lo


def _const_spec(shape):
    nd = len(shape)
    return pl.BlockSpec(shape, lambda *_: (0,) * nd, pipeline_mode=pl.Buffered(1))


def _in_proj_kernel(x_ref, g_ref, ws_ref, wzm_ref, wfq_ref, wfk_ref, wfvt_ref, wzf_ref, wga_ref, wgb_ref,
                    gq_ref, gkv_ref, wq_ref, wkm_ref, wvt_ref, cosq_ref, sinq_ref, cosk_ref, sink_ref,
                    fl_ref, szm_ref, fq_ref, fk_ref, fvt_ref, szf_ref, ga_ref, gb_ref, q_ref, k_ref, vt_ref):
    u = _rms(x_ref[...], g_ref[...]).astype(BF16)

    def mm(w_ref):
        return lax.dot_general(u, w_ref[...], NT_DIMS, preferred_element_type=F32)

    def store_pairs(out_ref, rows_t):
        for hp in range(HEAD_PAIRS):
            out_ref[hp] = rows_t[hp * LANE:(hp + 1) * LANE, :].astype(BF16)

    z = mm(wzm_ref)
    szm_ref[...] = (z * _sigmoid(z)).astype(BF16)
    fq_ref[...] = (mm(wfq_ref) * (FOX_SCALE * LOG2E)).astype(BF16)
    fk_ref[...] = mm(wfk_ref).astype(BF16)
    store_pairs(fvt_ref, lax.dot_general(wfvt_ref[...], u, NT_DIMS, preferred_element_type=F32))
    z = mm(wzf_ref)
    szf_ref[...] = (z * _sigmoid(z)).astype(BF16)
    ga_ref[...] = _sigmoid(mm(wga_ref)).astype(BF16)
    gb_ref[...] = _sigmoid(mm(wgb_ref)).astype(BF16)

    sm = mm(ws_ref)
    mixed = sm[:, 3 * LANE:4 * LANE]
    fl_ref[...] = mixed
    cn = _rms(sm[:, 0:MLA_Q_RANK], gq_ref[...]).astype(BF16)
    kn = _rms(sm[:, MLA_Q_RANK:MLA_Q_RANK + MLA_KV_RANK], gkv_ref[...]).astype(BF16)
    to_rope_lanes = LANE - MLA_ROPE
    k_rope = mixed * cosk_ref[...] + pltpu.roll(mixed, to_rope_lanes, axis=1) * sink_ref[...]
    qa = jnp.dot(cn, wq_ref[...], preferred_element_type=F32)
    km = jnp.dot(kn, wkm_ref[...], preferred_element_type=F32)
    cosq = cosq_ref[...]
    sinq = sinq_ref[...]
    for h in range(HEADS):
        sl = slice(h * LANE, (h + 1) * LANE)
        rotated = pltpu.roll(qa[:, sl], to_rope_lanes, axis=1)
        q_ref[:, sl] = (qa[:, sl] * cosq + rotated * sinq).astype(BF16)
        k_ref[:, sl] = (km[:, sl] + k_rope).astype(BF16)
    store_pairs(vt_ref, lax.dot_general(wvt_ref[...], kn, NT_DIMS, preferred_element_type=F32))


def _in_proj(x2d, g, weights, gq, gkv, mla_weights, tables, tm, tiles_per_seq, vt_shape, vt_spec):
    rows, d = x2d.shape
    width = weights[1].shape[0]
    row_spec = lambda w: pl.BlockSpec((tm, w), lambda i: (i, 0))
    tab_spec = pl.BlockSpec((tm, LANE), lambda i: (i % tiles_per_seq, 0))
    wide = HEADS * LANE
    outs = [((rows, LANE), F32, row_spec(LANE))]
    for name in ("szm", "fq", "fk", "fvt", "szf", "ga", "gb"):
        outs.append((vt_shape, BF16, vt_spec) if name == "fvt" else ((rows, width), BF16, row_spec(width)))
    outs += [((rows, wide), BF16, row_spec(wide)), ((rows, wide), BF16, row_spec(wide)), (vt_shape, BF16, vt_spec)]
    consts = [g] + list(weights) + [gq, gkv] + list(mla_weights)
    return pl.pallas_call(
        _in_proj_kernel,
        grid=(rows // tm,),
        in_specs=[row_spec(d)] + [_const_spec(c.shape) for c in consts] + [tab_spec] * 4,
        out_specs=[spec for _, _, spec in outs],
        out_shape=[jax.ShapeDtypeStruct(shape, dtype) for shape, dtype, _ in outs],
        compiler_params=pltpu.CompilerParams(dimension_semantics=("parallel",), vmem_limit_bytes=VMEM_LIMIT),
        name="in_proj",
    )(x2d, *consts, *tables)


def _log2_sigmoid(z):
    return (jnp.minimum(z, 0.0) - jnp.log1p(jnp.exp(-jnp.abs(z)))) * LOG2E


def _tri(n):
    row = lax.broadcasted_iota(jnp.int32, (n, n), 0)
    col = lax.broadcasted_iota(jnp.int32, (n, n), 1)
    return jnp.where(col <= row, 1.0, 0.0).astype(BF16)


def _cumsum_rows(tri, x):
    return sum(jnp.dot(tri, part, preferred_element_type=F32) for part in _split3(x))


def _place(x, place_ref):
    return sum(jnp.dot(part, place_ref[k], preferred_element_type=F32)
               for k, part in enumerate(_split3(x))).astype(BF16)


def _decay_kernel(sm_ref, bias_ref, place_ref, ex_ref, crow_ref, beta_ref, *, tile):
    n_tiles = sm_ref.shape[0] // tile
    tri = _tri(tile)
    eye = jnp.where(lax.broadcasted_iota(jnp.int32, (LANE, LANE), 0)
                    == lax.broadcasted_iota(jnp.int32, (LANE, LANE), 1), 1.0, 0.0).astype(BF16)
    carry = jnp.zeros((1, LANE), F32)
    withins = []
    for j in range(0, n_tiles, 2):
        pair = [_log2_sigmoid(sm_ref[(j + e) * tile:(j + e + 1) * tile, :] + bias_ref[...]) for e in range(2)]
        both = _cumsum_rows(tri, jnp.concatenate(pair, axis=1))
        withins += [both[:, 0:LANE], both[:, LANE:]]
    for j in range(n_tiles):
        rows = slice(j * tile, (j + 1) * tile)
        within = withins[j]
        ex_ref[rows, :] = _place(-within, place_ref)
        cum = within + carry
        cum_t = sum(lax.dot_general(eye, part, NT_DIMS, preferred_element_type=F32) for part in _split3(cum))
        crow_ref[:, j, :] = cum_t[0:HEADS, :]
        beta_ref[j:j + 1, :] = carry
        carry = cum[tile - 1:tile, :]


def _decay_meta_kernel(sm_ref, bias_ref, place_ref, ex_ref):
    n = sm_ref.shape[0]
    cum = _cumsum_rows(_tri(n), _log2_sigmoid(sm_ref[...] + bias_ref[...]))
    ex_ref[...] = _place(cum[n - 1:n, :] - cum, place_ref)


def _decay_prep(fl, fl_meta, bias_row, place, batch, seq):
    width = place.shape[2]
    ex, crow, beta = pl.pallas_call(
        functools.partial(_decay_kernel, tile=ATT_TILE),
        grid=(batch,),
        in_specs=[pl.BlockSpec((seq, LANE), lambda b: (b, 0)), _const_spec(bias_row.shape), _const_spec(place.shape)],
        out_specs=[pl.BlockSpec((seq, width), lambda b: (b, 0)),
                   pl.BlockSpec((None, HEADS, seq // ATT_TILE, ATT_TILE), lambda b: (b, 0, 0, 0)),
                   pl.BlockSpec((None, seq // ATT_TILE, LANE), lambda b: (b, 0, 0))],
        out_shape=[jax.ShapeDtypeStruct((batch * seq, width), BF16),
                   jax.ShapeDtypeStruct((batch, HEADS, seq // ATT_TILE, ATT_TILE), F32),
                   jax.ShapeDtypeStruct((batch, seq // ATT_TILE, LANE), F32)],
        compiler_params=pltpu.CompilerParams(dimension_semantics=("parallel",), vmem_limit_bytes=VMEM_LIMIT),
        name="decay_prep",
    )(fl, bias_row, place)
    ex_meta = pl.pallas_call(
        _decay_meta_kernel,
        grid=(1,),
        in_specs=[pl.BlockSpec((N_META, LANE), lambda i: (0, 0)), _const_spec(bias_row.shape),
                  _const_spec(place.shape)],
        out_specs=pl.BlockSpec((N_META, width), lambda i: (0, 0)),
        out_shape=jax.ShapeDtypeStruct((N_META, width), BF16),
        name="decay_prep_meta",
    )(fl_meta, bias_row, place)
    return ex, crow, beta, ex_meta


class _Chain(NamedTuple):
    q: jax.Array
    k_meta: jax.Array
    vt_meta: jax.Array
    k_tile: Callable
    vt_tile: Callable
    off_meta: Any
    off_tile: Any
    sub: int


def _scores(q_h, k_t):
    return lax.dot_general(k_t, q_h, NT_DIMS, preferred_element_type=F32)


def _col_max(s):
    return jnp.max(s, axis=0, keepdims=True)


def _pv(vt_t, p):
    v_aug = jnp.concatenate([vt_t, jnp.ones((L_ROWS, vt_t.shape[1]), BF16)], axis=0)
    return jnp.dot(v_aug, p, preferred_element_type=F32)


def _shifted(m_tile, off):
    return m_tile if off is None else m_tile + off


def _probs(s, m_new, off):
    return jnp.exp2(s - (m_new if off is None else m_new - off)).astype(BF16)


def _causal(shape, first_query):
    return (lax.broadcasted_iota(jnp.int32, shape, 0)
            <= lax.broadcasted_iota(jnp.int32, shape, 1) + first_query)


def _attend(step, chains, s_ref, p_ref, acc_ref, o_ref):
    n = len(chains)
    n_common = Q_SUB * step
    assert all(ch.sub in (0, 1) for ch in chains)

    def off_of(ch, t):
        return None if ch.off_tile is None else ch.off_tile(t)

    s_meta = [_scores(ch.q, ch.k_meta) for ch in chains]
    ms, alphas, tile_max = [], [], []
    for c, ch in enumerate(chains):
        s = _scores(ch.q, ch.k_tile(0))
        m = jnp.full((1, ATT_TILE), NEG_INF, F32)
        acc_ref[c] = jnp.zeros(acc_ref.shape[1:], F32)
        if ch.sub == 0:
            p_ref[c] = jnp.zeros(p_ref.shape[1:], BF16)
        else:
            m = _shifted(_col_max(s), off_of(ch, 0))
            p_ref[c] = _probs(s, m, off_of(ch, 0))
            s = _scores(ch.q, ch.k_tile(1))
        s_ref[c] = s
        tile_max.append(_col_max(s))
        ms.append(m)
        alphas.append(jnp.ones((1, ATT_TILE), F32))

    def pending_pv(c, ch, j, alpha):
        acc_ref[c] = alpha * acc_ref[c] + _pv(ch.vt_tile(jnp.maximum(j - 1 + ch.sub, 0)), p_ref[c])

    def body(j, carry):
        ms, alphas, tile_max = carry
        cur = [s_ref[c] for c in range(n)]
        for c, ch in enumerate(chains):
            pending_pv(c, ch, j, alphas[c])
        new_ms, new_alphas, probs = [], [], []
        for c, ch in enumerate(chains):
            off = off_of(ch, j + ch.sub)
            m_new = jnp.maximum(ms[c], _shifted(tile_max[c], off))
            new_alphas.append(jnp.exp2(ms[c] - m_new))
            probs.append(_probs(cur[c], m_new, off))
            new_ms.append(m_new)
        nxt = [_scores(ch.q, ch.k_tile(j + 1 + ch.sub)) for ch in chains]
        for c in range(n):
            p_ref[c] = probs[c]
            s_ref[c] = nxt[c]
        return tuple(new_ms), tuple(new_alphas), tuple(_col_max(s) for s in nxt)

    ms, alphas, _ = lax.fori_loop(0, n_common, body, (tuple(ms), tuple(alphas), tuple(tile_max)))

    outs = {}
    for c, ch in enumerate(chains):
        pending_pv(c, ch, n_common, alphas[c])
        t = n_common + ch.sub
        s = jnp.where(_causal((ATT_TILE, ATT_TILE), 0), s_ref[c], NEG_INF)
        off = off_of(ch, t)
        m_new = jnp.maximum(jnp.maximum(ms[c], _shifted(_col_max(s), off)),
                            _shifted(_col_max(s_meta[c]), ch.off_meta))
        acc = (jnp.exp2(ms[c] - m_new) * acc_ref[c] + _pv(ch.vt_tile(t), _probs(s, m_new, off))
               + _pv(ch.vt_meta, _probs(s_meta[c], m_new, ch.off_meta)))
        outs.setdefault(ch.sub, []).append(acc[0:FOX_DIM, :] / acc[FOX_DIM:FOX_DIM + 1, :])
    for u, out_t in outs.items():
        o_ref[_key_rows(Q_SUB * step + u), :] = jnp.concatenate(out_t, axis=0).T.astype(BF16)


def _key_rows(j):
    return pl.ds(pl.multiple_of(j * ATT_TILE, ATT_TILE), ATT_TILE)


def _query_steps(n_keys, one_step):
    def body(step, carry):
        one_step(step)
        return carry
    lax.fori_loop(0, n_keys // (Q_SUB * ATT_TILE), body, 0)


def _mla_attn_kernel(q_ref, k_ref, vt_ref, km_ref, vmt_ref, o_ref, *scratch):
    def one_step(step):
        chains = []
        for u in range(Q_SUB):
            for hh in range(2):
                lanes = slice(hh * LANE, (hh + 1) * LANE)
                rows64 = slice(hh * MLA_V, (hh + 1) * MLA_V)
                chains.append(_Chain(
                    q=q_ref[_key_rows(Q_SUB * step + u), lanes], k_meta=km_ref[:, lanes], vt_meta=vmt_ref[rows64, :],
                    k_tile=lambda j, lanes=lanes: k_ref[_key_rows(j), lanes],
                    vt_tile=lambda j, rows64=rows64: vt_ref[j, rows64, :],
                    off_meta=None, off_tile=None, sub=u))
        _attend(step, chains, *scratch, o_ref)

    _query_steps(k_ref.shape[0], one_step)


def _fox_attn_kernel(beta_ref, q_ref, k_ref, ex_ref, spread_ref, vt_ref, km_ref, exm_ref, vmt_ref, crow_ref, o_ref,
                     kaug_ref, *scratch):
    b = pl.program_id(0)
    hp = pl.program_id(1)
    n_tiles = k_ref.shape[0] // ATT_TILE
    lane = lax.broadcasted_iota(jnp.int32, (1, LANE), 1)
    in_a = lane < FOX_DIM

    def spread(packed):
        return jnp.dot(packed, spread_ref[...], preferred_element_type=F32).astype(BF16)

    for c in range(n_tiles):
        rows = slice(c * ATT_TILE, (c + 1) * ATT_TILE)
        kk = k_ref[rows, :]
        ee = spread(ex_ref[rows, :])
        kaug_ref[0, rows, :] = jnp.where(in_a, kk, ee)
        kaug_ref[1, rows, :] = jnp.where(in_a, ee, kk)

    km = km_ref[...]
    exm = spread(exm_ref[...])

    def one_step(step):
        chains = []
        for u in range(Q_SUB):
            tile = Q_SUB * step + u
            q = q_ref[_key_rows(tile), :]
            for hh in range(2):
                own = in_a if hh == 0 else jnp.logical_not(in_a)
                base = FOX_DIM if hh == 0 else 0
                ones = jnp.where((lane >= base) & (lane < base + 3), 1.0, 0.0).astype(BF16)
                rows64 = slice(hh * FOX_DIM, (hh + 1) * FOX_DIM)
                cum_q = crow_ref[2 * hp + hh, pl.ds(tile, 1), :]
                beta_base = b * n_tiles * LANE + 2 * hp + hh
                chains.append(_Chain(
                    q=jnp.where(own, q, ones), k_meta=jnp.where(own, km, exm), vt_meta=vmt_ref[rows64, :],
                    k_tile=lambda j, hh=hh: kaug_ref[hh, _key_rows(j), :],
                    vt_tile=lambda j, rows64=rows64: vt_ref[j, rows64, :],
                    off_meta=cum_q,
                    off_tile=lambda j, cum_q=cum_q, beta_base=beta_base: cum_q - beta_ref[beta_base + j * LANE],
                    sub=u))
        _attend(step, chains, *scratch, o_ref)

    _query_steps(k_ref.shape[0], one_step)


def _attention_specs(batch, seq, width):
    n_tiles = seq // ATT_TILE
    q_spec = pl.BlockSpec((None, seq, width), lambda b, hp, *_: (b, 0, hp))
    k_spec = pl.BlockSpec((None, seq, width), lambda b, hp, *_: (b, 0, hp))
    vt_spec = pl.BlockSpec((None, None, n_tiles, LANE, ATT_TILE), lambda b, hp, *_: (b, hp, 0, 0, 0))
    km_spec = pl.BlockSpec((N_META, width), lambda b, hp, *_: (0, hp))
    vmt_spec = pl.BlockSpec((None, LANE, N_META), lambda b, hp, *_: (hp, 0, 0))
    o_spec = pl.BlockSpec((None, seq, LANE), lambda b, hp, *_: (b, 0, hp))
    return q_spec, k_spec, vt_spec, km_spec, vmt_spec, o_spec


def _pair_scratch():
    n = 2 * Q_SUB
    return [pltpu.VMEM((n, ATT_TILE, ATT_TILE), F32), pltpu.VMEM((n, ATT_TILE, ATT_TILE), BF16),
            pltpu.VMEM((n, MLA_V + L_ROWS, ATT_TILE), F32)]


def _mla_attention(q, k, vt, km, vmt):
    batch, seq, _ = q.shape
    q_spec, k_spec, vt_spec, km_spec, vmt_spec, o_spec = _attention_specs(batch, seq, 2 * LANE)
    return pl.pallas_call(
        _mla_attn_kernel,
        grid=(batch, HEAD_PAIRS),
        in_specs=[q_spec, k_spec, vt_spec, km_spec, vmt_spec],
        out_specs=o_spec,
        out_shape=jax.ShapeDtypeStruct((batch, seq, HEADS * MLA_V), BF16),
        scratch_shapes=_pair_scratch(),
        compiler_params=pltpu.CompilerParams(dimension_semantics=("parallel", "parallel"),
                                             vmem_limit_bytes=VMEM_LIMIT),
        name="mla_attention",
    )(q, k, vt, km, vmt)


def _fox_attention(beta, q, k, ex, spread, vt, km, exm, vmt, crow):
    batch, seq, _ = q.shape
    q_spec, k_spec, vt_spec, km_spec, vmt_spec, o_spec = _attention_specs(batch, seq, LANE)
    crow_spec = pl.BlockSpec((None, HEADS, seq // ATT_TILE, ATT_TILE), lambda b, hp, *_: (b, 0, 0, 0))
    ex_spec = pl.BlockSpec((None, seq, LANE), lambda b, hp, *_: (b, 0, 0))
    exm_spec = pl.BlockSpec((N_META, LANE), lambda b, hp, *_: (0, 0))
    spread_spec = pl.BlockSpec((None, LANE, LANE), lambda b, hp, *_: (hp, 0, 0))
    grid_spec = pltpu.PrefetchScalarGridSpec(
        num_scalar_prefetch=1,
        grid=(batch, HEAD_PAIRS),
        in_specs=[q_spec, k_spec, ex_spec, spread_spec, vt_spec, km_spec, exm_spec, vmt_spec, crow_spec],
        out_specs=o_spec,
        scratch_shapes=[pltpu.VMEM((2, seq, LANE), BF16)] + _pair_scratch(),
    )
    return pl.pallas_call(
        _fox_attn_kernel,
        grid_spec=grid_spec,
        out_shape=jax.ShapeDtypeStruct((batch, seq, HEADS * FOX_DIM), BF16),
        compiler_params=pltpu.CompilerParams(dimension_semantics=("parallel", "parallel"),
                                             vmem_limit_bytes=VMEM_LIMIT),
        name="fox_attention",
    )(beta, q, k, ex, spread, vt, km, exm, vmt, crow)


def _merge_kernel(om_ref, szm_ref, of_ref, szf_ref, ga_ref, gb_ref, x_ref, wa_ref, wb_ref, wo_ref, g_ref, out_ref):
    ya = jnp.dot(om_ref[...] * szm_ref[...], wa_ref[...], preferred_element_type=F32)
    yb = jnp.dot(of_ref[...] * szf_ref[...], wb_ref[...], preferred_element_type=F32)
    mixed_in = (ga_ref[...].astype(F32) * ya + gb_ref[...].astype(F32) * yb).astype(BF16)
    mixed = jnp.dot(mixed_in, wo_ref[...], preferred_element_type=F32)
    out_ref[...] = x_ref[...] + _rms(mixed, g_ref[...])


def _merge(om, szm, of, szf, ga, gb, x2d, wa, wb, wo, g):
    rows, d = x2d.shape
    row_spec = pl.BlockSpec((ROW_TILE, d), lambda i: (i, 0))
    return pl.pallas_call(
        _merge_kernel,
        grid=(rows // ROW_TILE,),
        in_specs=[row_spec] * 7 + [_const_spec(w.shape) for w in (wa, wb, wo, g)],
        out_specs=row_spec,
        out_shape=jax.ShapeDtypeStruct((rows, d), F32),
        compiler_params=pltpu.CompilerParams(dimension_semantics=("parallel",), vmem_limit_bytes=VMEM_LIMIT),
        name="merge",
    )(om, szm, of, szf, ga, gb, x2d, wa, wb, wo, g)


def _transpose_kernel(w_ref, o_ref):
    o_ref[...] = w_ref[...].T.astype(BF16)


def _transpose_bf16(w):
    k, n = w.shape
    cols = 2 * LANE
    return pl.pallas_call(
        _transpose_kernel,
        grid=(n // cols,),
        in_specs=[pl.BlockSpec((k, cols), lambda i: (0, i))],
        out_specs=pl.BlockSpec((cols, k), lambda i: (i, 0)),
        out_shape=jax.ShapeDtypeStruct((n, k), BF16),
        name="weight_transpose",
    )(w)


def _rotate_half_cols(w):
    half = w.shape[-1] // 2
    return jnp.concatenate([-w[..., half:], w[..., :half]], axis=-1)


def _pad_cols(w, before, total):
    return jnp.pad(w, ((0, 0), (before, total - before - w.shape[1])))


def _in_proj_weights(w_in):
    widths = (MLA_Q_RANK, MLA_KV_RANK, MLA_ROPE, HEADS * MLA_V, HEADS * FOX_DIM, HEADS * FOX_DIM,
              HEADS * FOX_DIM, HEADS, HEADS * FOX_DIM, w_in.shape[0], w_in.shape[0])
    assert sum(widths) == w_in.shape[1]
    bounds = np.cumsum((0,) + widths)
    wt = w_in.T
    w_cq, w_ckv, w_kpe, w_zm, w_fq, w_fk, w_fv, w_fl, w_zf, w_ga, w_gb = (
        wt[bounds[n]:bounds[n + 1]] for n in range(len(widths)))
    w_small = jnp.concatenate([
        w_cq, w_ckv,
        _pad_cols(w_fl.T, 0, KPE_LANE).T, w_kpe, _rotate_half_cols(w_kpe.T).T], axis=0)
    assert w_small.shape[0] == SMALL_W
    return [w.astype(BF16) for w in (w_small, w_zm, w_fq, w_fk, w_fv, w_zf, w_ga, w_gb)]


def _head_blocks(w, per_head, first, last, at, width, fn=lambda blk: blk):
    blocks = [_pad_cols(fn(w[:, h * per_head + first:h * per_head + last]), at, width) for h in range(HEADS)]
    return jnp.concatenate(blocks, axis=1)


def _mla_weights(w_uq, w_ukv):
    dq = MLA_NOPE + MLA_ROPE
    dkv = MLA_NOPE + MLA_V
    assert dq + MLA_ROPE == LANE
    wq = jnp.concatenate([w for h in range(HEADS) for w in (
        w_uq[:, h * dq:(h + 1) * dq], _rotate_half_cols(w_uq[:, h * dq + MLA_NOPE:(h + 1) * dq]))], axis=1)
    wk_main = _head_blocks(w_ukv, dkv, 0, MLA_NOPE, 0, LANE)
    wv_t = _transpose_bf16(_head_blocks(w_ukv, dkv, MLA_NOPE, dkv, 0, MLA_V))
    return [w.astype(BF16) for w in (wq, wk_main)] + [wv_t]


def _rope_tables(first_pos, n_pos):
    half = MLA_ROPE // 2
    inv_freq = ROPE_THETA ** (-np.arange(half, dtype=np.float64) / half)
    ang = (first_pos + np.arange(n_pos, dtype=np.float64))[:, None] * inv_freq[None, :]
    cos = np.concatenate([np.cos(ang), np.cos(ang)], axis=1)
    sin = np.concatenate([np.sin(ang), np.sin(ang)], axis=1)
    pad = np.zeros((n_pos, LANE - MLA_NOPE - MLA_ROPE))
    ones = np.ones((n_pos, MLA_NOPE))
    zeros = np.zeros((n_pos, MLA_NOPE))
    cos_q = np.concatenate([ones, cos, pad], axis=1) * (MLA_SCALE * LOG2E)
    sin_q = np.concatenate([zeros, sin, pad], axis=1) * (MLA_SCALE * LOG2E)
    cos_k = np.concatenate([zeros, cos, pad], axis=1)
    sin_k = np.concatenate([zeros, sin, pad], axis=1)
    return [jnp.asarray(t, F32) for t in (cos_q, sin_q, cos_k, sin_k)]


def _placement():
    pack = np.zeros((3, LANE, LANE), np.float32)
    spread = np.zeros((HEAD_PAIRS, LANE, LANE), np.float32)
    for h in range(HEADS):
        base = FOX_DIM if h % 2 == 0 else 0
        for part in range(3):
            pack[part, h, HEADS * part + h] = 1.0
            spread[h // 2, HEADS * part + h, base + part] = 1.0
    return jnp.asarray(pack, BF16), jnp.asarray(spread, BF16)


def kernel(x, meta_tokens, pre_norm_g, w_in, fox_forget_b, mla_q_norm_g, mla_kv_norm_g, w_uq, w_ukv,
           w_br_mla, w_br_fox, w_out, post_norm_g):
    batch, seq, d = x.shape
    assert pre_norm_g.shape[0] == 1, "one layer supported"
    assert meta_tokens.shape[0] == N_META and seq % (Q_SUB * ATT_TILE) == 0 and ATT_TILE == ROW_TILE
    n_tiles = seq // ATT_TILE
    x2d = x.reshape(batch * seq, d)

    in_w = _in_proj_weights(w_in[0])
    mla_w = _mla_weights(w_uq[0], w_ukv[0])
    g_pre = pre_norm_g.astype(F32)
    gq = mla_q_norm_g.astype(F32)
    gkv = mla_kv_norm_g.astype(F32)
    bias_row = jnp.pad(fox_forget_b.astype(F32), ((0, 0), (0, LANE - HEADS)))

    vt_shape = (batch, HEAD_PAIRS, n_tiles, LANE, ATT_TILE)
    vt_spec = pl.BlockSpec((None, HEAD_PAIRS, None, LANE, ROW_TILE), lambda i: (i // n_tiles, 0, i % n_tiles, 0, 0))
    vt_meta_shape = (HEAD_PAIRS, LANE, N_META)
    vt_meta_spec = pl.BlockSpec(vt_meta_shape, lambda i: (0, 0, 0))

    fl, szm, fq, fk, fvt, szf, ga, gb, q, k, vt = _in_proj(
        x2d, g_pre, in_w, gq, gkv, mla_w, _rope_tables(N_META, seq), ROW_TILE, n_tiles, vt_shape, vt_spec)
    fl_m, _, _, fk_m, fvt_m, _, _, _, _, k_m, vt_m = _in_proj(
        meta_tokens.astype(F32), g_pre, in_w, gq, gkv, mla_w, _rope_tables(0, N_META), N_META, 1,
        vt_meta_shape, vt_meta_spec)
    pack, spread = _placement()
    ex, crow, beta, ex_m = _decay_prep(fl, fl_m, bias_row, pack, batch, seq)

    o_mla = _mla_attention(q.reshape(batch, seq, -1), k.reshape(batch, seq, -1), vt, k_m, vt_m)
    o_fox = _fox_attention(beta.reshape(-1),fq.reshape(batch, seq, -1), fk.reshape(batch, seq, -1),
                           ex.reshape(batch, seq, -1), spread, fvt, fk_m, ex_m, fvt_m,
                           crow)

    out = _merge(o_mla.reshape(batch * seq, -1), szm, o_fox.reshape(batch * seq, -1), szf, ga, gb, x2d,
                 w_br_mla[0].astype(BF16), w_br_fox[0].astype(BF16), w_out[0].astype(BF16),
                 post_norm_g.astype(F32))
    return out.reshape(batch, seq, d)
```

```python
import functools
import math
from typing import Any, Callable, NamedTuple

import numpy as np
import jax
import jax.numpy as jnp
from jax import lax
from jax.experimental import pallas as pl
from jax.experimental.pallas import tpu as pltpu

F32 = jnp.float32
BF16 = jnp.bfloat16

N_META = 16
RMS_EPS = 1e-6
HEADS = 16
MLA_Q_RANK = 256
MLA_KV_RANK = 128
MLA_NOPE = 64
MLA_ROPE = 32
MLA_V = 64
MLA_SCALE = 1.0 / math.sqrt(MLA_NOPE + MLA_ROPE)
ROPE_THETA = 10000.0
FOX_DIM = 64
FOX_SCALE = 1.0 / math.sqrt(FOX_DIM)
LOG2E = math.log2(math.e)

LANE = 128
HEAD_PAIRS = HEADS // 2
SMALL_W = 4 * LANE
KPE_LANE = 64
ROW_TILE = 512
ATT_TILE = 512
Q_SUB = 2
L_ROWS = 16
NEG_INF = -1e30
VMEM_LIMIT = 56 * 1024 * 1024

NT_DIMS = (((1,), (1,)), ((), ()))


def _rms(x, g):
    return x * lax.rsqrt(jnp.mean(x * x, axis=-1, keepdims=True) + RMS_EPS) * g


def _sigmoid(x):
    return 1.0 / (1.0 + jnp.exp(-x))


def _split3(x):
    hi = x.astype(BF16)
    r1 = x - hi.astype(F32)
    mid = r1.astype(BF16)
    lo = (r1 - mid.astype(F32)).astype(BF16)
    return hi, mid, lo


def _const_spec(shape):
    nd = len(shape)
    return pl.BlockSpec(shape, lambda *_: (0,) * nd, pipeline_mode=pl.Buffered(1))


def _in_proj_kernel(x_ref, g_ref, ws_ref, wzm_ref, wfq_ref, wfk_ref, wfvt_ref, wzf_ref, wga_ref, wgb_ref,
                    gq_ref, gkv_ref, wq_ref, wkm_ref, wvt_ref, cosq_ref, sinq_ref, cosk_ref, sink_ref,
                    fl_ref, szm_ref, fq_ref, fk_ref, fvt_ref, szf_ref, ga_ref, gb_ref, q_ref, k_ref, vt_ref):
    u = _rms(x_ref[...], g_ref[...]).astype(BF16)

    def mm(w_ref):
        return lax.dot_general(u, w_ref[...], NT_DIMS, preferred_element_type=F32)

    def store_pairs(out_ref, rows_t):
        for hp in range(HEAD_PAIRS):
            out_ref[hp] = rows_t[hp * LANE:(hp + 1) * LANE, :].astype(BF16)

    z = mm(wzm_ref)
    szm_ref[...] = (z * _sigmoid(z)).astype(BF16)
    fq_ref[...] = (mm(wfq_ref) * (FOX_SCALE * LOG2E)).astype(BF16)
    fk_ref[...] = mm(wfk_ref).astype(BF16)
    store_pairs(fvt_ref, lax.dot_general(wfvt_ref[...], u, NT_DIMS, preferred_element_type=F32))
    z = mm(wzf_ref)
    szf_ref[...] = (z * _sigmoid(z)).astype(BF16)
    ga_ref[...] = _sigmoid(mm(wga_ref)).astype(BF16)
    gb_ref[...] = _sigmoid(mm(wgb_ref)).astype(BF16)

    sm = mm(ws_ref)
    mixed = sm[:, 3 * LANE:4 * LANE]
    fl_ref[...] = mixed
    cn = _rms(sm[:, 0:MLA_Q_RANK], gq_ref[...]).astype(BF16)
    kn = _rms(sm[:, MLA_Q_RANK:MLA_Q_RANK + MLA_KV_RANK], gkv_ref[...]).astype(BF16)
    to_rope_lanes = LANE - MLA_ROPE
    k_rope = mixed * cosk_ref[...] + pltpu.roll(mixed, to_rope_lanes, axis=1) * sink_ref[...]
    qa = jnp.dot(cn, wq_ref[...], preferred_element_type=F32)
    km = jnp.dot(kn, wkm_ref[...], preferred_element_type=F32)
    cosq = cosq_ref[...]
    sinq = sinq_ref[...]
    for h in range(HEADS):
        sl = slice(h * LANE, (h + 1) * LANE)
        rotated = pltpu.roll(qa[:, sl], to_rope_lanes, axis=1)
        q_ref[:, sl] = (qa[:, sl] * cosq + rotated * sinq).astype(BF16)
        k_ref[:, sl] = (km[:, sl] + k_rope).astype(BF16)
    store_pairs(vt_ref, lax.dot_general(wvt_ref[...], kn, NT_DIMS, preferred_element_type=F32))


def _in_proj(x2d, g, weights, gq, gkv, mla_weights, tables, tm, tiles_per_seq, vt_shape, vt_spec):
    rows, d = x2d.shape
    width = weights[1].shape[0]
    row_spec = lambda w: pl.BlockSpec((tm, w), lambda i: (i, 0))
    tab_spec = pl.BlockSpec((tm, LANE), lambda i: (i % tiles_per_seq, 0))
    wide = HEADS * LANE
    outs = [((rows, LANE), F32, row_spec(LANE))]
    for name in ("szm", "fq", "fk", "fvt", "szf", "ga", "gb"):
        outs.append((vt_shape, BF16, vt_spec) if name == "fvt" else ((rows, width), BF16, row_spec(width)))
    outs += [((rows, wide), BF16, row_spec(wide)), ((rows, wide), BF16, row_spec(wide)), (vt_shape, BF16, vt_spec)]
    consts = [g] + list(weights) + [gq, gkv] + list(mla_weights)
    return pl.pallas_call(
        _in_proj_kernel,
        grid=(rows // tm,),
        in_specs=[row_spec(d)] + [_const_spec(c.shape) for c in consts] + [tab_spec] * 4,
        out_specs=[spec for _, _, spec in outs],
        out_shape=[jax.ShapeDtypeStruct(shape, dtype) for shape, dtype, _ in outs],
        compiler_params=pltpu.CompilerParams(dimension_semantics=("parallel",), vmem_limit_bytes=VMEM_LIMIT),
        name="in_proj",
    )(x2d, *consts, *tables)


def _log2_sigmoid(z):
    return (jnp.minimum(z, 0.0) - jnp.log1p(jnp.exp(-jnp.abs(z)))) * LOG2E


def _tri(n):
    row = lax.broadcasted_iota(jnp.int32, (n, n), 0)
    col = lax.broadcasted_iota(jnp.int32, (n, n), 1)
    return jnp.where(col <= row, 1.0, 0.0).astype(BF16)


def _cumsum_rows(tri, x):
    return sum(jnp.dot(tri, part, preferred_element_type=F32) for part in _split3(x))


def _place(x, place_ref):
    return sum(jnp.dot(part, place_ref[k], preferred_element_type=F32)
               for k, part in enumerate(_split3(x))).astype(BF16)


def _decay_kernel(sm_ref, bias_ref, place_ref, ex_ref, crow_ref, beta_ref, *, tile):
    n_tiles = sm_ref.shape[0] // tile
    tri = _tri(tile)
    eye = jnp.where(lax.broadcasted_iota(jnp.int32, (LANE, LANE), 0)
                    == lax.broadcasted_iota(jnp.int32, (LANE, LANE), 1), 1.0, 0.0).astype(BF16)
    carry = jnp.zeros((1, LANE), F32)
    withins = []
    for j in range(0, n_tiles, 2):
        pair = [_log2_sigmoid(sm_ref[(j + e) * tile:(j + e + 1) * tile, :] + bias_ref[...]) for e in range(2)]
        both = _cumsum_rows(tri, jnp.concatenate(pair, axis=1))
        withins += [both[:, 0:LANE], both[:, LANE:]]
    for j in range(n_tiles):
        rows = slice(j * tile, (j + 1) * tile)
        within = withins[j]
        ex_ref[rows, :] = _place(-within, place_ref)
        cum = within + carry
        cum_t = sum(lax.dot_general(eye, part, NT_DIMS, preferred_element_type=F32) for part in _split3(cum))
        crow_ref[:, j, :] = cum_t[0:HEADS, :]
        beta_ref[j:j + 1, :] = carry
        carry = cum[tile - 1:tile, :]


def _decay_meta_kernel(sm_ref, bias_ref, place_ref, ex_ref):
    n = sm_ref.shape[0]
    cum = _cumsum_rows(_tri(n), _log2_sigmoid(sm_ref[...] + bias_ref[...]))
    ex_ref[...] = _place(cum[n - 1:n, :] - cum, place_ref)


def _decay_prep(fl, fl_meta, bias_row, place, batch, seq):
    width = place.shape[2]
    ex, crow, beta = pl.pallas_call(
        functools.partial(_decay_kernel, tile=ATT_TILE),
        grid=(batch,),
        in_specs=[pl.BlockSpec((seq, LANE), lambda b: (b, 0)), _const_spec(bias_row.shape), _const_spec(place.shape)],
        out_specs=[pl.BlockSpec((seq, width), lambda b: (b, 0)),
                   pl.BlockSpec((None, HEADS, seq // ATT_TILE, ATT_TILE), lambda b: (b, 0, 0, 0)),
                   pl.BlockSpec((None, seq // ATT_TILE, LANE), lambda b: (b, 0, 0))],
        out_shape=[jax.ShapeDtypeStruct((batch * seq, width), BF16),
                   jax.ShapeDtypeStruct((batch, HEADS, seq // ATT_TILE, ATT_TILE), F32),
                   jax.ShapeDtypeStruct((batch, seq // ATT_TILE, LANE), F32)],
        compiler_params=pltpu.CompilerParams(dimension_semantics=("parallel",), vmem_limit_bytes=VMEM_LIMIT),
        name="decay_prep",
    )(fl, bias_row, place)
    ex_meta = pl.pallas_call(
        _decay_meta_kernel,
        grid=(1,),
        in_specs=[pl.BlockSpec((N_META, LANE), lambda i: (0, 0)), _const_spec(bias_row.shape),
                  _const_spec(place.shape)],
        out_specs=pl.BlockSpec((N_META, width), lambda i: (0, 0)),
        out_shape=jax.ShapeDtypeStruct((N_META, width), BF16),
        name="decay_prep_meta",
    )(fl_meta, bias_row, place)
    return ex, crow, beta, ex_meta


class _Chain(NamedTuple):
    q: jax.Array
    k_meta: jax.Array
    vt_meta: jax.Array
    k_tile: Callable
    vt_tile: Callable
    off_meta: Any
    off_tile: Any
    sub: int


def _scores(q_h, k_t):
    return lax.dot_general(k_t, q_h, NT_DIMS, preferred_element_type=F32)


def _col_max(s):
    return jnp.max(s, axis=0, keepdims=True)


def _pv(vt_t, p):
    v_aug = jnp.concatenate([vt_t, jnp.ones((L_ROWS, vt_t.shape[1]), BF16)], axis=0)
    return jnp.dot(v_aug, p, preferred_element_type=F32)


def _shifted(m_tile, off):
    return m_tile if off is None else m_tile + off


def _probs(s, m_new, off):
    return jnp.exp2(s - (m_new if off is None else m_new - off)).astype(BF16)


def _causal(shape, first_query):
    return (lax.broadcasted_iota(jnp.int32, shape, 0)
            <= lax.broadcasted_iota(jnp.int32, shape, 1) + first_query)


def _attend(step, chains, s_ref, p_ref, acc_ref, o_ref):
    n = len(chains)
    n_common = Q_SUB * step
    assert all(ch.sub in (0, 1) for ch in chains)

    def off_of(ch, t):
        return None if ch.off_tile is None else ch.off_tile(t)

    s_meta = [_scores(ch.q, ch.k_meta) for ch in chains]
    ms, alphas, tile_max = [], [], []
    for c, ch in enumerate(chains):
        s = _scores(ch.q, ch.k_tile(0))
        m = jnp.full((1, ATT_TILE), NEG_INF, F32)
        acc_ref[c] = jnp.zeros(acc_ref.shape[1:], F32)
        if ch.sub == 0:
            p_ref[c] = jnp.zeros(p_ref.shape[1:], BF16)
        else:
            m = _shifted(_col_max(s), off_of(ch, 0))
            p_ref[c] = _probs(s, m, off_of(ch, 0))
            s = _scores(ch.q, ch.k_tile(1))
        s_ref[c] = s
        tile_max.append(_col_max(s))
        ms.append(m)
        alphas.append(jnp.ones((1, ATT_TILE), F32))

    def pending_pv(c, ch, j, alpha):
        return alpha * acc_ref[c] + _pv(ch.vt_tile(jnp.maximum(j - 1 + ch.sub, 0)), p_ref[c])

    def body(j, carry):
        ms, alphas, tile_max = carry
        cur = [s_ref[c] for c in range(n)]
        for c, ch in enumerate(chains):
            acc_ref[c] = pending_pv(c, ch, j, alphas[c])
        new_ms, new_alphas, probs = [], [], []
        for c, ch in enumerate(chains):
            off = off_of(ch, j + ch.sub)
            m_new = jnp.maximum(ms[c], _shifted(tile_max[c], off))
            new_alphas.append(jnp.exp2(ms[c] - m_new))
            probs.append(_probs(cur[c], m_new, off))
            new_ms.append(m_new)
        nxt = [_scores(ch.q, ch.k_tile(j + 1 + ch.sub)) for ch in chains]
        for c in range(n):
            p_ref[c] = probs[c]
            s_ref[c] = nxt[c]
        return tuple(new_ms), tuple(new_alphas), tuple(_col_max(s) for s in nxt)

    ms, alphas, _ = lax.fori_loop(0, n_common, body, (tuple(ms), tuple(alphas), tuple(tile_max)))

    outs = {}
    for c, ch in enumerate(chains):
        acc = pending_pv(c, ch, n_common, alphas[c])
        t = n_common + ch.sub
        s = jnp.where(_causal((ATT_TILE, ATT_TILE), 0), s_ref[c], NEG_INF)
        off = off_of(ch, t)
        m_new = jnp.maximum(jnp.maximum(ms[c], _shifted(_col_max(s), off)),
                            _shifted(_col_max(s_meta[c]), ch.off_meta))
        acc = (jnp.exp2(ms[c] - m_new) * acc + _pv(ch.vt_tile(t), _probs(s, m_new, off))
               + _pv(ch.vt_meta, _probs(s_meta[c], m_new, ch.off_meta)))
        outs.setdefault(ch.sub, []).append(acc[0:FOX_DIM, :] / acc[FOX_DIM:FOX_DIM + 1, :])
    for u, out_t in outs.items():
        o_ref[_key_rows(Q_SUB * step + u), :] = jnp.concatenate(out_t, axis=0).T.astype(BF16)


def _key_rows(j):
    return pl.ds(pl.multiple_of(j * ATT_TILE, ATT_TILE), ATT_TILE)


def _query_steps(n_keys, one_step):
    def body(step, carry):
        one_step(step)
        return carry
    lax.fori_loop(0, n_keys // (Q_SUB * ATT_TILE), body, 0)


def _mla_attn_kernel(q_ref, k_ref, vt_ref, km_ref, vmt_ref, o_ref, *scratch):
    def one_step(step):
        chains = []
        for u in range(Q_SUB):
            for hh in range(2):
                lanes = slice(hh * LANE, (hh + 1) * LANE)
                rows64 = slice(hh * MLA_V, (hh + 1) * MLA_V)
                chains.append(_Chain(
                    q=q_ref[_key_rows(Q_SUB * step + u), lanes], k_meta=km_ref[:, lanes], vt_meta=vmt_ref[rows64, :],
                    k_tile=lambda j, lanes=lanes: k_ref[_key_rows(j), lanes],
                    vt_tile=lambda j, rows64=rows64: vt_ref[j, rows64, :],
                    off_meta=None, off_tile=None, sub=u))
        _attend(step, chains, *scratch, o_ref)

    _query_steps(k_ref.shape[0], one_step)


def _fox_attn_kernel(beta_ref, q_ref, k_ref, ex_ref, spread_ref, vt_ref, km_ref, exm_ref, vmt_ref, crow_ref, o_ref,
                     kaug_ref, *scratch):
    b = pl.program_id(0)
    hp = pl.program_id(1)
    n_tiles = k_ref.shape[0] // ATT_TILE
    lane = lax.broadcasted_iota(jnp.int32, (1, LANE), 1)
    in_a = lane < FOX_DIM

    def spread(packed):
        return jnp.dot(packed, spread_ref[...], preferred_element_type=F32).astype(BF16)

    for c in range(n_tiles):
        rows = slice(c * ATT_TILE, (c + 1) * ATT_TILE)
        kk = k_ref[rows, :]
        ee = spread(ex_ref[rows, :])
        kaug_ref[0, rows, :] = jnp.where(in_a, kk, ee)
        kaug_ref[1, rows, :] = jnp.where(in_a, ee, kk)

    km = km_ref[...]
    exm = spread(exm_ref[...])

    def one_step(step):
        chains = []
        for u in range(Q_SUB):
            tile = Q_SUB * step + u
            q = q_ref[_key_rows(tile), :]
            for hh in range(2):
                own = in_a if hh == 0 else jnp.logical_not(in_a)
                base = FOX_DIM if hh == 0 else 0
                ones = jnp.where((lane >= base) & (lane < base + 3), 1.0, 0.0).astype(BF16)
                rows64 = slice(hh * FOX_DIM, (hh + 1) * FOX_DIM)
                cum_q = crow_ref[2 * hp + hh, pl.ds(tile, 1), :]
                beta_base = b * n_tiles * LANE + 2 * hp + hh
                chains.append(_Chain(
                    q=jnp.where(own, q, ones), k_meta=jnp.where(own, km, exm), vt_meta=vmt_ref[rows64, :],
                    k_tile=lambda j, hh=hh: kaug_ref[hh, _key_rows(j), :],
                    vt_tile=lambda j, rows64=rows64: vt_ref[j, rows64, :],
                    off_meta=cum_q,
                    off_tile=lambda j, cum_q=cum_q, beta_base=beta_base: cum_q - beta_ref[beta_base + j * LANE],
                    sub=u))
        _attend(step, chains, *scratch, o_ref)

    _query_steps(k_ref.shape[0], one_step)


def _attention_specs(batch, seq, width):
    n_tiles = seq // ATT_TILE
    q_spec = pl.BlockSpec((None, seq, width), lambda b, hp, *_: (b, 0, hp))
    k_spec = pl.BlockSpec((None, seq, width), lambda b, hp, *_: (b, 0, hp))
    vt_spec = pl.BlockSpec((None, None, n_tiles, LANE, ATT_TILE), lambda b, hp, *_: (b, hp, 0, 0, 0))
    km_spec = pl.BlockSpec((N_META, width), lambda b, hp, *_: (0, hp))
    vmt_spec = pl.BlockSpec((None, LANE, N_META), lambda b, hp, *_: (hp, 0, 0))
    o_spec = pl.BlockSpec((None, seq, LANE), lambda b, hp, *_: (b, 0, hp))
    return q_spec, k_spec, vt_spec, km_spec, vmt_spec, o_spec


def _pair_scratch():
    n = 2 * Q_SUB
    return [pltpu.VMEM((n, ATT_TILE, ATT_TILE), F32), pltpu.VMEM((n, ATT_TILE, ATT_TILE), BF16),
            pltpu.VMEM((n, MLA_V + L_ROWS, ATT_TILE), F32)]


def _mla_attention(q, k, vt, km, vmt):
    batch, seq, _ = q.shape
    q_spec, k_spec, vt_spec, km_spec, vmt_spec, o_spec = _attention_specs(batch, seq, 2 * LANE)
    return pl.pallas_call(
        _mla_attn_kernel,
        grid=(batch, HEAD_PAIRS),
        in_specs=[q_spec, k_spec, vt_spec, km_spec, vmt_spec],
        out_specs=o_spec,
        out_shape=jax.ShapeDtypeStruct((batch, seq, HEADS * MLA_V), BF16),
        scratch_shapes=_pair_scratch(),
        compiler_params=pltpu.CompilerParams(dimension_semantics=("parallel", "parallel"),
                                             vmem_limit_bytes=VMEM_LIMIT),
        name="mla_attention",
    )(q, k, vt, km, vmt)


def _fox_attention(beta, q, k, ex, spread, vt, km, exm, vmt, crow):
    batch, seq, _ = q.shape
    q_spec, k_spec, vt_spec, km_spec, vmt_spec, o_spec = _attention_specs(batch, seq, LANE)
    crow_spec = pl.BlockSpec((None, HEADS, seq // ATT_TILE, ATT_TILE), lambda b, hp, *_: (b, 0, 0, 0))
    ex_spec = pl.BlockSpec((None, seq, LANE), lambda b, hp, *_: (b, 0, 0))
    exm_spec = pl.BlockSpec((N_META, LANE), lambda b, hp, *_: (0, 0))
    spread_spec = pl.BlockSpec((None, LANE, LANE), lambda b, hp, *_: (hp, 0, 0))
    grid_spec = pltpu.PrefetchScalarGridSpec(
        num_scalar_prefetch=1,
        grid=(batch, HEAD_PAIRS),
        in_specs=[q_spec, k_spec, ex_spec, spread_spec, vt_spec, km_spec, exm_spec, vmt_spec, crow_spec],
        out_specs=o_spec,
        scratch_shapes=[pltpu.VMEM((2, seq, LANE), BF16)] + _pair_scratch(),
    )
    return pl.pallas_call(
        _fox_attn_kernel,
        grid_spec=grid_spec,
        out_shape=jax.ShapeDtypeStruct((batch, seq, HEADS * FOX_DIM), BF16),
        compiler_params=pltpu.CompilerParams(dimension_semantics=("parallel", "parallel"),
                                             vmem_limit_bytes=VMEM_LIMIT),
        name="fox_attention",
    )(beta, q, k, ex, spread, vt, km, exm, vmt, crow)


def _merge_kernel(om_ref, szm_ref, of_ref, szf_ref, ga_ref, gb_ref, x_ref, wa_ref, wb_ref, wo_ref, g_ref, out_ref):
    ya = jnp.dot(om_ref[...] * szm_ref[...], wa_ref[...], preferred_element_type=F32)
    yb = jnp.dot(of_ref[...] * szf_ref[...], wb_ref[...], preferred_element_type=F32)
    mixed_in = (ga_ref[...].astype(F32) * ya + gb_ref[...].astype(F32) * yb).astype(BF16)
    mixed = jnp.dot(mixed_in, wo_ref[...], preferred_element_type=F32)
    out_ref[...] = x_ref[...] + _rms(mixed, g_ref[...])


def _merge(om, szm, of, szf, ga, gb, x2d, wa, wb, wo, g):
    rows, d = x2d.shape
    row_spec = pl.BlockSpec((ROW_TILE, d), lambda i: (i, 0))
    return pl.pallas_call(
        _merge_kernel,
        grid=(rows // ROW_TILE,),
        in_specs=[row_spec] * 7 + [_const_spec(w.shape) for w in (wa, wb, wo, g)],
        out_specs=row_spec,
        out_shape=jax.ShapeDtypeStruct((rows, d), F32),
        compiler_params=pltpu.CompilerParams(dimension_semantics=("parallel",), vmem_limit_bytes=VMEM_LIMIT),
        name="merge",
    )(om, szm, of, szf, ga, gb, x2d, wa, wb, wo, g)


def _transpose_kernel(w_ref, o_ref):
    o_ref[...] = w_ref[...].T.astype(BF16)


def _transpose_bf16(w):
    k, n = w.shape
    cols = 2 * LANE
    return pl.pallas_call(
        _transpose_kernel,
        grid=(n // cols,),
        in_specs=[pl.BlockSpec((k, cols), lambda i: (0, i))],
        out_specs=pl.BlockSpec((cols, k), lambda i: (i, 0)),
        out_shape=jax.ShapeDtypeStruct((n, k), BF16),
        name="weight_transpose",
    )(w)


def _rotate_half_cols(w):
    half = w.shape[-1] // 2
    return jnp.concatenate([-w[..., half:], w[..., :half]], axis=-1)


def _pad_cols(w, before, total):
    return jnp.pad(w, ((0, 0), (before, total - before - w.shape[1])))


def _in_proj_weights(w_in):
    widths = (MLA_Q_RANK, MLA_KV_RANK, MLA_ROPE, HEADS * MLA_V, HEADS * FOX_DIM, HEADS * FOX_DIM,
              HEADS * FOX_DIM, HEADS, HEADS * FOX_DIM, w_in.shape[0], w_in.shape[0])
    assert sum(widths) == w_in.shape[1]
    bounds = np.cumsum((0,) + widths)
    wt = w_in.T
    w_cq, w_ckv, w_kpe, w_zm, w_fq, w_fk, w_fv, w_fl, w_zf, w_ga, w_gb = (
        wt[bounds[n]:bounds[n + 1]] for n in range(len(widths)))
    w_small = jnp.concatenate([
        w_cq, w_ckv,
        _pad_cols(w_fl.T, 0, KPE_LANE).T, w_kpe, _rotate_half_cols(w_kpe.T).T], axis=0)
    assert w_small.shape[0] == SMALL_W
    return [w.astype(BF16) for w in (w_small, w_zm, w_fq, w_fk, w_fv, w_zf, w_ga, w_gb)]


def _head_blocks(w, per_head, first, last, at, width, fn=lambda blk: blk):
    blocks = [_pad_cols(fn(w[:, h * per_head + first:h * per_head + last]), at, width) for h in range(HEADS)]
    return jnp.concatenate(blocks, axis=1)


def _mla_weights(w_uq, w_ukv):
    dq = MLA_NOPE + MLA_ROPE
    dkv = MLA_NOPE + MLA_V
    assert dq + MLA_ROPE == LANE
    wq = jnp.concatenate([w for h in range(HEADS) for w in (
        w_uq[:, h * dq:(h + 1) * dq], _rotate_half_cols(w_uq[:, h * dq + MLA_NOPE:(h + 1) * dq]))], axis=1)
    wk_main = _head_blocks(w_ukv, dkv, 0, MLA_NOPE, 0, LANE)
    wv_t = _transpose_bf16(_head_blocks(w_ukv, dkv, MLA_NOPE, dkv, 0, MLA_V))
    return [w.astype(BF16) for w in (wq, wk_main)] + [wv_t]


def _rope_tables(first_pos, n_pos):
    half = MLA_ROPE // 2
    inv_freq = ROPE_THETA ** (-np.arange(half, dtype=np.float64) / half)
    ang = (first_pos + np.arange(n_pos, dtype=np.float64))[:, None] * inv_freq[None, :]
    cos = np.concatenate([np.cos(ang), np.cos(ang)], axis=1)
    sin = np.concatenate([np.sin(ang), np.sin(ang)], axis=1)
    pad = np.zeros((n_pos, LANE - MLA_NOPE - MLA_ROPE))
    ones = np.ones((n_pos, MLA_NOPE))
    zeros = np.zeros((n_pos, MLA_NOPE))
    cos_q = np.concatenate([ones, cos, pad], axis=1) * (MLA_SCALE * LOG2E)
    sin_q = np.concatenate([zeros, sin, pad], axis=1) * (MLA_SCALE * LOG2E)
    cos_k = np.concatenate([zeros, cos, pad], axis=1)
    sin_k = np.concatenate([zeros, sin, pad], axis=1)
    return [jnp.asarray(t, F32) for t in (cos_q, sin_q, cos_k, sin_k)]


def _placement():
    pack = np.zeros((3, LANE, LANE), np.float32)
    spread = np.zeros((HEAD_PAIRS, LANE, LANE), np.float32)
    for h in range(HEADS):
        base = FOX_DIM if h % 2 == 0 else 0
        for part in range(3):
            pack[part, h, HEADS * part + h] = 1.0
            spread[h // 2, HEADS * part + h, base + part] = 1.0
    return jnp.asarray(pack, BF16), jnp.asarray(spread, BF16)


def kernel(x, meta_tokens, pre_norm_g, w_in, fox_forget_b, mla_q_norm_g, mla_kv_norm_g, w_uq, w_ukv,
           w_br_mla, w_br_fox, w_out, post_norm_g):
    batch, seq, d = x.shape
    assert pre_norm_g.shape[0] == 1, "one layer supported"
    assert meta_tokens.shape[0] == N_META and seq % (Q_SUB * ATT_TILE) == 0 and ATT_TILE == ROW_TILE
    n_tiles = seq // ATT_TILE
    x2d = x.reshape(batch * seq, d)

    in_w = _in_proj_weights(w_in[0])
    mla_w = _mla_weights(w_uq[0], w_ukv[0])
    g_pre = pre_norm_g.astype(F32)
    gq = mla_q_norm_g.astype(F32)
    gkv = mla_kv_norm_g.astype(F32)
    bias_row = jnp.pad(fox_forget_b.astype(F32), ((0, 0), (0, LANE - HEADS)))

    vt_shape = (batch, HEAD_PAIRS, n_tiles, LANE, ATT_TILE)
    vt_spec = pl.BlockSpec((None, HEAD_PAIRS, None, LANE, ROW_TILE), lambda i: (i // n_tiles, 0, i % n_tiles, 0, 0))
    vt_meta_shape = (HEAD_PAIRS, LANE, N_META)
    vt_meta_spec = pl.BlockSpec(vt_meta_shape, lambda i: (0, 0, 0))

    fl, szm, fq, fk, fvt, szf, ga, gb, q, k, vt = _in_proj(
        x2d, g_pre, in_w, gq, gkv, mla_w, _rope_tables(N_META, seq), ROW_TILE, n_tiles, vt_shape, vt_spec)
    fl_m, _, _, fk_m, fvt_m, _, _, _, _, k_m, vt_m = _in_proj(
        meta_tokens.astype(F32), g_pre, in_w, gq, gkv, mla_w, _rope_tables(0, N_META), N_META, 1,
        vt_meta_shape, vt_meta_spec)
    pack, spread = _placement()
    ex, crow, beta, ex_m = _decay_prep(fl, fl_m, bias_row, pack, batch, seq)

    o_mla = _mla_attention(q.reshape(batch, seq, -1), k.reshape(batch, seq, -1), vt, k_m, vt_m)
    o_fox = _fox_attention(beta.reshape(-1),fq.reshape(batch, seq, -1), fk.reshape(batch, seq, -1),
                           ex.reshape(batch, seq, -1), spread, fvt, fk_m, ex_m, fvt_m,
                           crow)

    out = _merge(o_mla.reshape(batch * seq, -1), szm, o_fox.reshape(batch * seq, -1), szf, ga, gb, x2d,
                 w_br_mla[0].astype(BF16), w_br_fox[0].astype(BF16), w_out[0].astype(BF16),
                 post_norm_g.astype(F32))
    return out.reshape(batch, seq, d)
```

```python
import functools
import math
from typing import Any, Callable, NamedTuple

import numpy as np
import jax
import jax.numpy as jnp
from jax import lax
from jax.experimental import pallas as pl
from jax.experimental.pallas import tpu as pltpu

F32 = jnp.float32
BF16 = jnp.bfloat16

N_META = 16
RMS_EPS = 1e-6
HEADS = 16
MLA_Q_RANK = 256
MLA_KV_RANK = 128
MLA_NOPE = 64
MLA_ROPE = 32
MLA_V = 64
MLA_SCALE = 1.0 / math.sqrt(MLA_NOPE + MLA_ROPE)
ROPE_THETA = 10000.0
FOX_DIM = 64
FOX_SCALE = 1.0 / math.sqrt(FOX_DIM)
LOG2E = math.log2(math.e)

LANE = 128
HEAD_PAIRS = HEADS // 2
SMALL_W = 4 * LANE
KPE_LANE = 64
ROW_TILE = 512
ATT_TILE = 512
Q_SUB = 2
L_ROWS = 16
NEG_INF = -1e30
VMEM_LIMIT = 56 * 1024 * 1024

NT_DIMS = (((1,), (1,)), ((), ()))


def _rms(x, g):
    return x * lax.rsqrt(jnp.mean(x * x, axis=-1, keepdims=True) + RMS_EPS) * g


def _sigmoid(x):
    return 1.0 / (1.0 + jnp.exp(-x))


def _split3(x):
    hi = x.astype(BF16)
    r1 = x - hi.astype(F32)
    mid = r1.astype(BF16)
    lo = (r1 - mid.astype(F32)).astype(BF16)
    return hi, mid, lo


def _const_spec(shape):
    nd = len(shape)
    return pl.BlockSpec(shape, lambda *_: (0,) * nd, pipeline_mode=pl.Buffered(1))


def _in_proj_kernel(x_ref, g_ref, ws_ref, wzm_ref, wfq_ref, wfk_ref, wfvt_ref, wzf_ref, wga_ref, wgb_ref,
                    gq_ref, gkv_ref, wq_ref, wkm_ref, wvt_ref, cosq_ref, sinq_ref, cosk_ref, sink_ref,
                    fl_ref, szm_ref, fq_ref, fk_ref, fvt_ref, szf_ref, ga_ref, gb_ref, q_ref, k_ref, vt_ref):
    u = _rms(x_ref[...], g_ref[...]).astype(BF16)

    def mm(w_ref):
        return lax.dot_general(u, w_ref[...], NT_DIMS, preferred_element_type=F32)

    def store_pairs(out_ref, rows_t):
        for hp in range(HEAD_PAIRS):
            out_ref[hp] = rows_t[hp * LANE:(hp + 1) * LANE, :].astype(BF16)

    z = mm(wzm_ref)
    szm_ref[...] = (z * _sigmoid(z)).astype(BF16)
    fq_ref[...] = (mm(wfq_ref) * (FOX_SCALE * LOG2E)).astype(BF16)
    fk_ref[...] = mm(wfk_ref).astype(BF16)
    store_pairs(fvt_ref, lax.dot_general(wfvt_ref[...], u, NT_DIMS, preferred_element_type=F32))
    z = mm(wzf_ref)
    szf_ref[...] = (z * _sigmoid(z)).astype(BF16)
    ga_ref[...] = _sigmoid(mm(wga_ref)).astype(BF16)
    gb_ref[...] = _sigmoid(mm(wgb_ref)).astype(BF16)

    sm = mm(ws_ref)
    mixed = sm[:, 3 * LANE:4 * LANE]
    fl_ref[...] = mixed
    cn = _rms(sm[:, 0:MLA_Q_RANK], gq_ref[...]).astype(BF16)
    kn = _rms(sm[:, MLA_Q_RANK:MLA_Q_RANK + MLA_KV_RANK], gkv_ref[...]).astype(BF16)
    to_rope_lanes = LANE - MLA_ROPE
    k_rope = mixed * cosk_ref[...] + pltpu.roll(mixed, to_rope_lanes, axis=1) * sink_ref[...]
    qa = jnp.dot(cn, wq_ref[...], preferred_element_type=F32)
    km = jnp.dot(kn, wkm_ref[...], preferred_element_type=F32)
    cosq = cosq_ref[...]
    sinq = sinq_ref[...]
    for h in range(HEADS):
        sl = slice(h * LANE, (h + 1) * LANE)
        rotated = pltpu.roll(qa[:, sl], to_rope_lanes, axis=1)
        q_ref[:, sl] = (qa[:, sl] * cosq + rotated * sinq).astype(BF16)
        k_ref[:, sl] = (km[:, sl] + k_rope).astype(BF16)
    store_pairs(vt_ref, lax.dot_general(wvt_ref[...], kn, NT_DIMS, preferred_element_type=F32))


def _in_proj(x2d, g, weights, gq, gkv, mla_weights, tables, tm, tiles_per_seq, vt_shape, vt_spec):
    rows, d = x2d.shape
    width = weights[1].shape[0]
    row_spec = lambda w: pl.BlockSpec((tm, w), lambda i: (i, 0))
    tab_spec = pl.BlockSpec((tm, LANE), lambda i: (i % tiles_per_seq, 0))
    wide = HEADS * LANE
    outs = [((rows, LANE), F32, row_spec(LANE))]
    for name in ("szm", "fq", "fk", "fvt", "szf", "ga", "gb"):
        outs.append((vt_shape, BF16, vt_spec) if name == "fvt" else ((rows, width), BF16, row_spec(width)))
    outs += [((rows, wide), BF16, row_spec(wide)), ((rows, wide), BF16, row_spec(wide)), (vt_shape, BF16, vt_spec)]
    consts = [g] + list(weights) + [gq, gkv] + list(mla_weights)
    return pl.pallas_call(
        _in_proj_kernel,
        grid=(rows // tm,),
        in_specs=[row_spec(d)] + [_const_spec(c.shape) for c in consts] + [tab_spec] * 4,
        out_specs=[spec for _, _, spec in outs],
        out_shape=[jax.ShapeDtypeStruct(shape, dtype) for shape, dtype, _ in outs],
        compiler_params=pltpu.CompilerParams(dimension_semantics=("parallel",), vmem_limit_bytes=VMEM_LIMIT),
        name="in_proj",
    )(x2d, *consts, *tables)


def _log2_sigmoid(z):
    return (jnp.minimum(z, 0.0) - jnp.log1p(jnp.exp(-jnp.abs(z)))) * LOG2E


def _tri(n):
    row = lax.broadcasted_iota(jnp.int32, (n, n), 0)
    col = lax.broadcasted_iota(jnp.int32, (n, n), 1)
    return jnp.where(col <= row, 1.0, 0.0).astype(BF16)


def _cumsum_rows(tri, x):
    return sum(jnp.dot(tri, part, preferred_element_type=F32) for part in _split3(x))


def _place(x, place_ref):
    return sum(jnp.dot(part, place_ref[k], preferred_element_type=F32)
               for k, part in enumerate(_split3(x))).astype(BF16)


def _decay_kernel(sm_ref, bias_ref, place_ref, ex_ref, crow_ref, beta_ref, *, tile):
    n_tiles = sm_ref.shape[0] // tile
    tri = _tri(tile)
    eye = jnp.where(lax.broadcasted_iota(jnp.int32, (LANE, LANE), 0)
                    == lax.broadcasted_iota(jnp.int32, (LANE, LANE), 1), 1.0, 0.0).astype(BF16)
    carry = jnp.zeros((1, LANE), F32)
    withins = []
    for j in range(0, n_tiles, 2):
        pair = [_log2_sigmoid(sm_ref[(j + e) * tile:(j + e + 1) * tile, :] + bias_ref[...]) for e in range(2)]
        both = _cumsum_rows(tri, jnp.concatenate(pair, axis=1))
        withins += [both[:, 0:LANE], both[:, LANE:]]
    for j in range(n_tiles):
        rows = slice(j * tile, (j + 1) * tile)
        within = withins[j]
        ex_ref[rows, :] = _place(-within, place_ref)
        cum = within + carry
        cum_t = sum(lax.dot_general(eye, part, NT_DIMS, preferred_element_type=F32) for part in _split3(cum))
        crow_ref[:, j, :] = cum_t[0:HEADS, :]
        beta_ref[j:j + 1, :] = carry
        carry = cum[tile - 1:tile, :]


def _decay_meta_kernel(sm_ref, bias_ref, place_ref, ex_ref):
    n = sm_ref.shape[0]
    cum = _cumsum_rows(_tri(n), _log2_sigmoid(sm_ref[...] + bias_ref[...]))
    ex_ref[...] = _place(cum[n - 1:n, :] - cum, place_ref)


def _decay_prep(fl, fl_meta, bias_row, place, batch, seq):
    width = place.shape[2]
    ex, crow, beta = pl.pallas_call(
        functools.partial(_decay_kernel, tile=ATT_TILE),
        grid=(batch,),
        in_specs=[pl.BlockSpec((seq, LANE), lambda b: (b, 0)), _const_spec(bias_row.shape), _const_spec(place.shape)],
        out_specs=[pl.BlockSpec((seq, width), lambda b: (b, 0)),
                   pl.BlockSpec((None, HEADS, seq // ATT_TILE, ATT_TILE), lambda b: (b, 0, 0, 0)),
                   pl.BlockSpec((None, seq // ATT_TILE, LANE), lambda b: (b, 0, 0))],
        out_shape=[jax.ShapeDtypeStruct((batch * seq, width), BF16),
                   jax.ShapeDtypeStruct((batch, HEADS, seq // ATT_TILE, ATT_TILE), F32),
                   jax.ShapeDtypeStruct((batch, seq // ATT_TILE, LANE), F32)],
        compiler_params=pltpu.CompilerParams(dimension_semantics=("parallel",), vmem_limit_bytes=VMEM_LIMIT),
        name="decay_prep",
    )(fl, bias_row, place)
    ex_meta = pl.pallas_call(
        _decay_meta_kernel,
        grid=(1,),
        in_specs=[pl.BlockSpec((N_META, LANE), lambda i: (0, 0)), _const_spec(bias_row.shape),
                  _const_spec(place.shape)],
        out_specs=pl.BlockSpec((N_META, width), lambda i: (0, 0)),
        out_shape=jax.ShapeDtypeStruct((N_META, width), BF16),
        name="decay_prep_meta",
    )(fl_meta, bias_row, place)
    return ex, crow, beta, ex_meta


class _Chain(NamedTuple):
    q: jax.Array
    k_meta: jax.Array
    vt_meta: jax.Array
    k_tile: Callable
    vt_tile: Callable
    off_meta: Any
    off_tile: Any
    sub: int


def _scores(q_h, k_t):
    return lax.dot_general(k_t, q_h, NT_DIMS, preferred_element_type=F32)


def _col_max(s):
    return jnp.max(s, axis=0, keepdims=True)


def _pv(vt_t, p):
    v_aug = jnp.concatenate([vt_t, jnp.ones((L_ROWS, vt_t.shape[1]), BF16)], axis=0)
    return jnp.dot(v_aug, p, preferred_element_type=F32)


def _shifted(m_tile, off):
    return m_tile if off is None else m_tile + off


def _probs(s, m_new, off):
    return jnp.exp2(s - (m_new if off is None else m_new - off)).astype(BF16)


def _causal(shape, first_query):
    return (lax.broadcasted_iota(jnp.int32, shape, 0)
            <= lax.broadcasted_iota(jnp.int32, shape, 1) + first_query)


def _attend(step, chains, s_ref, p_ref, acc_ref, o_ref):
    n = len(chains)
    n_common = Q_SUB * step
    assert all(ch.sub in (0, 1) for ch in chains)

    def off_of(ch, t):
        return None if ch.off_tile is None else ch.off_tile(t)

    s_meta, ms, alphas, tile_max = [], [], [], []
    for c, ch in enumerate(chains):
        s = _scores(ch.q, jnp.concatenate([ch.k_tile(0), ch.k_meta], axis=0))
        s_meta.append(s[ATT_TILE:, :])
        s = s[0:ATT_TILE, :]
        m = jnp.full((1, ATT_TILE), NEG_INF, F32)
        acc_ref[c] = jnp.zeros(acc_ref.shape[1:], F32)
        if ch.sub == 0:
            p_ref[c] = jnp.zeros(p_ref.shape[1:], BF16)
        else:
            m = _shifted(_col_max(s), off_of(ch, 0))
            p_ref[c] = _probs(s, m, off_of(ch, 0))
            s = _scores(ch.q, ch.k_tile(1))
        s_ref[c] = s
        tile_max.append(_col_max(s))
        ms.append(m)
        alphas.append(jnp.ones((1, ATT_TILE), F32))

    def pending_pv(c, ch, j, alpha):
        return alpha * acc_ref[c] + _pv(ch.vt_tile(jnp.maximum(j - 1 + ch.sub, 0)), p_ref[c])

    def body(j, carry):
        ms, alphas, tile_max = carry
        cur = [s_ref[c] for c in range(n)]
        for c, ch in enumerate(chains):
            acc_ref[c] = pending_pv(c, ch, j, alphas[c])
        new_ms, new_alphas, probs = [], [], []
        for c, ch in enumerate(chains):
            off = off_of(ch, j + ch.sub)
            m_new = jnp.maximum(ms[c], _shifted(tile_max[c], off))
            new_alphas.append(jnp.exp2(ms[c] - m_new))
            probs.append(_probs(cur[c], m_new, off))
            new_ms.append(m_new)
        nxt = [_scores(ch.q, ch.k_tile(j + 1 + ch.sub)) for ch in chains]
        for c in range(n):
            p_ref[c] = probs[c]
            s_ref[c] = nxt[c]
        return tuple(new_ms), tuple(new_alphas), tuple(_col_max(s) for s in nxt)

    ms, alphas, _ = lax.fori_loop(0, n_common, body, (tuple(ms), tuple(alphas), tuple(tile_max)))

    outs = {}
    for c, ch in enumerate(chains):
        acc = pending_pv(c, ch, n_common, alphas[c])
        t = n_common + ch.sub
        s = jnp.where(_causal((ATT_TILE, ATT_TILE), 0), s_ref[c], NEG_INF)
        off = off_of(ch, t)
        m_new = jnp.maximum(jnp.maximum(ms[c], _shifted(_col_max(s), off)),
                            _shifted(_col_max(s_meta[c]), ch.off_meta))
        acc = (jnp.exp2(ms[c] - m_new) * acc + _pv(ch.vt_tile(t), _probs(s, m_new, off))
               + _pv(ch.vt_meta, _probs(s_meta[c], m_new, ch.off_meta)))
        outs.setdefault(ch.sub, []).append(acc[0:FOX_DIM, :] / acc[FOX_DIM:FOX_DIM + 1, :])
    for u, out_t in outs.items():
        o_ref[_key_rows(Q_SUB * step + u), :] = jnp.concatenate(out_t, axis=0).T.astype(BF16)


def _key_rows(j):
    return pl.ds(pl.multiple_of(j * ATT_TILE, ATT_TILE), ATT_TILE)


def _query_steps(n_keys, one_step):
    def body(step, carry):
        one_step(step)
        return carry
    lax.fori_loop(0, n_keys // (Q_SUB * ATT_TILE), body, 0)


def _mla_attn_kernel(q_ref, k_ref, vt_ref, km_ref, vmt_ref, o_ref, *scratch):
    def one_step(step):
        chains = []
        for u in range(Q_SUB):
            for hh in range(2):
                lanes = slice(hh * LANE, (hh + 1) * LANE)
                rows64 = slice(hh * MLA_V, (hh + 1) * MLA_V)
                chains.append(_Chain(
                    q=q_ref[_key_rows(Q_SUB * step + u), lanes], k_meta=km_ref[:, lanes], vt_meta=vmt_ref[rows64, :],
                    k_tile=lambda j, lanes=lanes: k_ref[_key_rows(j), lanes],
                    vt_tile=lambda j, rows64=rows64: vt_ref[j, rows64, :],
                    off_meta=None, off_tile=None, sub=u))
        _attend(step, chains, *scratch, o_ref)

    _query_steps(k_ref.shape[0], one_step)


def _fox_attn_kernel(beta_ref, q_ref, k_ref, ex_ref, spread_ref, vt_ref, km_ref, exm_ref, vmt_ref, crow_ref, o_ref,
                     kaug_ref, *scratch):
    b = pl.program_id(0)
    hp = pl.program_id(1)
    n_tiles = k_ref.shape[0] // ATT_TILE
    lane = lax.broadcasted_iota(jnp.int32, (1, LANE), 1)
    in_a = lane < FOX_DIM

    def spread(packed):
        return jnp.dot(packed, spread_ref[...], preferred_element_type=F32).astype(BF16)

    for c in range(n_tiles):
        rows = slice(c * ATT_TILE, (c + 1) * ATT_TILE)
        kk = k_ref[rows, :]
        ee = spread(ex_ref[rows, :])
        kaug_ref[0, rows, :] = jnp.where(in_a, kk, ee)
        kaug_ref[1, rows, :] = jnp.where(in_a, ee, kk)

    km = km_ref[...]
    exm = spread(exm_ref[...])

    def one_step(step):
        chains = []
        for u in range(Q_SUB):
            tile = Q_SUB * step + u
            q = q_ref[_key_rows(tile), :]
            for hh in range(2):
                own = in_a if hh == 0 else jnp.logical_not(in_a)
                base = FOX_DIM if hh == 0 else 0
                ones = jnp.where((lane >= base) & (lane < base + 3), 1.0, 0.0).astype(BF16)
                rows64 = slice(hh * FOX_DIM, (hh + 1) * FOX_DIM)
                cum_q = crow_ref[2 * hp + hh, pl.ds(tile, 1), :]
                beta_base = b * n_tiles * LANE + 2 * hp + hh
                chains.append(_Chain(
                    q=jnp.where(own, q, ones), k_meta=jnp.where(own, km, exm), vt_meta=vmt_ref[rows64, :],
                    k_tile=lambda j, hh=hh: kaug_ref[hh, _key_rows(j), :],
                    vt_tile=lambda j, rows64=rows64: vt_ref[j, rows64, :],
                    off_meta=cum_q,
                    off_tile=lambda j, cum_q=cum_q, beta_base=beta_base: cum_q - beta_ref[beta_base + j * LANE],
                    sub=u))
        _attend(step, chains, *scratch, o_ref)

    _query_steps(k_ref.shape[0], one_step)


def _attention_specs(batch, seq, width):
    n_tiles = seq // ATT_TILE
    q_spec = pl.BlockSpec((None, seq, width), lambda b, hp, *_: (b, 0, hp))
    k_spec = pl.BlockSpec((None, seq, width), lambda b, hp, *_: (b, 0, hp))
    vt_spec = pl.BlockSpec((None, None, n_tiles, LANE, ATT_TILE), lambda b, hp, *_: (b, hp, 0, 0, 0))
    km_spec = pl.BlockSpec((N_META, width), lambda b, hp, *_: (0, hp))
    vmt_spec = pl.BlockSpec((None, LANE, N_META), lambda b, hp, *_: (hp, 0, 0))
    o_spec = pl.BlockSpec((None, seq, LANE), lambda b, hp, *_: (b, 0, hp))
    return q_spec, k_spec, vt_spec, km_spec, vmt_spec, o_spec


def _pair_scratch():
    n = 2 * Q_SUB
    return [pltpu.VMEM((n, ATT_TILE, ATT_TILE), F32), pltpu.VMEM((n, ATT_TILE, ATT_TILE), BF16),
            pltpu.VMEM((n, MLA_V + L_ROWS, ATT_TILE), F32)]


def _mla_attention(q, k, vt, km, vmt):
    batch, seq, _ = q.shape
    q_spec, k_spec, vt_spec, km_spec, vmt_spec, o_spec = _attention_specs(batch, seq, 2 * LANE)
    return pl.pallas_call(
        _mla_attn_kernel,
        grid=(batch, HEAD_PAIRS),
        in_specs=[q_spec, k_spec, vt_spec, km_spec, vmt_spec],
        out_specs=o_spec,
        out_shape=jax.ShapeDtypeStruct((batch, seq, HEADS * MLA_V), BF16),
        scratch_shapes=_pair_scratch(),
        compiler_params=pltpu.CompilerParams(dimension_semantics=("parallel", "parallel"),
                                             vmem_limit_bytes=VMEM_LIMIT),
        name="mla_attention",
    )(q, k, vt, km, vmt)


def _fox_attention(beta, q, k, ex, spread, vt, km, exm, vmt, crow):
    batch, seq, _ = q.shape
    q_spec, k_spec, vt_spec, km_spec, vmt_spec, o_spec = _attention_specs(batch, seq, LANE)
    crow_spec = pl.BlockSpec((None, HEADS, seq // ATT_TILE, ATT_TILE), lambda b, hp, *_: (b, 0, 0, 0))
    ex_spec = pl.BlockSpec((None, seq, LANE), lambda b, hp, *_: (b, 0, 0))
    exm_spec = pl.BlockSpec((N_META, LANE), lambda b, hp, *_: (0, 0))
    spread_spec = pl.BlockSpec((None, LANE, LANE), lambda b, hp, *_: (hp, 0, 0))
    grid_spec = pltpu.PrefetchScalarGridSpec(
        num_scalar_prefetch=1,
        grid=(batch, HEAD_PAIRS),
        in_specs=[q_spec, k_spec, ex_spec, spread_spec, vt_spec, km_spec, exm_spec, vmt_spec, crow_spec],
        out_specs=o_spec,
        scratch_shapes=[pltpu.VMEM((2, seq, LANE), BF16)] + _pair_scratch(),
    )
    return pl.pallas_call(
        _fox_attn_kernel,
        grid_spec=grid_spec,
        out_shape=jax.ShapeDtypeStruct((batch, seq, HEADS * FOX_DIM), BF16),
        compiler_params=pltpu.CompilerParams(dimension_semantics=("parallel", "parallel"),
                                             vmem_limit_bytes=VMEM_LIMIT),
        name="fox_attention",
    )(beta, q, k, ex, spread, vt, km, exm, vmt, crow)


def _merge_kernel(om_ref, szm_ref, of_ref, szf_ref, ga_ref, gb_ref, x_ref, wa_ref, wb_ref, wo_ref, g_ref, out_ref):
    ya = jnp.dot(om_ref[...] * szm_ref[...], wa_ref[...], preferred_element_type=F32)
    yb = jnp.dot(of_ref[...] * szf_ref[...], wb_ref[...], preferred_element_type=F32)
    mixed_in = (ga_ref[...].astype(F32) * ya + gb_ref[...].astype(F32) * yb).astype(BF16)
    mixed = jnp.dot(mixed_in, wo_ref[...], preferred_element_type=F32)
    out_ref[...] = x_ref[...] + _rms(mixed, g_ref[...])


def _merge(om, szm, of, szf, ga, gb, x2d, wa, wb, wo, g):
    rows, d = x2d.shape
    row_spec = pl.BlockSpec((ROW_TILE, d), lambda i: (i, 0))
    return pl.pallas_call(
        _merge_kernel,
        grid=(rows // ROW_TILE,),
        in_specs=[row_spec] * 7 + [_const_spec(w.shape) for w in (wa, wb, wo, g)],
        out_specs=row_spec,
        out_shape=jax.ShapeDtypeStruct((rows, d), F32),
        compiler_params=pltpu.CompilerParams(dimension_semantics=("parallel",), vmem_limit_bytes=VMEM_LIMIT),
        name="merge",
    )(om, szm, of, szf, ga, gb, x2d, wa, wb, wo, g)


def _transpose_kernel(w_ref, o_ref):
    o_ref[...] = w_ref[...].T.astype(BF16)


def _transpose_bf16(w):
    k, n = w.shape
    cols = 2 * LANE
    return pl.pallas_call(
        _transpose_kernel,
        grid=(n // cols,),
        in_specs=[pl.BlockSpec((k, cols), lambda i: (0, i))],
        out_specs=pl.BlockSpec((cols, k), lambda i: (i, 0)),
        out_shape=jax.ShapeDtypeStruct((n, k), BF16),
        name="weight_transpose",
    )(w)


def _rotate_half_cols(w):
    half = w.shape[-1] // 2
    return jnp.concatenate([-w[..., half:], w[..., :half]], axis=-1)


def _pad_cols(w, before, total):
    return jnp.pad(w, ((0, 0), (before, total - before - w.shape[1])))


def _in_proj_weights(w_in):
    widths = (MLA_Q_RANK, MLA_KV_RANK, MLA_ROPE, HEADS * MLA_V, HEADS * FOX_DIM, HEADS * FOX_DIM,
              HEADS * FOX_DIM, HEADS, HEADS * FOX_DIM, w_in.shape[0], w_in.shape[0])
    assert sum(widths) == w_in.shape[1]
    bounds = np.cumsum((0,) + widths)
    wt = w_in.T
    w_cq, w_ckv, w_kpe, w_zm, w_fq, w_fk, w_fv, w_fl, w_zf, w_ga, w_gb = (
        wt[bounds[n]:bounds[n + 1]] for n in range(len(widths)))
    w_small = jnp.concatenate([
        w_cq, w_ckv,
        _pad_cols(w_fl.T, 0, KPE_LANE).T, w_kpe, _rotate_half_cols(w_kpe.T).T], axis=0)
    assert w_small.shape[0] == SMALL_W
    return [w.astype(BF16) for w in (w_small, w_zm, w_fq, w_fk, w_fv, w_zf, w_ga, w_gb)]


def _head_blocks(w, per_head, first, last, at, width, fn=lambda blk: blk):
    blocks = [_pad_cols(fn(w[:, h * per_head + first:h * per_head + last]), at, width) for h in range(HEADS)]
    return jnp.concatenate(blocks, axis=1)


def _mla_weights(w_uq, w_ukv):
    dq = MLA_NOPE + MLA_ROPE
    dkv = MLA_NOPE + MLA_V
    assert dq + MLA_ROPE == LANE
    wq = jnp.concatenate([w for h in range(HEADS) for w in (
        w_uq[:, h * dq:(h + 1) * dq], _rotate_half_cols(w_uq[:, h * dq + MLA_NOPE:(h + 1) * dq]))], axis=1)
    wk_main = _head_blocks(w_ukv, dkv, 0, MLA_NOPE, 0, LANE)
    wv_t = _transpose_bf16(_head_blocks(w_ukv, dkv, MLA_NOPE, dkv, 0, MLA_V))
    return [w.astype(BF16) for w in (wq, wk_main)] + [wv_t]


def _rope_tables(first_pos, n_pos):
    half = MLA_ROPE // 2
    inv_freq = ROPE_THETA ** (-np.arange(half, dtype=np.float64) / half)
    ang = (first_pos + np.arange(n_pos, dtype=np.float64))[:, None] * inv_freq[None, :]
    cos = np.concatenate([np.cos(ang), np.cos(ang)], axis=1)
    sin = np.concatenate([np.sin(ang), np.sin(ang)], axis=1)
    pad = np.zeros((n_pos, LANE - MLA_NOPE - MLA_ROPE))
    ones = np.ones((n_pos, MLA_NOPE))
    zeros = np.zeros((n_pos, MLA_NOPE))
    cos_q = np.concatenate([ones, cos, pad], axis=1) * (MLA_SCALE * LOG2E)
    sin_q = np.concatenate([zeros, sin, pad], axis=1) * (MLA_SCALE * LOG2E)
    cos_k = np.concatenate([zeros, cos, pad], axis=1)
    sin_k = np.concatenate([zeros, sin, pad], axis=1)
    return [jnp.asarray(t, F32) for t in (cos_q, sin_q, cos_k, sin_k)]


def _placement():
    pack = np.zeros((3, LANE, LANE), np.float32)
    spread = np.zeros((HEAD_PAIRS, LANE, LANE), np.float32)
    for h in range(HEADS):
        base = FOX_DIM if h % 2 == 0 else 0
        for part in range(3):
            pack[part, h, HEADS * part + h] = 1.0
            spread[h // 2, HEADS * part + h, base + part] = 1.0
    return jnp.asarray(pack, BF16), jnp.asarray(spread, BF16)


def kernel(x, meta_tokens, pre_norm_g, w_in, fox_forget_b, mla_q_norm_g, mla_kv_norm_g, w_uq, w_ukv,
           w_br_mla, w_br_fox, w_out, post_norm_g):
    batch, seq, d = x.shape
    assert pre_norm_g.shape[0] == 1, "one layer supported"
    assert meta_tokens.shape[0] == N_META and seq % (Q_SUB * ATT_TILE) == 0 and ATT_TILE == ROW_TILE
    n_tiles = seq // ATT_TILE
    x2d = x.reshape(batch * seq, d)

    in_w = _in_proj_weights(w_in[0])
    mla_w = _mla_weights(w_uq[0], w_ukv[0])
    g_pre = pre_norm_g.astype(F32)
    gq = mla_q_norm_g.astype(F32)
    gkv = mla_kv_norm_g.astype(F32)
    bias_row = jnp.pad(fox_forget_b.astype(F32), ((0, 0), (0, LANE - HEADS)))

    vt_shape = (batch, HEAD_PAIRS, n_tiles, LANE, ATT_TILE)
    vt_spec = pl.BlockSpec((None, HEAD_PAIRS, None, LANE, ROW_TILE), lambda i: (i // n_tiles, 0, i % n_tiles, 0, 0))
    vt_meta_shape = (HEAD_PAIRS, LANE, N_META)
    vt_meta_spec = pl.BlockSpec(vt_meta_shape, lambda i: (0, 0, 0))

    fl, szm, fq, fk, fvt, szf, ga, gb, q, k, vt = _in_proj(
        x2d, g_pre, in_w, gq, gkv, mla_w, _rope_tables(N_META, seq), ROW_TILE, n_tiles, vt_shape, vt_spec)
    fl_m, _, _, fk_m, fvt_m, _, _, _, _, k_m, vt_m = _in_proj(
        meta_tokens.astype(F32), g_pre, in_w, gq, gkv, mla_w, _rope_tables(0, N_META), N_META, 1,
        vt_meta_shape, vt_meta_spec)
    pack, spread = _placement()
    ex, crow, beta, ex_m = _decay_prep(fl, fl_m, bias_row, pack, batch, seq)

    o_mla = _mla_attention(q.reshape(batch, seq, -1), k.reshape(batch, seq, -1), vt, k_m, vt_m)
    o_fox = _fox_attention(beta.reshape(-1),fq.reshape(batch, seq, -1), fk.reshape(batch, seq, -1),
                           ex.reshape(batch, seq, -1), spread, fvt, fk_m, ex_m, fvt_m,
                           crow)

    out = _merge(o_mla.reshape(batch * seq, -1), szm, o_fox.reshape(batch * seq, -1), szf, ga, gb, x2d,
                 w_br_mla[0].astype(BF16), w_br_fox[0].astype(BF16), w_out[0].astype(BF16),
                 post_norm_g.astype(F32))
    return out.reshape(batch, seq, d)
```

```python
import functools
import math
from typing import Any, Callable, NamedTuple

import numpy as np
import jax
import jax.numpy as jnp
from jax import lax
from jax.experimental import pallas as pl
from jax.experimental.pallas import tpu as pltpu

F32 = jnp.float32
BF16 = jnp.bfloat16

N_META = 16
RMS_EPS = 1e-6
HEADS = 16
MLA_Q_RANK = 256
MLA_KV_RANK = 128
MLA_NOPE = 64
MLA_ROPE = 32
MLA_V = 64
MLA_SCALE = 1.0 / math.sqrt(MLA_NOPE + MLA_ROPE)
ROPE_THETA = 10000.0
FOX_DIM = 64
FOX_SCALE = 1.0 / math.sqrt(FOX_DIM)
LOG2E = math.log2(math.e)

LANE = 128
HEAD_PAIRS = HEADS // 2
SMALL_W = 4 * LANE
KPE_LANE = 64
ROW_TILE = 512
ATT_TILE = 512
Q_SUB = 2
L_ROWS = 16
NEG_INF = -1e30
STEP_PAIRS = 2
VMEM_LIMIT = 56 * 1024 * 1024

NT_DIMS = (((1,), (1,)), ((), ()))


def _rms(x, g):
    return x * lax.rsqrt(jnp.mean(x * x, axis=-1, keepdims=True) + RMS_EPS) * g


def _sigmoid(x):
    return 1.0 / (1.0 + jnp.exp(-x))


def _split3(x):
    hi = x.astype(BF16)
    r1 = x - hi.astype(F32)
    mid = r1.astype(BF16)
    lo = (r1 - mid.astype(F32)).astype(BF16)
    return hi, mid, lo


def _const_spec(shape):
    nd = len(shape)
    return pl.BlockSpec(shape, lambda *_: (0,) * nd, pipeline_mode=pl.Buffered(1))


def _in_proj_kernel(x_ref, g_ref, ws_ref, wzm_ref, wfq_ref, wfk_ref, wfvt_ref, wzf_ref, wga_ref, wgb_ref,
                    gq_ref, gkv_ref, wq_ref, wkm_ref, wvt_ref, cosq_ref, sinq_ref, cosk_ref, sink_ref,
                    fl_ref, szm_ref, fq_ref, fk_ref, fvt_ref, szf_ref, ga_ref, gb_ref, q_ref, k_ref, vt_ref):
    u = _rms(x_ref[...], g_ref[...]).astype(BF16)

    def mm(w_ref):
        return lax.dot_general(u, w_ref[...], NT_DIMS, preferred_element_type=F32)

    def store_pairs(out_ref, rows_t):
        for hp in range(HEAD_PAIRS):
            out_ref[hp] = rows_t[hp * LANE:(hp + 1) * LANE, :].astype(BF16)

    z = mm(wzm_ref)
    szm_ref[...] = (z * _sigmoid(z)).astype(BF16)
    fq_ref[...] = (mm(wfq_ref) * (FOX_SCALE * LOG2E)).astype(BF16)
    fk_ref[...] = mm(wfk_ref).astype(BF16)
    store_pairs(fvt_ref, lax.dot_general(wfvt_ref[...], u, NT_DIMS, preferred_element_type=F32))
    z = mm(wzf_ref)
    szf_ref[...] = (z * _sigmoid(z)).astype(BF16)
    ga_ref[...] = _sigmoid(mm(wga_ref)).astype(BF16)
    gb_ref[...] = _sigmoid(mm(wgb_ref)).astype(BF16)

    sm = mm(ws_ref)
    mixed = sm[:, 3 * LANE:4 * LANE]
    fl_ref[...] = mixed
    cn = _rms(sm[:, 0:MLA_Q_RANK], gq_ref[...]).astype(BF16)
    kn = _rms(sm[:, MLA_Q_RANK:MLA_Q_RANK + MLA_KV_RANK], gkv_ref[...]).astype(BF16)
    to_rope_lanes = LANE - MLA_ROPE
    k_rope = mixed * cosk_ref[...] + pltpu.roll(mixed, to_rope_lanes, axis=1) * sink_ref[...]
    qa = jnp.dot(cn, wq_ref[...], preferred_element_type=F32)
    km = jnp.dot(kn, wkm_ref[...], preferred_element_type=F32)
    cosq = cosq_ref[...]
    sinq = sinq_ref[...]
    for h in range(HEADS):
        sl = slice(h * LANE, (h + 1) * LANE)
        rotated = pltpu.roll(qa[:, sl], to_rope_lanes, axis=1)
        q_ref[:, sl] = (qa[:, sl] * cosq + rotated * sinq).astype(BF16)
        k_ref[:, sl] = (km[:, sl] + k_rope).astype(BF16)
    store_pairs(vt_ref, lax.dot_general(wvt_ref[...], kn, NT_DIMS, preferred_element_type=F32))


def _in_proj(x2d, g, weights, gq, gkv, mla_weights, tables, tm, tiles_per_seq, vt_shape, vt_spec):
    rows, d = x2d.shape
    width = weights[1].shape[0]
    row_spec = lambda w: pl.BlockSpec((tm, w), lambda i: (i, 0))
    tab_spec = pl.BlockSpec((tm, LANE), lambda i: (i % tiles_per_seq, 0))
    wide = HEADS * LANE
    outs = [((rows, LANE), F32, row_spec(LANE))]
    for name in ("szm", "fq", "fk", "fvt", "szf", "ga", "gb"):
        outs.append((vt_shape, BF16, vt_spec) if name == "fvt" else ((rows, width), BF16, row_spec(width)))
    outs += [((rows, wide), BF16, row_spec(wide)), ((rows, wide), BF16, row_spec(wide)), (vt_shape, BF16, vt_spec)]
    consts = [g] + list(weights) + [gq, gkv] + list(mla_weights)
    return pl.pallas_call(
        _in_proj_kernel,
        grid=(rows // tm,),
        in_specs=[row_spec(d)] + [_const_spec(c.shape) for c in consts] + [tab_spec] * 4,
        out_specs=[spec for _, _, spec in outs],
        out_shape=[jax.ShapeDtypeStruct(shape, dtype) for shape, dtype, _ in outs],
        compiler_params=pltpu.CompilerParams(dimension_semantics=("parallel",), vmem_limit_bytes=VMEM_LIMIT),
        name="in_proj",
    )(x2d, *consts, *tables)


def _log2_sigmoid(z):
    return (jnp.minimum(z, 0.0) - jnp.log1p(jnp.exp(-jnp.abs(z)))) * LOG2E


def _tri(n):
    row = lax.broadcasted_iota(jnp.int32, (n, n), 0)
    col = lax.broadcasted_iota(jnp.int32, (n, n), 1)
    return jnp.where(col <= row, 1.0, 0.0).astype(BF16)


def _cumsum_rows(tri, x):
    return sum(jnp.dot(tri, part, preferred_element_type=F32) for part in _split3(x))


def _place(x, place_ref):
    return sum(jnp.dot(part, place_ref[k], preferred_element_type=F32)
               for k, part in enumerate(_split3(x))).astype(BF16)


def _decay_kernel(sm_ref, bias_ref, place_ref, ex_ref, crow_ref, beta_ref, *, tile):
    n_tiles = sm_ref.shape[0] // tile
    tri = _tri(tile)
    eye = jnp.where(lax.broadcasted_iota(jnp.int32, (LANE, LANE), 0)
                    == lax.broadcasted_iota(jnp.int32, (LANE, LANE), 1), 1.0, 0.0).astype(BF16)
    carry = jnp.zeros((1, LANE), F32)
    withins = []
    for j in range(0, n_tiles, 2):
        pair = [_log2_sigmoid(sm_ref[(j + e) * tile:(j + e + 1) * tile, :] + bias_ref[...]) for e in range(2)]
        both = _cumsum_rows(tri, jnp.concatenate(pair, axis=1))
        withins += [both[:, 0:LANE], both[:, LANE:]]
    for j in range(n_tiles):
        rows = slice(j * tile, (j + 1) * tile)
        within = withins[j]
        ex_ref[rows, :] = _place(-within, place_ref)
        cum = within + carry
        cum_t = sum(lax.dot_general(eye, part, NT_DIMS, preferred_element_type=F32) for part in _split3(cum))
        crow_ref[:, j, :] = cum_t[0:HEADS, :]
        beta_ref[j:j + 1, :] = carry
        carry = cum[tile - 1:tile, :]


def _decay_meta_kernel(sm_ref, bias_ref, place_ref, ex_ref):
    n = sm_ref.shape[0]
    cum = _cumsum_rows(_tri(n), _log2_sigmoid(sm_ref[...] + bias_ref[...]))
    ex_ref[...] = _place(cum[n - 1:n, :] - cum, place_ref)


def _decay_prep(fl, fl_meta, bias_row, place, batch, seq):
    width = place.shape[2]
    ex, crow, beta = pl.pallas_call(
        functools.partial(_decay_kernel, tile=ATT_TILE),
        grid=(batch,),
        in_specs=[pl.BlockSpec((seq, LANE), lambda b: (b, 0)), _const_spec(bias_row.shape), _const_spec(place.shape)],
        out_specs=[pl.BlockSpec((seq, width), lambda b: (b, 0)),
                   pl.BlockSpec((None, HEADS, seq // ATT_TILE, ATT_TILE), lambda b: (b, 0, 0, 0)),
                   pl.BlockSpec((None, seq // ATT_TILE, LANE), lambda b: (b, 0, 0))],
        out_shape=[jax.ShapeDtypeStruct((batch * seq, width), BF16),
                   jax.ShapeDtypeStruct((batch, HEADS, seq // ATT_TILE, ATT_TILE), F32),
                   jax.ShapeDtypeStruct((batch, seq // ATT_TILE, LANE), F32)],
        compiler_params=pltpu.CompilerParams(dimension_semantics=("parallel",), vmem_limit_bytes=VMEM_LIMIT),
        name="decay_prep",
    )(fl, bias_row, place)
    ex_meta = pl.pallas_call(
        _decay_meta_kernel,
        grid=(1,),
        in_specs=[pl.BlockSpec((N_META, LANE), lambda i: (0, 0)), _const_spec(bias_row.shape),
                  _const_spec(place.shape)],
        out_specs=pl.BlockSpec((N_META, width), lambda i: (0, 0)),
        out_shape=jax.ShapeDtypeStruct((N_META, width), BF16),
        name="decay_prep_meta",
    )(fl_meta, bias_row, place)
    return ex, crow, beta, ex_meta


class _Chain(NamedTuple):
    q: jax.Array
    k_meta: jax.Array
    vt_meta: jax.Array
    k_tile: Callable
    vt_tile: Callable
    off_meta: Any
    off_tile: Any
    sub: int


def _scores(q_h, k_t):
    return lax.dot_general(k_t, q_h, NT_DIMS, preferred_element_type=F32)


def _col_max(s):
    return jnp.max(s, axis=0, keepdims=True)


def _pv(vt_t, p):
    v_aug = jnp.concatenate([vt_t, jnp.ones((L_ROWS, vt_t.shape[1]), BF16)], axis=0)
    return jnp.dot(v_aug, p, preferred_element_type=F32)


def _shifted(m_tile, off):
    return m_tile if off is None else m_tile + off


def _probs(s, m_new, off):
    return jnp.exp2(s - (m_new if off is None else m_new - off)).astype(BF16)


def _causal(shape, first_query):
    return (lax.broadcasted_iota(jnp.int32, shape, 0)
            <= lax.broadcasted_iota(jnp.int32, shape, 1) + first_query)


def _attend(step, chains, s_ref, p_ref, acc_ref, o_ref):
    n = len(chains)
    n_common = Q_SUB * step
    assert all(ch.sub in (0, 1) for ch in chains)

    def off_of(ch, t):
        return None if ch.off_tile is None else ch.off_tile(t)

    s_meta, ms, alphas, tile_max = [], [], [], []
    for c, ch in enumerate(chains):
        s = _scores(ch.q, jnp.concatenate([ch.k_tile(0), ch.k_meta], axis=0))
        s_meta.append(s[ATT_TILE:, :])
        s = s[0:ATT_TILE, :]
        m = jnp.full((1, ATT_TILE), NEG_INF, F32)
        acc_ref[c] = jnp.zeros(acc_ref.shape[1:], F32)
        if ch.sub == 0:
            p_ref[c] = jnp.zeros(p_ref.shape[1:], BF16)
        else:
            m = _shifted(_col_max(s), off_of(ch, 0))
            p_ref[c] = _probs(s, m, off_of(ch, 0))
            s = _scores(ch.q, ch.k_tile(1))
        s_ref[c] = s
        tile_max.append(_col_max(s))
        ms.append(m)
        alphas.append(jnp.ones((1, ATT_TILE), F32))

    def pending_pv(c, ch, j, alpha):
        return alpha * acc_ref[c] + _pv(ch.vt_tile(jnp.maximum(j - 1 + ch.sub, 0)), p_ref[c])

    def body(j, carry):
        ms, alphas, tile_max = carry
        cur = [s_ref[c] for c in range(n)]
        for c, ch in enumerate(chains):
            acc_ref[c] = pending_pv(c, ch, j, alphas[c])
        new_ms, new_alphas, probs = [], [], []
        for c, ch in enumerate(chains):
            off = off_of(ch, j + ch.sub)
            m_new = jnp.maximum(ms[c], _shifted(tile_max[c], off))
            new_alphas.append(jnp.exp2(ms[c] - m_new))
            probs.append(_probs(cur[c], m_new, off))
            new_ms.append(m_new)
        nxt = [_scores(ch.q, ch.k_tile(j + 1 + ch.sub)) for ch in chains]
        for c in range(n):
            p_ref[c] = probs[c]
            s_ref[c] = nxt[c]
        return tuple(new_ms), tuple(new_alphas), tuple(_col_max(s) for s in nxt)

    ms, alphas, _ = lax.fori_loop(0, n_common, body, (tuple(ms), tuple(alphas), tuple(tile_max)))

    outs = {}
    for c, ch in enumerate(chains):
        acc = pending_pv(c, ch, n_common, alphas[c])
        t = n_common + ch.sub
        s = jnp.where(_causal((ATT_TILE, ATT_TILE), 0), s_ref[c], NEG_INF)
        off = off_of(ch, t)
        m_new = jnp.maximum(jnp.maximum(ms[c], _shifted(_col_max(s), off)),
                            _shifted(_col_max(s_meta[c]), ch.off_meta))
        acc = (jnp.exp2(ms[c] - m_new) * acc + _pv(ch.vt_tile(t), _probs(s, m_new, off))
               + _pv(ch.vt_meta, _probs(s_meta[c], m_new, ch.off_meta)))
        outs.setdefault(ch.sub, []).append(acc[0:FOX_DIM, :] / acc[FOX_DIM:FOX_DIM + 1, :])
    for u, out_t in outs.items():
        o_ref[_key_rows(Q_SUB * step + u), :] = jnp.concatenate(out_t, axis=0).T.astype(BF16)


def _key_rows(j):
    return pl.ds(pl.multiple_of(j * ATT_TILE, ATT_TILE), ATT_TILE)


def _query_steps(n_keys, one_step):
    def body(step, carry):
        one_step(step)
        return carry
    lax.fori_loop(0, n_keys // (Q_SUB * ATT_TILE), body, 0)


def _mla_attn_kernel(q_ref, k_ref, vt_ref, km_ref, vmt_ref, o_ref, *scratch):
    def one_step(step):
        chains = []
        for u in range(Q_SUB):
            for head in range(2 * STEP_PAIRS):
                lanes = slice(head * LANE, (head + 1) * LANE)
                pair = head // 2
                rows64 = slice((head % 2) * MLA_V, (head % 2 + 1) * MLA_V)
                chains.append(_Chain(
                    q=q_ref[_key_rows(Q_SUB * step + u), lanes], k_meta=km_ref[:, lanes],
                    vt_meta=vmt_ref[pair, rows64, :],
                    k_tile=lambda j, lanes=lanes: k_ref[_key_rows(j), lanes],
                    vt_tile=lambda j, pair=pair, rows64=rows64: vt_ref[pair, j, rows64, :],
                    off_meta=None, off_tile=None, sub=u))
        _attend(step, chains, *scratch, o_ref)

    _query_steps(k_ref.shape[0], one_step)


def _fox_attn_kernel(beta_ref, q_ref, k_ref, ex_ref, spread_ref, vt_ref, km_ref, exm_ref, vmt_ref, crow_ref, o_ref,
                     kaug_ref, *scratch):
    b = pl.program_id(0)
    first_pair = pl.program_id(1) * STEP_PAIRS
    n_tiles = k_ref.shape[0] // ATT_TILE
    lane = lax.broadcasted_iota(jnp.int32, (1, LANE), 1)
    in_a = lane < FOX_DIM

    def spread(packed, pair):
        return jnp.dot(packed, spread_ref[pair], preferred_element_type=F32).astype(BF16)

    def pair_lanes(pair):
        return slice(pair * LANE, (pair + 1) * LANE)

    for c in range(n_tiles):
        rows = slice(c * ATT_TILE, (c + 1) * ATT_TILE)
        for pair in range(STEP_PAIRS):
            kk = k_ref[rows, pair_lanes(pair)]
            ee = spread(ex_ref[rows, :], pair)
            kaug_ref[2 * pair, rows, :] = jnp.where(in_a, kk, ee)
            kaug_ref[2 * pair + 1, rows, :] = jnp.where(in_a, ee, kk)

    km = [km_ref[:, pair_lanes(pair)] for pair in range(STEP_PAIRS)]
    exm = [spread(exm_ref[...], pair) for pair in range(STEP_PAIRS)]

    def one_step(step):
        chains = []
        for u in range(Q_SUB):
            tile = Q_SUB * step + u
            for pair in range(STEP_PAIRS):
                q = q_ref[_key_rows(tile), pair_lanes(pair)]
                for hh in range(2):
                    own = in_a if hh == 0 else jnp.logical_not(in_a)
                    base = FOX_DIM if hh == 0 else 0
                    ones = jnp.where((lane >= base) & (lane < base + 3), 1.0, 0.0).astype(BF16)
                    rows64 = slice(hh * FOX_DIM, (hh + 1) * FOX_DIM)
                    head = 2 * (first_pair + pair) + hh
                    cum_q = crow_ref[head, pl.ds(tile, 1), :]
                    beta_base = b * n_tiles * LANE + head
                    chains.append(_Chain(
                        q=jnp.where(own, q, ones), k_meta=jnp.where(own, km[pair], exm[pair]),
                        vt_meta=vmt_ref[pair, rows64, :],
                        k_tile=lambda j, slot=2 * pair + hh: kaug_ref[slot, _key_rows(j), :],
                        vt_tile=lambda j, pair=pair, rows64=rows64: vt_ref[pair, j, rows64, :],
                        off_meta=cum_q,
                        off_tile=lambda j, cum_q=cum_q, beta_base=beta_base: cum_q - beta_ref[beta_base + j * LANE],
                        sub=u))
        _attend(step, chains, *scratch, o_ref)

    _query_steps(k_ref.shape[0], one_step)


def _attention_specs(batch, seq, width):
    n_tiles = seq // ATT_TILE
    pairs = STEP_PAIRS
    q_spec = pl.BlockSpec((None, seq, pairs * width), lambda b, hp, *_: (b, 0, hp))
    k_spec = pl.BlockSpec((None, seq, pairs * width), lambda b, hp, *_: (b, 0, hp))
    vt_spec = pl.BlockSpec((None, pairs, n_tiles, LANE, ATT_TILE), lambda b, hp, *_: (b, hp, 0, 0, 0))
    km_spec = pl.BlockSpec((N_META, pairs * width), lambda b, hp, *_: (0, hp))
    vmt_spec = pl.BlockSpec((pairs, LANE, N_META), lambda b, hp, *_: (hp, 0, 0))
    o_spec = pl.BlockSpec((None, seq, pairs * LANE), lambda b, hp, *_: (b, 0, hp))
    return q_spec, k_spec, vt_spec, km_spec, vmt_spec, o_spec


def _pair_scratch():
    n = 2 * STEP_PAIRS * Q_SUB
    return [pltpu.VMEM((n, ATT_TILE, ATT_TILE), F32), pltpu.VMEM((n, ATT_TILE, ATT_TILE), BF16),
            pltpu.VMEM((n, MLA_V + L_ROWS, ATT_TILE), F32)]


def _mla_attention(q, k, vt, km, vmt):
    batch, seq, _ = q.shape
    q_spec, k_spec, vt_spec, km_spec, vmt_spec, o_spec = _attention_specs(batch, seq, 2 * LANE)
    return pl.pallas_call(
        _mla_attn_kernel,
        grid=(batch, HEAD_PAIRS // STEP_PAIRS),
        in_specs=[q_spec, k_spec, vt_spec, km_spec, vmt_spec],
        out_specs=o_spec,
        out_shape=jax.ShapeDtypeStruct((batch, seq, HEADS * MLA_V), BF16),
        scratch_shapes=_pair_scratch(),
        compiler_params=pltpu.CompilerParams(dimension_semantics=("parallel", "parallel"),
                                             vmem_limit_bytes=VMEM_LIMIT),
        name="mla_attention",
    )(q, k, vt, km, vmt)


def _fox_attention(beta, q, k, ex, spread, vt, km, exm, vmt, crow):
    batch, seq, _ = q.shape
    q_spec, k_spec, vt_spec, km_spec, vmt_spec, o_spec = _attention_specs(batch, seq, LANE)
    crow_spec = pl.BlockSpec((None, HEADS, seq // ATT_TILE, ATT_TILE), lambda b, hp, *_: (b, 0, 0, 0))
    ex_spec = pl.BlockSpec((None, seq, LANE), lambda b, hp, *_: (b, 0, 0))
    exm_spec = pl.BlockSpec((N_META, LANE), lambda b, hp, *_: (0, 0))
    spread_spec = pl.BlockSpec((STEP_PAIRS, LANE, LANE), lambda b, hp, *_: (hp, 0, 0))
    grid_spec = pltpu.PrefetchScalarGridSpec(
        num_scalar_prefetch=1,
        grid=(batch, HEAD_PAIRS // STEP_PAIRS),
        in_specs=[q_spec, k_spec, ex_spec, spread_spec, vt_spec, km_spec, exm_spec, vmt_spec, crow_spec],
        out_specs=o_spec,
        scratch_shapes=[pltpu.VMEM((2 * STEP_PAIRS, seq, LANE), BF16)] + _pair_scratch(),
    )
    return pl.pallas_call(
        _fox_attn_kernel,
        grid_spec=grid_spec,
        out_shape=jax.ShapeDtypeStruct((batch, seq, HEADS * FOX_DIM), BF16),
        compiler_params=pltpu.CompilerParams(dimension_semantics=("parallel", "parallel"),
                                             vmem_limit_bytes=VMEM_LIMIT),
        name="fox_attention",
    )(beta, q, k, ex, spread, vt, km, exm, vmt, crow)


def _merge_kernel(om_ref, szm_ref, of_ref, szf_ref, ga_ref, gb_ref, x_ref, wa_ref, wb_ref, wo_ref, g_ref, out_ref):
    ya = jnp.dot(om_ref[...] * szm_ref[...], wa_ref[...], preferred_element_type=F32)
    yb = jnp.dot(of_ref[...] * szf_ref[...], wb_ref[...], preferred_element_type=F32)
    mixed_in = (ga_ref[...].astype(F32) * ya + gb_ref[...].astype(F32) * yb).astype(BF16)
    mixed = jnp.dot(mixed_in, wo_ref[...], preferred_element_type=F32)
    out_ref[...] = x_ref[...] + _rms(mixed, g_ref[...])


def _merge(om, szm, of, szf, ga, gb, x2d, wa, wb, wo, g):
    rows, d = x2d.shape
    row_spec = pl.BlockSpec((ROW_TILE, d), lambda i: (i, 0))
    return pl.pallas_call(
        _merge_kernel,
        grid=(rows // ROW_TILE,),
        in_specs=[row_spec] * 7 + [_const_spec(w.shape) for w in (wa, wb, wo, g)],
        out_specs=row_spec,
        out_shape=jax.ShapeDtypeStruct((rows, d), F32),
        compiler_params=pltpu.CompilerParams(dimension_semantics=("parallel",), vmem_limit_bytes=VMEM_LIMIT),
        name="merge",
    )(om, szm, of, szf, ga, gb, x2d, wa, wb, wo, g)


def _transpose_kernel(w_ref, o_ref):
    o_ref[...] = w_ref[...].T.astype(BF16)


def _transpose_bf16(w):
    k, n = w.shape
    cols = 2 * LANE
    return pl.pallas_call(
        _transpose_kernel,
        grid=(n // cols,),
        in_specs=[pl.BlockSpec((k, cols), lambda i: (0, i))],
        out_specs=pl.BlockSpec((cols, k), lambda i: (i, 0)),
        out_shape=jax.ShapeDtypeStruct((n, k), BF16),
        name="weight_transpose",
    )(w)


def _rotate_half_cols(w):
    half = w.shape[-1] // 2
    return jnp.concatenate([-w[..., half:], w[..., :half]], axis=-1)


def _pad_cols(w, before, total):
    return jnp.pad(w, ((0, 0), (before, total - before - w.shape[1])))


def _in_proj_weights(w_in):
    widths = (MLA_Q_RANK, MLA_KV_RANK, MLA_ROPE, HEADS * MLA_V, HEADS * FOX_DIM, HEADS * FOX_DIM,
              HEADS * FOX_DIM, HEADS, HEADS * FOX_DIM, w_in.shape[0], w_in.shape[0])
    assert sum(widths) == w_in.shape[1]
    bounds = np.cumsum((0,) + widths)
    wt = w_in.T
    w_cq, w_ckv, w_kpe, w_zm, w_fq, w_fk, w_fv, w_fl, w_zf, w_ga, w_gb = (
        wt[bounds[n]:bounds[n + 1]] for n in range(len(widths)))
    w_small = jnp.concatenate([
        w_cq, w_ckv,
        _pad_cols(w_fl.T, 0, KPE_LANE).T, w_kpe, _rotate_half_cols(w_kpe.T).T], axis=0)
    assert w_small.shape[0] == SMALL_W
    return [w.astype(BF16) for w in (w_small, w_zm, w_fq, w_fk, w_fv, w_zf, w_ga, w_gb)]


def _head_blocks(w, per_head, first, last, at, width, fn=lambda blk: blk):
    blocks = [_pad_cols(fn(w[:, h * per_head + first:h * per_head + last]), at, width) for h in range(HEADS)]
    return jnp.concatenate(blocks, axis=1)


def _mla_weights(w_uq, w_ukv):
    dq = MLA_NOPE + MLA_ROPE
    dkv = MLA_NOPE + MLA_V
    assert dq + MLA_ROPE == LANE
    wq = jnp.concatenate([w for h in range(HEADS) for w in (
        w_uq[:, h * dq:(h + 1) * dq], _rotate_half_cols(w_uq[:, h * dq + MLA_NOPE:(h + 1) * dq]))], axis=1)
    wk_main = _head_blocks(w_ukv, dkv, 0, MLA_NOPE, 0, LANE)
    wv_t = _transpose_bf16(_head_blocks(w_ukv, dkv, MLA_NOPE, dkv, 0, MLA_V))
    return [w.astype(BF16) for w in (wq, wk_main)] + [wv_t]


def _rope_tables(first_pos, n_pos):
    half = MLA_ROPE // 2
    inv_freq = ROPE_THETA ** (-np.arange(half, dtype=np.float64) / half)
    ang = (first_pos + np.arange(n_pos, dtype=np.float64))[:, None] * inv_freq[None, :]
    cos = np.concatenate([np.cos(ang), np.cos(ang)], axis=1)
    sin = np.concatenate([np.sin(ang), np.sin(ang)], axis=1)
    pad = np.zeros((n_pos, LANE - MLA_NOPE - MLA_ROPE))
    ones = np.ones((n_pos, MLA_NOPE))
    zeros = np.zeros((n_pos, MLA_NOPE))
    cos_q = np.concatenate([ones, cos, pad], axis=1) * (MLA_SCALE * LOG2E)
    sin_q = np.concatenate([zeros, sin, pad], axis=1) * (MLA_SCALE * LOG2E)
    cos_k = np.concatenate([zeros, cos, pad], axis=1)
    sin_k = np.concatenate([zeros, sin, pad], axis=1)
    return [jnp.asarray(t, F32) for t in (cos_q, sin_q, cos_k, sin_k)]


def _placement():
    pack = np.zeros((3, LANE, LANE), np.float32)
    spread = np.zeros((HEAD_PAIRS, LANE, LANE), np.float32)
    for h in range(HEADS):
        base = FOX_DIM if h % 2 == 0 else 0
        for part in range(3):
            pack[part, h, HEADS * part + h] = 1.0
            spread[h // 2, HEADS * part + h, base + part] = 1.0
    return jnp.asarray(pack, BF16), jnp.asarray(spread, BF16)


def kernel(x, meta_tokens, pre_norm_g, w_in, fox_forget_b, mla_q_norm_g, mla_kv_norm_g, w_uq, w_ukv,
           w_br_mla, w_br_fox, w_out, post_norm_g):
    batch, seq, d = x.shape
    assert pre_norm_g.shape[0] == 1, "one layer supported"
    assert meta_tokens.shape[0] == N_META and seq % (Q_SUB * ATT_TILE) == 0 and ATT_TILE == ROW_TILE
    n_tiles = seq // ATT_TILE
    x2d = x.reshape(batch * seq, d)

    in_w = _in_proj_weights(w_in[0])
    mla_w = _mla_weights(w_uq[0], w_ukv[0])
    g_pre = pre_norm_g.astype(F32)
    gq = mla_q_norm_g.astype(F32)
    gkv = mla_kv_norm_g.astype(F32)
    bias_row = jnp.pad(fox_forget_b.astype(F32), ((0, 0), (0, LANE - HEADS)))

    vt_shape = (batch, HEAD_PAIRS, n_tiles, LANE, ATT_TILE)
    vt_spec = pl.BlockSpec((None, HEAD_PAIRS, None, LANE, ROW_TILE), lambda i: (i // n_tiles, 0, i % n_tiles, 0, 0))
    vt_meta_shape = (HEAD_PAIRS, LANE, N_META)
    vt_meta_spec = pl.BlockSpec(vt_meta_shape, lambda i: (0, 0, 0))

    fl, szm, fq, fk, fvt, szf, ga, gb, q, k, vt = _in_proj(
        x2d, g_pre, in_w, gq, gkv, mla_w, _rope_tables(N_META, seq), ROW_TILE, n_tiles, vt_shape, vt_spec)
    fl_m, _, _, fk_m, fvt_m, _, _, _, _, k_m, vt_m = _in_proj(
        meta_tokens.astype(F32), g_pre, in_w, gq, gkv, mla_w, _rope_tables(0, N_META), N_META, 1,
        vt_meta_shape, vt_meta_spec)
    pack, spread = _placement()
    ex, crow, beta, ex_m = _decay_prep(fl, fl_m, bias_row, pack, batch, seq)

    o_mla = _mla_attention(q.reshape(batch, seq, -1), k.reshape(batch, seq, -1), vt, k_m, vt_m)
    o_fox = _fox_attention(beta.reshape(-1),fq.reshape(batch, seq, -1), fk.reshape(batch, seq, -1),
                           ex.reshape(batch, seq, -1), spread, fvt, fk_m, ex_m, fvt_m,
                           crow)

    out = _merge(o_mla.reshape(batch * seq, -1), szm, o_fox.reshape(batch * seq, -1), szf, ga, gb, x2d,
                 w_br_mla[0].astype(BF16), w_br_fox[0].astype(BF16), w_out[0].astype(BF16),
                 post_norm_g.astype(F32))
    return out.reshape(batch, seq, d)
```

```python
import functools
import math
from typing import Any, Callable, NamedTuple

import numpy as np
import jax
import jax.numpy as jnp
from jax import lax
from jax.experimental import pallas as pl
from jax.experimental.pallas import tpu as pltpu

F32 = jnp.float32
BF16 = jnp.bfloat16

N_META = 16
RMS_EPS = 1e-6
HEADS = 16
MLA_Q_RANK = 256
MLA_KV_RANK = 128
MLA_NOPE = 64
MLA_ROPE = 32
MLA_V = 64
MLA_SCALE = 1.0 / math.sqrt(MLA_NOPE + MLA_ROPE)
ROPE_THETA = 10000.0
FOX_DIM = 64
FOX_SCALE = 1.0 / math.sqrt(FOX_DIM)
LOG2E = math.log2(math.e)

LANE = 128
HEAD_PAIRS = HEADS // 2
SMALL_W = 4 * LANE
KPE_LANE = 64
ROW_TILE = 512
ATT_TILE = 512
Q_SUB = 2
L_ROWS = 16
NEG_INF = -1e30
STEP_PAIRS = 2
VMEM_LIMIT = 56 * 1024 * 1024

NT_DIMS = (((1,), (1,)), ((), ()))


def _rms(x, g):
    return x * lax.rsqrt(jnp.mean(x * x, axis=-1, keepdims=True) + RMS_EPS) * g


def _sigmoid(x):
    return 1.0 / (1.0 + jnp.exp(-x))


def _split3(x):
    hi = x.astype(BF16)
    r1 = x - hi.astype(F32)
    mid = r1.astype(BF16)
    lo = (r1 - mid.astype(F32)).astype(BF16)
    return hi, mid, lo


def _const_spec(shape):
    nd = len(shape)
    return pl.BlockSpec(shape, lambda *_: (0,) * nd, pipeline_mode=pl.Buffered(1))


def _in_proj_kernel(x_ref, g_ref, ws_ref, wzm_ref, wfq_ref, wfk_ref, wfvt_ref, wzf_ref, wga_ref, wgb_ref,
                    gq_ref, gkv_ref, wq_ref, wkm_ref, wvt_ref, cosq_ref, sinq_ref, cosk_ref, sink_ref,
                    fl_ref, szm_ref, fq_ref, fk_ref, fvt_ref, szf_ref, ga_ref, gb_ref, q_ref, k_ref, vt_ref):
    u = _rms(x_ref[...], g_ref[...]).astype(BF16)

    def mm(w_ref):
        return lax.dot_general(u, w_ref[...], NT_DIMS, preferred_element_type=F32)

    def store_pairs(out_ref, rows_t):
        for hp in range(HEAD_PAIRS):
            out_ref[hp] = rows_t[hp * LANE:(hp + 1) * LANE, :].astype(BF16)

    z = mm(wzm_ref)
    szm_ref[...] = (z * _sigmoid(z)).astype(BF16)
    fq_ref[...] = (mm(wfq_ref) * (FOX_SCALE * LOG2E)).astype(BF16)
    fk_ref[...] = mm(wfk_ref).astype(BF16)
    store_pairs(fvt_ref, lax.dot_general(wfvt_ref[...], u, NT_DIMS, preferred_element_type=F32))
    z = mm(wzf_ref)
    szf_ref[...] = (z * _sigmoid(z)).astype(BF16)
    ga_ref[...] = _sigmoid(mm(wga_ref)).astype(BF16)
    gb_ref[...] = _sigmoid(mm(wgb_ref)).astype(BF16)

    sm = mm(ws_ref)
    mixed = sm[:, 3 * LANE:4 * LANE]
    fl_ref[...] = mixed
    cn = _rms(sm[:, 0:MLA_Q_RANK], gq_ref[...]).astype(BF16)
    kn = _rms(sm[:, MLA_Q_RANK:MLA_Q_RANK + MLA_KV_RANK], gkv_ref[...]).astype(BF16)
    to_rope_lanes = LANE - MLA_ROPE
    k_rope = mixed * cosk_ref[...] + pltpu.roll(mixed, to_rope_lanes, axis=1) * sink_ref[...]
    qa = jnp.dot(cn, wq_ref[...], preferred_element_type=F32)
    km = jnp.dot(kn, wkm_ref[...], preferred_element_type=F32)
    cosq = cosq_ref[...]
    sinq = sinq_ref[...]
    for h in range(HEADS):
        sl = slice(h * LANE, (h + 1) * LANE)
        rotated = pltpu.roll(qa[:, sl], to_rope_lanes, axis=1)
        q_ref[:, sl] = (qa[:, sl] * cosq + rotated * sinq).astype(BF16)
        k_ref[:, sl] = (km[:, sl] + k_rope).astype(BF16)
    store_pairs(vt_ref, lax.dot_general(wvt_ref[...], kn, NT_DIMS, preferred_element_type=F32))


def _in_proj(x2d, g, weights, gq, gkv, mla_weights, tables, tm, tiles_per_seq, vt_shape, vt_spec):
    rows, d = x2d.shape
    width = weights[1].shape[0]
    row_spec = lambda w: pl.BlockSpec((tm, w), lambda i: (i, 0))
    tab_spec = pl.BlockSpec((tm, LANE), lambda i: (i % tiles_per_seq, 0))
    wide = HEADS * LANE
    outs = [((rows, LANE), F32, row_spec(LANE))]
    for name in ("szm", "fq", "fk", "fvt", "szf", "ga", "gb"):
        outs.append((vt_shape, BF16, vt_spec) if name == "fvt" else ((rows, width), BF16, row_spec(width)))
    outs += [((rows, wide), BF16, row_spec(wide)), ((rows, wide), BF16, row_spec(wide)), (vt_shape, BF16, vt_spec)]
    consts = [g] + list(weights) + [gq, gkv] + list(mla_weights)
    return pl.pallas_call(
        _in_proj_kernel,
        grid=(rows // tm,),
        in_specs=[row_spec(d)] + [_const_spec(c.shape) for c in consts] + [tab_spec] * 4,
        out_specs=[spec for _, _, spec in outs],
        out_shape=[jax.ShapeDtypeStruct(shape, dtype) for shape, dtype, _ in outs],
        compiler_params=pltpu.CompilerParams(dimension_semantics=("parallel",), vmem_limit_bytes=VMEM_LIMIT),
        name="in_proj",
    )(x2d, *consts, *tables)


def _log2_sigmoid(z):
    return (jnp.minimum(z, 0.0) - jnp.log1p(jnp.exp(-jnp.abs(z)))) * LOG2E


def _tri(n):
    row = lax.broadcasted_iota(jnp.int32, (n, n), 0)
    col = lax.broadcasted_iota(jnp.int32, (n, n), 1)
    return jnp.where(col <= row, 1.0, 0.0).astype(BF16)


def _cumsum_rows(tri, x):
    return sum(jnp.dot(tri, part, preferred_element_type=F32) for part in _split3(x))


def _place(x, place_ref):
    return sum(jnp.dot(part, place_ref[k], preferred_element_type=F32)
               for k, part in enumerate(_split3(x))).astype(BF16)


def _decay_kernel(sm_ref, bias_ref, place_ref, ex_ref, crow_ref, beta_ref, *, tile):
    n_tiles = sm_ref.shape[0] // tile
    tri = _tri(tile)
    eye = jnp.where(lax.broadcasted_iota(jnp.int32, (LANE, LANE), 0)
                    == lax.broadcasted_iota(jnp.int32, (LANE, LANE), 1), 1.0, 0.0).astype(BF16)
    carry = jnp.zeros((1, LANE), F32)
    withins = []
    for j in range(0, n_tiles, 2):
        pair = [_log2_sigmoid(sm_ref[(j + e) * tile:(j + e + 1) * tile, :] + bias_ref[...]) for e in range(2)]
        both = _cumsum_rows(tri, jnp.concatenate(pair, axis=1))
        withins += [both[:, 0:LANE], both[:, LANE:]]
    for j in range(n_tiles):
        rows = slice(j * tile, (j + 1) * tile)
        within = withins[j]
        ex_ref[rows, :] = _place(-within, place_ref)
        cum = within + carry
        cum_t = sum(lax.dot_general(eye, part, NT_DIMS, preferred_element_type=F32) for part in _split3(cum))
        crow_ref[:, j, :] = cum_t[0:HEADS, :]
        beta_ref[j:j + 1, :] = carry
        carry = cum[tile - 1:tile, :]


def _decay_meta_kernel(sm_ref, bias_ref, place_ref, ex_ref):
    n = sm_ref.shape[0]
    cum = _cumsum_rows(_tri(n), _log2_sigmoid(sm_ref[...] + bias_ref[...]))
    ex_ref[...] = _place(cum[n - 1:n, :] - cum, place_ref)


def _decay_prep(fl, fl_meta, bias_row, place, batch, seq):
    width = place.shape[2]
    ex, crow, beta = pl.pallas_call(
        functools.partial(_decay_kernel, tile=ATT_TILE),
        grid=(batch,),
        in_specs=[pl.BlockSpec((seq, LANE), lambda b: (b, 0)), _const_spec(bias_row.shape), _const_spec(place.shape)],
        out_specs=[pl.BlockSpec((seq, width), lambda b: (b, 0)),
                   pl.BlockSpec((None, HEADS, seq // ATT_TILE, ATT_TILE), lambda b: (b, 0, 0, 0)),
                   pl.BlockSpec((None, seq // ATT_TILE, LANE), lambda b: (b, 0, 0))],
        out_shape=[jax.ShapeDtypeStruct((batch * seq, width), BF16),
                   jax.ShapeDtypeStruct((batch, HEADS, seq // ATT_TILE, ATT_TILE), F32),
                   jax.ShapeDtypeStruct((batch, seq // ATT_TILE, LANE), F32)],
        compiler_params=pltpu.CompilerParams(dimension_semantics=("parallel",), vmem_limit_bytes=VMEM_LIMIT),
        name="decay_prep",
    )(fl, bias_row, place)
    ex_meta = pl.pallas_call(
        _decay_meta_kernel,
        grid=(1,),
        in_specs=[pl.BlockSpec((N_META, LANE), lambda i: (0, 0)), _const_spec(bias_row.shape),
                  _const_spec(place.shape)],
        out_specs=pl.BlockSpec((N_META, width), lambda i: (0, 0)),
        out_shape=jax.ShapeDtypeStruct((N_META, width), BF16),
        name="decay_prep_meta",
    )(fl_meta, bias_row, place)
    return ex, crow, beta, ex_meta


class _Chain(NamedTuple):
    q: jax.Array
    k_meta: jax.Array
    vt_meta: jax.Array
    k_tile: Callable
    vt_tile: Callable
    off_meta: Any
    off_tile: Any
    sub: int


def _scores(q_h, k_t):
    return lax.dot_general(k_t, q_h, NT_DIMS, preferred_element_type=F32)


def _col_max(s):
    return jnp.max(s, axis=0, keepdims=True)


def _pv(vt_t, p):
    v_aug = jnp.concatenate([vt_t, jnp.ones((L_ROWS, vt_t.shape[1]), BF16)], axis=0)
    return jnp.dot(v_aug, p, preferred_element_type=F32)


def _shifted(m_tile, off):
    return m_tile if off is None else m_tile + off


def _probs(s, m_new, off):
    return jnp.exp2(s - (m_new if off is None else m_new - off)).astype(BF16)


def _zero_of(x):
    bits = lax.bitcast_convert_type(x, jnp.uint32)
    sixteen = jnp.full(x.shape, 16, jnp.uint32)
    return lax.bitcast_convert_type(lax.shift_right_logical(lax.shift_right_logical(bits, sixteen), sixteen), F32)


def _causal(shape, first_query):
    return (lax.broadcasted_iota(jnp.int32, shape, 0)
            <= lax.broadcasted_iota(jnp.int32, shape, 1) + first_query)


def _attend(step, chains, s_ref, p_ref, acc_ref, o_ref):
    n = len(chains)
    n_common = Q_SUB * step
    assert all(ch.sub in (0, 1) for ch in chains)

    def off_of(ch, t):
        return None if ch.off_tile is None else ch.off_tile(t)

    s_meta, ms, alphas, tile_max = [], [], [], []
    for c, ch in enumerate(chains):
        s = _scores(ch.q, jnp.concatenate([ch.k_tile(0), ch.k_meta], axis=0))
        s_meta.append(s[ATT_TILE:, :])
        s = s[0:ATT_TILE, :]
        m = jnp.full((1, ATT_TILE), NEG_INF, F32)
        if ch.sub == 1:
            m = _shifted(_col_max(s), off_of(ch, 0))
            p_ref[c] = _probs(s, m, off_of(ch, 0))
            s = _scores(ch.q, ch.k_tile(1))
        s_ref[c] = s
        tile_max.append(_col_max(s))
        zero = _zero_of(tile_max[c])
        acc_ref[c] = jnp.broadcast_to(zero, acc_ref.shape[1:])
        if ch.sub == 0:
            p_ref[c] = jnp.broadcast_to(zero.astype(BF16), p_ref.shape[1:])
        ms.append(m)
        alphas.append(jnp.ones((1, ATT_TILE), F32))

    def pending_pv(c, ch, j, alpha):
        return alpha * acc_ref[c] + _pv(ch.vt_tile(jnp.maximum(j - 1 + ch.sub, 0)), p_ref[c])

    def body(j, carry):
        ms, alphas, tile_max = carry
        cur = [s_ref[c] for c in range(n)]
        for c, ch in enumerate(chains):
            acc_ref[c] = pending_pv(c, ch, j, alphas[c])
        new_ms, new_alphas, probs = [], [], []
        for c, ch in enumerate(chains):
            off = off_of(ch, j + ch.sub)
            m_new = jnp.maximum(ms[c], _shifted(tile_max[c], off))
            new_alphas.append(jnp.exp2(ms[c] - m_new))
            probs.append(_probs(cur[c], m_new, off))
            new_ms.append(m_new)
        nxt = [_scores(ch.q, ch.k_tile(j + 1 + ch.sub)) for ch in chains]
        for c in range(n):
            p_ref[c] = probs[c]
            s_ref[c] = nxt[c]
        return tuple(new_ms), tuple(new_alphas), tuple(_col_max(s) for s in nxt)

    ms, alphas, _ = lax.fori_loop(0, n_common, body, (tuple(ms), tuple(alphas), tuple(tile_max)))

    outs = {}
    for c, ch in enumerate(chains):
        acc = pending_pv(c, ch, n_common, alphas[c])
        t = n_common + ch.sub
        s = jnp.where(_causal((ATT_TILE, ATT_TILE), 0), s_ref[c], NEG_INF)
        off = off_of(ch, t)
        m_new = jnp.maximum(jnp.maximum(ms[c], _shifted(_col_max(s), off)),
                            _shifted(_col_max(s_meta[c]), ch.off_meta))
        acc = (jnp.exp2(ms[c] - m_new) * acc + _pv(ch.vt_tile(t), _probs(s, m_new, off))
               + _pv(ch.vt_meta, _probs(s_meta[c], m_new, ch.off_meta)))
        outs.setdefault(ch.sub, []).append(acc[0:FOX_DIM, :] / acc[FOX_DIM:FOX_DIM + 1, :])
    for u, out_t in outs.items():
        o_ref[_key_rows(Q_SUB * step + u), :] = jnp.concatenate(out_t, axis=0).T.astype(BF16)


def _key_rows(j):
    return pl.ds(pl.multiple_of(j * ATT_TILE, ATT_TILE), ATT_TILE)


def _query_steps(n_keys, one_step):
    def body(step, carry):
        one_step(step)
        return carry
    lax.fori_loop(0, n_keys // (Q_SUB * ATT_TILE), body, 0)


def _mla_attn_kernel(q_ref, k_ref, vt_ref, km_ref, vmt_ref, o_ref, *scratch):
    def one_step(step):
        chains = []
        for u in range(Q_SUB):
            for head in range(2 * STEP_PAIRS):
                lanes = slice(head * LANE, (head + 1) * LANE)
                pair = head // 2
                rows64 = slice((head % 2) * MLA_V, (head % 2 + 1) * MLA_V)
                chains.append(_Chain(
                    q=q_ref[_key_rows(Q_SUB * step + u), lanes], k_meta=km_ref[:, lanes],
                    vt_meta=vmt_ref[pair, rows64, :],
                    k_tile=lambda j, lanes=lanes: k_ref[_key_rows(j), lanes],
                    vt_tile=lambda j, pair=pair, rows64=rows64: vt_ref[pair, j, rows64, :],
                    off_meta=None, off_tile=None, sub=u))
        _attend(step, chains, *scratch, o_ref)

    _query_steps(k_ref.shape[0], one_step)


def _fox_attn_kernel(beta_ref, q_ref, k_ref, ex_ref, spread_ref, vt_ref, km_ref, exm_ref, vmt_ref, crow_ref, o_ref,
                     kaug_ref, *scratch):
    b = pl.program_id(0)
    first_pair = pl.program_id(1) * STEP_PAIRS
    n_tiles = k_ref.shape[0] // ATT_TILE
    lane = lax.broadcasted_iota(jnp.int32, (1, LANE), 1)
    in_a = lane < FOX_DIM

    def spread(packed, pair):
        return jnp.dot(packed, spread_ref[pair], preferred_element_type=F32).astype(BF16)

    def pair_lanes(pair):
        return slice(pair * LANE, (pair + 1) * LANE)

    for c in range(n_tiles):
        rows = slice(c * ATT_TILE, (c + 1) * ATT_TILE)
        for pair in range(STEP_PAIRS):
            kk = k_ref[rows, pair_lanes(pair)]
            ee = spread(ex_ref[rows, :], pair)
            kaug_ref[2 * pair, rows, :] = jnp.where(in_a, kk, ee)
            kaug_ref[2 * pair + 1, rows, :] = jnp.where(in_a, ee, kk)

    km = [km_ref[:, pair_lanes(pair)] for pair in range(STEP_PAIRS)]
    exm = [spread(exm_ref[...], pair) for pair in range(STEP_PAIRS)]

    def one_step(step):
        chains = []
        for u in range(Q_SUB):
            tile = Q_SUB * step + u
            for pair in range(STEP_PAIRS):
                q = q_ref[_key_rows(tile), pair_lanes(pair)]
                for hh in range(2):
                    own = in_a if hh == 0 else jnp.logical_not(in_a)
                    base = FOX_DIM if hh == 0 else 0
                    ones = jnp.where((lane >= base) & (lane < base + 3), 1.0, 0.0).astype(BF16)
                    rows64 = slice(hh * FOX_DIM, (hh + 1) * FOX_DIM)
                    head = 2 * (first_pair + pair) + hh
                    cum_q = crow_ref[head, pl.ds(tile, 1), :]
                    beta_base = b * n_tiles * LANE + head
                    chains.append(_Chain(
                        q=jnp.where(own, q, ones), k_meta=jnp.where(own, km[pair], exm[pair]),
                        vt_meta=vmt_ref[pair, rows64, :],
                        k_tile=lambda j, slot=2 * pair + hh: kaug_ref[slot, _key_rows(j), :],
                        vt_tile=lambda j, pair=pair, rows64=rows64: vt_ref[pair, j, rows64, :],
                        off_meta=cum_q,
                        off_tile=lambda j, cum_q=cum_q, beta_base=beta_base: cum_q - beta_ref[beta_base + j * LANE],
                        sub=u))
        _attend(step, chains, *scratch, o_ref)

    _query_steps(k_ref.shape[0], one_step)


def _attention_specs(batch, seq, width):
    n_tiles = seq // ATT_TILE
    pairs = STEP_PAIRS
    q_spec = pl.BlockSpec((None, seq, pairs * width), lambda b, hp, *_: (b, 0, hp))
    k_spec = pl.BlockSpec((None, seq, pairs * width), lambda b, hp, *_: (b, 0, hp))
    vt_spec = pl.BlockSpec((None, pairs, n_tiles, LANE, ATT_TILE), lambda b, hp, *_: (b, hp, 0, 0, 0))
    km_spec = pl.BlockSpec((N_META, pairs * width), lambda b, hp, *_: (0, hp))
    vmt_spec = pl.BlockSpec((pairs, LANE, N_META), lambda b, hp, *_: (hp, 0, 0))
    o_spec = pl.BlockSpec((None, seq, pairs * LANE), lambda b, hp, *_: (b, 0, hp))
    return q_spec, k_spec, vt_spec, km_spec, vmt_spec, o_spec


def _pair_scratch():
    n = 2 * STEP_PAIRS * Q_SUB
    return [pltpu.VMEM((n, ATT_TILE, ATT_TILE), F32), pltpu.VMEM((n, ATT_TILE, ATT_TILE), BF16),
            pltpu.VMEM((n, MLA_V + L_ROWS, ATT_TILE), F32)]


def _mla_attention(q, k, vt, km, vmt):
    batch, seq, _ = q.shape
    q_spec, k_spec, vt_spec, km_spec, vmt_spec, o_spec = _attention_specs(batch, seq, 2 * LANE)
    return pl.pallas_call(
        _mla_attn_kernel,
        grid=(batch, HEAD_PAIRS // STEP_PAIRS),
        in_specs=[q_spec, k_spec, vt_spec, km_spec, vmt_spec],
        out_specs=o_spec,
        out_shape=jax.ShapeDtypeStruct((batch, seq, HEADS * MLA_V), BF16),
        scratch_shapes=_pair_scratch(),
        compiler_params=pltpu.CompilerParams(dimension_semantics=("parallel", "parallel"),
                                             vmem_limit_bytes=VMEM_LIMIT),
        name="mla_attention",
    )(q, k, vt, km, vmt)


def _fox_attention(beta, q, k, ex, spread, vt, km, exm, vmt, crow):
    batch, seq, _ = q.shape
    q_spec, k_spec, vt_spec, km_spec, vmt_spec, o_spec = _attention_specs(batch, seq, LANE)
    crow_spec = pl.BlockSpec((None, HEADS, seq // ATT_TILE, ATT_TILE), lambda b, hp, *_: (b, 0, 0, 0))
    ex_spec = pl.BlockSpec((None, seq, LANE), lambda b, hp, *_: (b, 0, 0))
    exm_spec = pl.BlockSpec((N_META, LANE), lambda b, hp, *_: (0, 0))
    spread_spec = pl.BlockSpec((STEP_PAIRS, LANE, LANE), lambda b, hp, *_: (hp, 0, 0))
    grid_spec = pltpu.PrefetchScalarGridSpec(
        num_scalar_prefetch=1,
        grid=(batch, HEAD_PAIRS // STEP_PAIRS),
        in_specs=[q_spec, k_spec, ex_spec, spread_spec, vt_spec, km_spec, exm_spec, vmt_spec, crow_spec],
        out_specs=o_spec,
        scratch_shapes=[pltpu.VMEM((2 * STEP_PAIRS, seq, LANE), BF16)] + _pair_scratch(),
    )
    return pl.pallas_call(
        _fox_attn_kernel,
        grid_spec=grid_spec,
        out_shape=jax.ShapeDtypeStruct((batch, seq, HEADS * FOX_DIM), BF16),
        compiler_params=pltpu.CompilerParams(dimension_semantics=("parallel", "parallel"),
                                             vmem_limit_bytes=VMEM_LIMIT),
        name="fox_attention",
    )(beta, q, k, ex, spread, vt, km, exm, vmt, crow)


def _merge_kernel(om_ref, szm_ref, of_ref, szf_ref, ga_ref, gb_ref, x_ref, wa_ref, wb_ref, wo_ref, g_ref, out_ref):
    ya = jnp.dot(om_ref[...] * szm_ref[...], wa_ref[...], preferred_element_type=F32)
    yb = jnp.dot(of_ref[...] * szf_ref[...], wb_ref[...], preferred_element_type=F32)
    mixed_in = (ga_ref[...].astype(F32) * ya + gb_ref[...].astype(F32) * yb).astype(BF16)
    mixed = jnp.dot(mixed_in, wo_ref[...], preferred_element_type=F32)
    out_ref[...] = x_ref[...] + _rms(mixed, g_ref[...])


def _merge(om, szm, of, szf, ga, gb, x2d, wa, wb, wo, g):
    rows, d = x2d.shape
    row_spec = pl.BlockSpec((ROW_TILE, d), lambda i: (i, 0))
    return pl.pallas_call(
        _merge_kernel,
        grid=(rows // ROW_TILE,),
        in_specs=[row_spec] * 7 + [_const_spec(w.shape) for w in (wa, wb, wo, g)],
        out_specs=row_spec,
        out_shape=jax.ShapeDtypeStruct((rows, d), F32),
        compiler_params=pltpu.CompilerParams(dimension_semantics=("parallel",), vmem_limit_bytes=VMEM_LIMIT),
        name="merge",
    )(om, szm, of, szf, ga, gb, x2d, wa, wb, wo, g)


def _transpose_kernel(w_ref, o_ref):
    o_ref[...] = w_ref[...].T.astype(BF16)


def _transpose_bf16(w):
    k, n = w.shape
    cols = 2 * LANE
    return pl.pallas_call(
        _transpose_kernel,
        grid=(n // cols,),
        in_specs=[pl.BlockSpec((k, cols), lambda i: (0, i))],
        out_specs=pl.BlockSpec((cols, k), lambda i: (i, 0)),
        out_shape=jax.ShapeDtypeStruct((n, k), BF16),
        name="weight_transpose",
    )(w)


def _rotate_half_cols(w):
    half = w.shape[-1] // 2
    return jnp.concatenate([-w[..., half:], w[..., :half]], axis=-1)


def _pad_cols(w, before, total):
    return jnp.pad(w, ((0, 0), (before, total - before - w.shape[1])))


def _in_proj_weights(w_in):
    widths = (MLA_Q_RANK, MLA_KV_RANK, MLA_ROPE, HEADS * MLA_V, HEADS * FOX_DIM, HEADS * FOX_DIM,
              HEADS * FOX_DIM, HEADS, HEADS * FOX_DIM, w_in.shape[0], w_in.shape[0])
    assert sum(widths) == w_in.shape[1]
    bounds = np.cumsum((0,) + widths)
    wt = w_in.T
    w_cq, w_ckv, w_kpe, w_zm, w_fq, w_fk, w_fv, w_fl, w_zf, w_ga, w_gb = (
        wt[bounds[n]:bounds[n + 1]] for n in range(len(widths)))
    w_small = jnp.concatenate([
        w_cq, w_ckv,
        _pad_cols(w_fl.T, 0, KPE_LANE).T, w_kpe, _rotate_half_cols(w_kpe.T).T], axis=0)
    assert w_small.shape[0] == SMALL_W
    return [w.astype(BF16) for w in (w_small, w_zm, w_fq, w_fk, w_fv, w_zf, w_ga, w_gb)]


def _head_blocks(w, per_head, first, last, at, width, fn=lambda blk: blk):
    blocks = [_pad_cols(fn(w[:, h * per_head + first:h * per_head + last]), at, width) for h in range(HEADS)]
    return jnp.concatenate(blocks, axis=1)


def _mla_weights(w_uq, w_ukv):
    dq = MLA_NOPE + MLA_ROPE
    dkv = MLA_NOPE + MLA_V
    assert dq + MLA_ROPE == LANE
    wq = jnp.concatenate([w for h in range(HEADS) for w in (
        w_uq[:, h * dq:(h + 1) * dq], _rotate_half_cols(w_uq[:, h * dq + MLA_NOPE:(h + 1) * dq]))], axis=1)
    wk_main = _head_blocks(w_ukv, dkv, 0, MLA_NOPE, 0, LANE)
    wv_t = _transpose_bf16(_head_blocks(w_ukv, dkv, MLA_NOPE, dkv, 0, MLA_V))
    return [w.astype(BF16) for w in (wq, wk_main)] + [wv_t]


def _rope_tables(first_pos, n_pos):
    half = MLA_ROPE // 2
    inv_freq = ROPE_THETA ** (-np.arange(half, dtype=np.float64) / half)
    ang = (first_pos + np.arange(n_pos, dtype=np.float64))[:, None] * inv_freq[None, :]
    cos = np.concatenate([np.cos(ang), np.cos(ang)], axis=1)
    sin = np.concatenate([np.sin(ang), np.sin(ang)], axis=1)
    pad = np.zeros((n_pos, LANE - MLA_NOPE - MLA_ROPE))
    ones = np.ones((n_pos, MLA_NOPE))
    zeros = np.zeros((n_pos, MLA_NOPE))
    cos_q = np.concatenate([ones, cos, pad], axis=1) * (MLA_SCALE * LOG2E)
    sin_q = np.concatenate([zeros, sin, pad], axis=1) * (MLA_SCALE * LOG2E)
    cos_k = np.concatenate([zeros, cos, pad], axis=1)
    sin_k = np.concatenate([zeros, sin, pad], axis=1)
    return [jnp.asarray(t, F32) for t in (cos_q, sin_q, cos_k, sin_k)]


def _placement():
    pack = np.zeros((3, LANE, LANE), np.float32)
    spread = np.zeros((HEAD_PAIRS, LANE, LANE), np.float32)
    for h in range(HEADS):
        base = FOX_DIM if h % 2 == 0 else 0
        for part in range(3):
            pack[part, h, HEADS * part + h] = 1.0
            spread[h // 2, HEADS * part + h, base + part] = 1.0
    return jnp.asarray(pack, BF16), jnp.asarray(spread, BF16)


def kernel(x, meta_tokens, pre_norm_g, w_in, fox_forget_b, mla_q_norm_g, mla_kv_norm_g, w_uq, w_ukv,
           w_br_mla, w_br_fox, w_out, post_norm_g):
    batch, seq, d = x.shape
    assert pre_norm_g.shape[0] == 1, "one layer supported"
    assert meta_tokens.shape[0] == N_META and seq % (Q_SUB * ATT_TILE) == 0 and ATT_TILE == ROW_TILE
    n_tiles = seq // ATT_TILE
    x2d = x.reshape(batch * seq, d)

    in_w = _in_proj_weights(w_in[0])
    mla_w = _mla_weights(w_uq[0], w_ukv[0])
    g_pre = pre_norm_g.astype(F32)
    gq = mla_q_norm_g.astype(F32)
    gkv = mla_kv_norm_g.astype(F32)
    bias_row = jnp.pad(fox_forget_b.astype(F32), ((0, 0), (0, LANE - HEADS)))

    vt_shape = (batch, HEAD_PAIRS, n_tiles, LANE, ATT_TILE)
    vt_spec = pl.BlockSpec((None, HEAD_PAIRS, None, LANE, ROW_TILE), lambda i: (i // n_tiles, 0, i % n_tiles, 0, 0))
    vt_meta_shape = (HEAD_PAIRS, LANE, N_META)
    vt_meta_spec = pl.BlockSpec(vt_meta_shape, lambda i: (0, 0, 0))

    fl, szm, fq, fk, fvt, szf, ga, gb, q, k, vt = _in_proj(
        x2d, g_pre, in_w, gq, gkv, mla_w, _rope_tables(N_META, seq), ROW_TILE, n_tiles, vt_shape, vt_spec)
    fl_m, _, _, fk_m, fvt_m, _, _, _, _, k_m, vt_m = _in_proj(
        meta_tokens.astype(F32), g_pre, in_w, gq, gkv, mla_w, _rope_tables(0, N_META), N_META, 1,
        vt_meta_shape, vt_meta_spec)
    pack, spread = _placement()
    ex, crow, beta, ex_m = _decay_prep(fl, fl_m, bias_row, pack, batch, seq)

    o_mla = _mla_attention(q.reshape(batch, seq, -1), k.reshape(batch, seq, -1), vt, k_m, vt_m)
    o_fox = _fox_attention(beta.reshape(-1),fq.reshape(batch, seq, -1), fk.reshape(batch, seq, -1),
                           ex.reshape(batch, seq, -1), spread, fvt, fk_m, ex_m, fvt_m,
                           crow)

    out = _merge(o_mla.reshape(batch * seq, -1), szm, o_fox.reshape(batch * seq, -1), szf, ga, gb, x2d,
                 w_br_mla[0].astype(BF16), w_br_fox[0].astype(BF16), w_out[0].astype(BF16),
                 post_norm_g.astype(F32))
    return out.reshape(batch, seq, d)
```

```python
import functools
import math
from typing import Any, Callable, NamedTuple

import numpy as np
import jax
import jax.numpy as jnp
from jax import lax
from jax.experimental import pallas as pl
from jax.experimental.pallas import tpu as pltpu

F32 = jnp.float32
BF16 = jnp.bfloat16

N_META = 16
RMS_EPS = 1e-6
HEADS = 16
MLA_Q_RANK = 256
MLA_KV_RANK = 128
MLA_NOPE = 64
MLA_ROPE = 32
MLA_V = 64
MLA_SCALE = 1.0 / math.sqrt(MLA_NOPE + MLA_ROPE)
ROPE_THETA = 10000.0
FOX_DIM = 64
FOX_SCALE = 1.0 / math.sqrt(FOX_DIM)
LOG2E = math.log2(math.e)

LANE = 128
HEAD_PAIRS = HEADS // 2
SMALL_W = 4 * LANE
KPE_LANE = 64
ROW_TILE = 512
ATT_TILE = 512
Q_SUB = 2
L_ROWS = 16
NEG_INF = -1e30
STEP_PAIRS = 2
VMEM_LIMIT = 56 * 1024 * 1024

NT_DIMS = (((1,), (1,)), ((), ()))


def _rms(x, g):
    return x * lax.rsqrt(jnp.mean(x * x, axis=-1, keepdims=True) + RMS_EPS) * g


def _sigmoid(x):
    return 1.0 / (1.0 + jnp.exp(-x))


def _split3(x):
    hi = x.astype(BF16)
    r1 = x - hi.astype(F32)
    mid = r1.astype(BF16)
    lo = (r1 - mid.astype(F32)).astype(BF16)
    return hi, mid, lo


def _const_spec(shape):
    nd = len(shape)
    return pl.BlockSpec(shape, lambda *_: (0,) * nd, pipeline_mode=pl.Buffered(1))


def _in_proj_kernel(x_ref, g_ref, ws_ref, wzm_ref, wfq_ref, wfk_ref, wfvt_ref, wzf_ref, wga_ref, wgb_ref,
                    gq_ref, gkv_ref, wq_ref, wkm_ref, wvt_ref, cosq_ref, sinq_ref, cosk_ref, sink_ref,
                    fl_ref, szm_ref, fq_ref, fk_ref, fvt_ref, szf_ref, ga_ref, gb_ref, q_ref, k_ref, vt_ref):
    u = _rms(x_ref[...], g_ref[...]).astype(BF16)

    def mm(w_ref):
        return lax.dot_general(u, w_ref[...], NT_DIMS, preferred_element_type=F32)

    def store_pairs(out_ref, rows_t):
        for hp in range(HEAD_PAIRS):
            out_ref[hp] = rows_t[hp * LANE:(hp + 1) * LANE, :].astype(BF16)

    z = mm(wzm_ref)
    szm_ref[...] = (z * _sigmoid(z)).astype(BF16)
    fq_ref[...] = (mm(wfq_ref) * (FOX_SCALE * LOG2E)).astype(BF16)
    fk_ref[...] = mm(wfk_ref).astype(BF16)
    store_pairs(fvt_ref, lax.dot_general(wfvt_ref[...], u, NT_DIMS, preferred_element_type=F32))
    z = mm(wzf_ref)
    szf_ref[...] = (z * _sigmoid(z)).astype(BF16)
    ga_ref[...] = _sigmoid(mm(wga_ref)).astype(BF16)
    gb_ref[...] = _sigmoid(mm(wgb_ref)).astype(BF16)

    sm = mm(ws_ref)
    mixed = sm[:, 3 * LANE:4 * LANE]
    fl_ref[...] = mixed
    cn = _rms(sm[:, 0:MLA_Q_RANK], gq_ref[...]).astype(BF16)
    kn = _rms(sm[:, MLA_Q_RANK:MLA_Q_RANK + MLA_KV_RANK], gkv_ref[...]).astype(BF16)
    to_rope_lanes = LANE - MLA_ROPE
    k_rope = mixed * cosk_ref[...] + pltpu.roll(mixed, to_rope_lanes, axis=1) * sink_ref[...]
    qa = jnp.dot(cn, wq_ref[...], preferred_element_type=F32)
    km = jnp.dot(kn, wkm_ref[...], preferred_element_type=F32)
    cosq = cosq_ref[...]
    sinq = sinq_ref[...]
    for h in range(HEADS):
        sl = slice(h * LANE, (h + 1) * LANE)
        rotated = pltpu.roll(qa[:, sl], to_rope_lanes, axis=1)
        q_ref[:, sl] = (qa[:, sl] * cosq + rotated * sinq).astype(BF16)
        k_ref[:, sl] = (km[:, sl] + k_rope).astype(BF16)
    store_pairs(vt_ref, lax.dot_general(wvt_ref[...], kn, NT_DIMS, preferred_element_type=F32))


def _in_proj(x2d, g, weights, gq, gkv, mla_weights, tables, tm, tiles_per_seq, vt_shape, vt_spec):
    rows, d = x2d.shape
    width = weights[1].shape[0]
    row_spec = lambda w: pl.BlockSpec((tm, w), lambda i: (i, 0))
    tab_spec = pl.BlockSpec((tm, LANE), lambda i: (i % tiles_per_seq, 0))
    wide = HEADS * LANE
    outs = [((rows, LANE), F32, row_spec(LANE))]
    for name in ("szm", "fq", "fk", "fvt", "szf", "ga", "gb"):
        outs.append((vt_shape, BF16, vt_spec) if name == "fvt" else ((rows, width), BF16, row_spec(width)))
    outs += [((rows, wide), BF16, row_spec(wide)), ((rows, wide), BF16, row_spec(wide)), (vt_shape, BF16, vt_spec)]
    consts = [g] + list(weights) + [gq, gkv] + list(mla_weights)
    return pl.pallas_call(
        _in_proj_kernel,
        grid=(rows // tm,),
        in_specs=[row_spec(d)] + [_const_spec(c.shape) for c in consts] + [tab_spec] * 4,
        out_specs=[spec for _, _, spec in outs],
        out_shape=[jax.ShapeDtypeStruct(shape, dtype) for shape, dtype, _ in outs],
        compiler_params=pltpu.CompilerParams(dimension_semantics=("parallel",), vmem_limit_bytes=VMEM_LIMIT),
        name="in_proj",
    )(x2d, *consts, *tables)


def _log2_sigmoid(z):
    return (jnp.minimum(z, 0.0) - jnp.log1p(jnp.exp(-jnp.abs(z)))) * LOG2E


def _tri(n):
    row = lax.broadcasted_iota(jnp.int32, (n, n), 0)
    col = lax.broadcasted_iota(jnp.int32, (n, n), 1)
    return jnp.where(col <= row, 1.0, 0.0).astype(BF16)


def _cumsum_rows(tri, x):
    return sum(jnp.dot(tri, part, preferred_element_type=F32) for part in _split3(x))


def _place(x, place_ref):
    return sum(jnp.dot(part, place_ref[k], preferred_element_type=F32)
               for k, part in enumerate(_split3(x))).astype(BF16)


def _decay_kernel(sm_ref, bias_ref, place_ref, ex_ref, crow_ref, beta_ref, *, tile):
    n_tiles = sm_ref.shape[0] // tile
    tri = _tri(tile)
    eye = jnp.where(lax.broadcasted_iota(jnp.int32, (LANE, LANE), 0)
                    == lax.broadcasted_iota(jnp.int32, (LANE, LANE), 1), 1.0, 0.0).astype(BF16)
    carry = jnp.zeros((1, LANE), F32)
    withins = []
    for j in range(0, n_tiles, 2):
        pair = [_log2_sigmoid(sm_ref[(j + e) * tile:(j + e + 1) * tile, :] + bias_ref[...]) for e in range(2)]
        both = _cumsum_rows(tri, jnp.concatenate(pair, axis=1))
        withins += [both[:, 0:LANE], both[:, LANE:]]
    for j in range(n_tiles):
        rows = slice(j * tile, (j + 1) * tile)
        within = withins[j]
        ex_ref[rows, :] = _place(-within, place_ref)
        cum = within + carry
        cum_t = sum(lax.dot_general(eye, part, NT_DIMS, preferred_element_type=F32) for part in _split3(cum))
        crow_ref[:, j, :] = cum_t[0:HEADS, :]
        beta_ref[j:j + 1, :] = carry
        carry = cum[tile - 1:tile, :]


def _decay_meta_kernel(sm_ref, bias_ref, place_ref, ex_ref):
    n = sm_ref.shape[0]
    cum = _cumsum_rows(_tri(n), _log2_sigmoid(sm_ref[...] + bias_ref[...]))
    ex_ref[...] = _place(cum[n - 1:n, :] - cum, place_ref)


def _decay_prep(fl, fl_meta, bias_row, place, batch, seq):
    width = place.shape[2]
    ex, crow, beta = pl.pallas_call(
        functools.partial(_decay_kernel, tile=ATT_TILE),
        grid=(batch,),
        in_specs=[pl.BlockSpec((seq, LANE), lambda b: (b, 0)), _const_spec(bias_row.shape), _const_spec(place.shape)],
        out_specs=[pl.BlockSpec((seq, width), lambda b: (b, 0)),
                   pl.BlockSpec((None, HEADS, seq // ATT_TILE, ATT_TILE), lambda b: (b, 0, 0, 0)),
                   pl.BlockSpec((None, seq // ATT_TILE, LANE), lambda b: (b, 0, 0))],
        out_shape=[jax.ShapeDtypeStruct((batch * seq, width), BF16),
                   jax.ShapeDtypeStruct((batch, HEADS, seq // ATT_TILE, ATT_TILE), F32),
                   jax.ShapeDtypeStruct((batch, seq // ATT_TILE, LANE), F32)],
        compiler_params=pltpu.CompilerParams(dimension_semantics=("parallel",), vmem_limit_bytes=VMEM_LIMIT),
        name="decay_prep",
    )(fl, bias_row, place)
    ex_meta = pl.pallas_call(
        _decay_meta_kernel,
        grid=(1,),
        in_specs=[pl.BlockSpec((N_META, LANE), lambda i: (0, 0)), _const_spec(bias_row.shape),
                  _const_spec(place.shape)],
        out_specs=pl.BlockSpec((N_META, width), lambda i: (0, 0)),
        out_shape=jax.ShapeDtypeStruct((N_META, width), BF16),
        name="decay_prep_meta",
    )(fl_meta, bias_row, place)
    return ex, crow, beta, ex_meta


class _Chain(NamedTuple):
    q: jax.Array
    k_meta: jax.Array
    vt_meta: jax.Array
    k_tile: Callable
    vt_tile: Callable
    off_meta: Any
    off_tile: Any
    sub: int


def _scores(q_h, k_t):
    return lax.dot_general(k_t, q_h, NT_DIMS, preferred_element_type=F32)


def _col_max(s):
    return jnp.max(s, axis=0, keepdims=True)


def _pv(vt_t, p):
    v_aug = jnp.concatenate([vt_t, jnp.ones((L_ROWS, vt_t.shape[1]), BF16)], axis=0)
    return jnp.dot(v_aug, p, preferred_element_type=F32)


def _shifted(m_tile, off):
    return m_tile if off is None else m_tile + off


def _probs(s, m_new, off):
    return jnp.exp2(s - (m_new if off is None else m_new - off)).astype(BF16)


def _causal(shape, first_query):
    return (lax.broadcasted_iota(jnp.int32, shape, 0)
            <= lax.broadcasted_iota(jnp.int32, shape, 1) + first_query)


def _attend(step, chains, s_ref, p_ref, acc_ref, o_ref):
    n = len(chains)
    n_common = Q_SUB * step
    assert all(ch.sub in (0, 1) for ch in chains)

    def off_of(ch, t):
        return None if ch.off_tile is None else ch.off_tile(t)

    s_meta, ms, alphas, tile_max = [], [], [], []
    for c, ch in enumerate(chains):
        s = _scores(ch.q, jnp.concatenate([ch.k_tile(0), ch.k_meta], axis=0))
        s_meta.append(s[ATT_TILE:, :])
        s = s[0:ATT_TILE, :]
        m = jnp.full((1, ATT_TILE), NEG_INF, F32)
        acc_ref[c] = jnp.zeros(acc_ref.shape[1:], F32)
        if ch.sub == 0:
            p_ref[c] = jnp.zeros(p_ref.shape[1:], BF16)
        else:
            m = _shifted(_col_max(s), off_of(ch, 0))
            p_ref[c] = _probs(s, m, off_of(ch, 0))
            s = _scores(ch.q, ch.k_tile(1))
        s_ref[c] = s
        tile_max.append(_col_max(s))
        ms.append(m)
        alphas.append(jnp.ones((1, ATT_TILE), F32))

    def pending_pv(c, ch, j, alpha):
        return alpha * acc_ref[c] + _pv(ch.vt_tile(jnp.maximum(j - 1 + ch.sub, 0)), p_ref[c])

    def body(j, carry):
        ms, alphas, tile_max = carry
        cur = [s_ref[c] for c in range(n)]
        for c, ch in enumerate(chains):
            acc_ref[c] = pending_pv(c, ch, j, alphas[c])
        new_ms, new_alphas, probs = [], [], []
        for c, ch in enumerate(chains):
            off = off_of(ch, j + ch.sub)
            m_new = jnp.maximum(ms[c], _shifted(tile_max[c], off))
            new_alphas.append(jnp.exp2(ms[c] - m_new))
            probs.append(_probs(cur[c], m_new, off))
            new_ms.append(m_new)
        nxt = [_scores(ch.q, ch.k_tile(j + 1 + ch.sub)) for ch in chains]
        for c in range(n):
            p_ref[c] = probs[c]
            s_ref[c] = nxt[c]
        return tuple(new_ms), tuple(new_alphas), tuple(_col_max(s) for s in nxt)

    ms, alphas, _ = lax.fori_loop(0, n_common, body, (tuple(ms), tuple(alphas), tuple(tile_max)))

    outs = {}
    for c, ch in enumerate(chains):
        acc = pending_pv(c, ch, n_common, alphas[c])
        t = n_common + ch.sub
        s = jnp.where(_causal((ATT_TILE, ATT_TILE), 0), s_ref[c], NEG_INF)
        off = off_of(ch, t)
        m_new = jnp.maximum(jnp.maximum(ms[c], _shifted(_col_max(s), off)),
                            _shifted(_col_max(s_meta[c]), ch.off_meta))
        acc = (jnp.exp2(ms[c] - m_new) * acc + _pv(ch.vt_tile(t), _probs(s, m_new, off))
               + _pv(ch.vt_meta, _probs(s_meta[c], m_new, ch.off_meta)))
        outs.setdefault(ch.sub, []).append(acc[0:FOX_DIM, :] / acc[FOX_DIM:FOX_DIM + 1, :])
    for u, out_t in outs.items():
        o_ref[_key_rows(Q_SUB * step + u), :] = jnp.concatenate(out_t, axis=0).T.astype(BF16)


def _key_rows(j):
    return pl.ds(pl.multiple_of(j * ATT_TILE, ATT_TILE), ATT_TILE)


def _query_steps(n_keys, one_step):
    def body(step, carry):
        one_step(step)
        return carry
    lax.fori_loop(0, n_keys // (Q_SUB * ATT_TILE), body, 0)


def _mla_attn_kernel(q_ref, k_ref, vt_ref, km_ref, vmt_ref, o_ref, *scratch):
    def one_step(step):
        chains = []
        for u in range(Q_SUB):
            for head in range(2 * STEP_PAIRS):
                lanes = slice(head * LANE, (head + 1) * LANE)
                pair = head // 2
                rows64 = slice((head % 2) * MLA_V, (head % 2 + 1) * MLA_V)
                chains.append(_Chain(
                    q=q_ref[_key_rows(Q_SUB * step + u), lanes], k_meta=km_ref[:, lanes],
                    vt_meta=vmt_ref[pair, rows64, :],
                    k_tile=lambda j, lanes=lanes: k_ref[_key_rows(j), lanes],
                    vt_tile=lambda j, pair=pair, rows64=rows64: vt_ref[pair, j, rows64, :],
                    off_meta=None, off_tile=None, sub=u))
        _attend(step, chains, *scratch, o_ref)

    _query_steps(k_ref.shape[0], one_step)


def _fox_attn_kernel(beta_ref, q_ref, k_ref, ex_ref, spread_ref, vt_ref, km_ref, exm_ref, vmt_ref, crow_ref, o_ref,
                     kaug_ref, *scratch):
    b = pl.program_id(0)
    first_pair = pl.program_id(1) * STEP_PAIRS
    n_tiles = k_ref.shape[0] // ATT_TILE
    lane = lax.broadcasted_iota(jnp.int32, (1, LANE), 1)
    in_a = lane < FOX_DIM

    def spread(packed, pair):
        return jnp.dot(packed, spread_ref[pair], preferred_element_type=F32).astype(BF16)

    def pair_lanes(pair):
        return slice(pair * LANE, (pair + 1) * LANE)

    spread_all = jnp.concatenate([spread_ref[pair] for pair in range(STEP_PAIRS)], axis=1)
    for c in range(n_tiles):
        rows = slice(c * ATT_TILE, (c + 1) * ATT_TILE)
        ee_all = jnp.dot(ex_ref[rows, :], spread_all, preferred_element_type=F32).astype(BF16)
        for pair in range(STEP_PAIRS):
            kk = k_ref[rows, pair_lanes(pair)]
            ee = ee_all[:, pair_lanes(pair)]
            kaug_ref[2 * pair, rows, :] = jnp.where(in_a, kk, ee)
            kaug_ref[2 * pair + 1, rows, :] = jnp.where(in_a, ee, kk)

    km = [km_ref[:, pair_lanes(pair)] for pair in range(STEP_PAIRS)]
    exm = [spread(exm_ref[...], pair) for pair in range(STEP_PAIRS)]

    def one_step(step):
        chains = []
        for u in range(Q_SUB):
            tile = Q_SUB * step + u
            for pair in range(STEP_PAIRS):
                q = q_ref[_key_rows(tile), pair_lanes(pair)]
                for hh in range(2):
                    own = in_a if hh == 0 else jnp.logical_not(in_a)
                    base = FOX_DIM if hh == 0 else 0
                    ones = jnp.where((lane >= base) & (lane < base + 3), 1.0, 0.0).astype(BF16)
                    rows64 = slice(hh * FOX_DIM, (hh + 1) * FOX_DIM)
                    head = 2 * (first_pair + pair) + hh
                    cum_q = crow_ref[head, pl.ds(tile, 1), :]
                    beta_base = b * n_tiles * LANE + head
                    chains.append(_Chain(
                        q=jnp.where(own, q, ones), k_meta=jnp.where(own, km[pair], exm[pair]),
                        vt_meta=vmt_ref[pair, rows64, :],
                        k_tile=lambda j, slot=2 * pair + hh: kaug_ref[slot, _key_rows(j), :],
                        vt_tile=lambda j, pair=pair, rows64=rows64: vt_ref[pair, j, rows64, :],
                        off_meta=cum_q,
                        off_tile=lambda j, cum_q=cum_q, beta_base=beta_base: cum_q - beta_ref[beta_base + j * LANE],
                        sub=u))
        _attend(step, chains, *scratch, o_ref)

    _query_steps(k_ref.shape[0], one_step)


def _attention_specs(batch, seq, width):
    n_tiles = seq // ATT_TILE
    pairs = STEP_PAIRS
    q_spec = pl.BlockSpec((None, seq, pairs * width), lambda b, hp, *_: (b, 0, hp))
    k_spec = pl.BlockSpec((None, seq, pairs * width), lambda b, hp, *_: (b, 0, hp))
    vt_spec = pl.BlockSpec((None, pairs, n_tiles, LANE, ATT_TILE), lambda b, hp, *_: (b, hp, 0, 0, 0))
    km_spec = pl.BlockSpec((N_META, pairs * width), lambda b, hp, *_: (0, hp))
    vmt_spec = pl.BlockSpec((pairs, LANE, N_META), lambda b, hp, *_: (hp, 0, 0))
    o_spec = pl.BlockSpec((None, seq, pairs * LANE), lambda b, hp, *_: (b, 0, hp))
    return q_spec, k_spec, vt_spec, km_spec, vmt_spec, o_spec


def _pair_scratch():
    n = 2 * STEP_PAIRS * Q_SUB
    return [pltpu.VMEM((n, ATT_TILE, ATT_TILE), F32), pltpu.VMEM((n, ATT_TILE, ATT_TILE), BF16),
            pltpu.VMEM((n, MLA_V + L_ROWS, ATT_TILE), F32)]


def _mla_attention(q, k, vt, km, vmt):
    batch, seq, _ = q.shape
    q_spec, k_spec, vt_spec, km_spec, vmt_spec, o_spec = _attention_specs(batch, seq, 2 * LANE)
    return pl.pallas_call(
        _mla_attn_kernel,
        grid=(batch, HEAD_PAIRS // STEP_PAIRS),
        in_specs=[q_spec, k_spec, vt_spec, km_spec, vmt_spec],
        out_specs=o_spec,
        out_shape=jax.ShapeDtypeStruct((batch, seq, HEADS * MLA_V), BF16),
        scratch_shapes=_pair_scratch(),
        compiler_params=pltpu.CompilerParams(dimension_semantics=("parallel", "parallel"),
                                             vmem_limit_bytes=VMEM_LIMIT),
        name="mla_attention",
    )(q, k, vt, km, vmt)


def _fox_attention(beta, q, k, ex, spread, vt, km, exm, vmt, crow):
    batch, seq, _ = q.shape
    q_spec, k_spec, vt_spec, km_spec, vmt_spec, o_spec = _attention_specs(batch, seq, LANE)
    crow_spec = pl.BlockSpec((None, HEADS, seq // ATT_TILE, ATT_TILE), lambda b, hp, *_: (b, 0, 0, 0))
    ex_spec = pl.BlockSpec((None, seq, LANE), lambda b, hp, *_: (b, 0, 0))
    exm_spec = pl.BlockSpec((N_META, LANE), lambda b, hp, *_: (0, 0))
    spread_spec = pl.BlockSpec((STEP_PAIRS, LANE, LANE), lambda b, hp, *_: (hp, 0, 0))
    grid_spec = pltpu.PrefetchScalarGridSpec(
        num_scalar_prefetch=1,
        grid=(batch, HEAD_PAIRS // STEP_PAIRS),
        in_specs=[q_spec, k_spec, ex_spec, spread_spec, vt_spec, km_spec, exm_spec, vmt_spec, crow_spec],
        out_specs=o_spec,
        scratch_shapes=[pltpu.VMEM((2 * STEP_PAIRS, seq, LANE), BF16)] + _pair_scratch(),
    )
    return pl.pallas_call(
        _fox_attn_kernel,
        grid_spec=grid_spec,
        out_shape=jax.ShapeDtypeStruct((batch, seq, HEADS * FOX_DIM), BF16),
        compiler_params=pltpu.CompilerParams(dimension_semantics=("parallel", "parallel"),
                                             vmem_limit_bytes=VMEM_LIMIT),
        name="fox_attention",
    )(beta, q, k, ex, spread, vt, km, exm, vmt, crow)


def _merge_kernel(om_ref, szm_ref, of_ref, szf_ref, ga_ref, gb_ref, x_ref, wa_ref, wb_ref, wo_ref, g_ref, out_ref):
    ya = jnp.dot(om_ref[...] * szm_ref[...], wa_ref[...], preferred_element_type=F32)
    yb = jnp.dot(of_ref[...] * szf_ref[...], wb_ref[...], preferred_element_type=F32)
    mixed_in = (ga_ref[...].astype(F32) * ya + gb_ref[...].astype(F32) * yb).astype(BF16)
    mixed = jnp.dot(mixed_in, wo_ref[...], preferred_element_type=F32)
    out_ref[...] = x_ref[...] + _rms(mixed, g_ref[...])


def _merge(om, szm, of, szf, ga, gb, x2d, wa, wb, wo, g):
    rows, d = x2d.shape
    row_spec = pl.BlockSpec((ROW_TILE, d), lambda i: (i, 0))
    return pl.pallas_call(
        _merge_kernel,
        grid=(rows // ROW_TILE,),
        in_specs=[row_spec] * 7 + [_const_spec(w.shape) for w in (wa, wb, wo, g)],
        out_specs=row_spec,
        out_shape=jax.ShapeDtypeStruct((rows, d), F32),
        compiler_params=pltpu.CompilerParams(dimension_semantics=("parallel",), vmem_limit_bytes=VMEM_LIMIT),
        name="merge",
    )(om, szm, of, szf, ga, gb, x2d, wa, wb, wo, g)


def _transpose_kernel(w_ref, o_ref):
    o_ref[...] = w_ref[...].T.astype(BF16)


def _transpose_bf16(w):
    k, n = w.shape
    cols = 2 * LANE
    return pl.pallas_call(
        _transpose_kernel,
        grid=(n // cols,),
        in_specs=[pl.BlockSpec((k, cols), lambda i: (0, i))],
        out_specs=pl.BlockSpec((cols, k), lambda i: (i, 0)),
        out_shape=jax.ShapeDtypeStruct((n, k), BF16),
        name="weight_transpose",
    )(w)


def _rotate_half_cols(w):
    half = w.shape[-1] // 2
    return jnp.concatenate([-w[..., half:], w[..., :half]], axis=-1)


def _pad_cols(w, before, total):
    return jnp.pad(w, ((0, 0), (before, total - before - w.shape[1])))


def _in_proj_weights(w_in):
    widths = (MLA_Q_RANK, MLA_KV_RANK, MLA_ROPE, HEADS * MLA_V, HEADS * FOX_DIM, HEADS * FOX_DIM,
              HEADS * FOX_DIM, HEADS, HEADS * FOX_DIM, w_in.shape[0], w_in.shape[0])
    assert sum(widths) == w_in.shape[1]
    bounds = np.cumsum((0,) + widths)
    wt = w_in.T
    w_cq, w_ckv, w_kpe, w_zm, w_fq, w_fk, w_fv, w_fl, w_zf, w_ga, w_gb = (
        wt[bounds[n]:bounds[n + 1]] for n in range(len(widths)))
    w_small = jnp.concatenate([
        w_cq, w_ckv,
        _pad_cols(w_fl.T, 0, KPE_LANE).T, w_kpe, _rotate_half_cols(w_kpe.T).T], axis=0)
    assert w_small.shape[0] == SMALL_W
    return [w.astype(BF16) for w in (w_small, w_zm, w_fq, w_fk, w_fv, w_zf, w_ga, w_gb)]


def _head_blocks(w, per_head, first, last, at, width, fn=lambda blk: blk):
    blocks = [_pad_cols(fn(w[:, h * per_head + first:h * per_head + last]), at, width) for h in range(HEADS)]
    return jnp.concatenate(blocks, axis=1)


def _mla_weights(w_uq, w_ukv):
    dq = MLA_NOPE + MLA_ROPE
    dkv = MLA_NOPE + MLA_V
    assert dq + MLA_ROPE == LANE
    wq = jnp.concatenate([w for h in range(HEADS) for w in (
        w_uq[:, h * dq:(h + 1) * dq], _rotate_half_cols(w_uq[:, h * dq + MLA_NOPE:(h + 1) * dq]))], axis=1)
    wk_main = _head_blocks(w_ukv, dkv, 0, MLA_NOPE, 0, LANE)
    wv_t = _transpose_bf16(_head_blocks(w_ukv, dkv, MLA_NOPE, dkv, 0, MLA_V))
    return [w.astype(BF16) for w in (wq, wk_main)] + [wv_t]


def _rope_tables(first_pos, n_pos):
    half = MLA_ROPE // 2
    inv_freq = ROPE_THETA ** (-np.arange(half, dtype=np.float64) / half)
    ang = (first_pos + np.arange(n_pos, dtype=np.float64))[:, None] * inv_freq[None, :]
    cos = np.concatenate([np.cos(ang), np.cos(ang)], axis=1)
    sin = np.concatenate([np.sin(ang), np.sin(ang)], axis=1)
    pad = np.zeros((n_pos, LANE - MLA_NOPE - MLA_ROPE))
    ones = np.ones((n_pos, MLA_NOPE))
    zeros = np.zeros((n_pos, MLA_NOPE))
    cos_q = np.concatenate([ones, cos, pad], axis=1) * (MLA_SCALE * LOG2E)
    sin_q = np.concatenate([zeros, sin, pad], axis=1) * (MLA_SCALE * LOG2E)
    cos_k = np.concatenate([zeros, cos, pad], axis=1)
    sin_k = np.concatenate([zeros, sin, pad], axis=1)
    return [jnp.asarray(t, F32) for t in (cos_q, sin_q, cos_k, sin_k)]


def _placement():
    pack = np.zeros((3, LANE, LANE), np.float32)
    spread = np.zeros((HEAD_PAIRS, LANE, LANE), np.float32)
    for h in range(HEADS):
        base = FOX_DIM if h % 2 == 0 else 0
        for part in range(3):
            pack[part, h, HEADS * part + h] = 1.0
            spread[h // 2, HEADS * part + h, base + part] = 1.0
    return jnp.asarray(pack, BF16), jnp.asarray(spread, BF16)


def kernel(x, meta_tokens, pre_norm_g, w_in, fox_forget_b, mla_q_norm_g, mla_kv_norm_g, w_uq, w_ukv,
           w_br_mla, w_br_fox, w_out, post_norm_g):
    batch, seq, d = x.shape
    assert pre_norm_g.shape[0] == 1, "one layer supported"
    assert meta_tokens.shape[0] == N_META and seq % (Q_SUB * ATT_TILE) == 0 and ATT_TILE == ROW_TILE
    n_tiles = seq // ATT_TILE
    x2d = x.reshape(batch * seq, d)

    in_w = _in_proj_weights(w_in[0])
    mla_w = _mla_weights(w_uq[0], w_ukv[0])
    g_pre = pre_norm_g.astype(F32)
    gq = mla_q_norm_g.astype(F32)
    gkv = mla_kv_norm_g.astype(F32)
    bias_row = jnp.pad(fox_forget_b.astype(F32), ((0, 0), (0, LANE - HEADS)))

    vt_shape = (batch, HEAD_PAIRS, n_tiles, LANE, ATT_TILE)
    vt_spec = pl.BlockSpec((None, HEAD_PAIRS, None, LANE, ROW_TILE), lambda i: (i // n_tiles, 0, i % n_tiles, 0, 0))
    vt_meta_shape = (HEAD_PAIRS, LANE, N_META)
    vt_meta_spec = pl.BlockSpec(vt_meta_shape, lambda i: (0, 0, 0))

    fl, szm, fq, fk, fvt, szf, ga, gb, q, k, vt = _in_proj(
        x2d, g_pre, in_w, gq, gkv, mla_w, _rope_tables(N_META, seq), ROW_TILE, n_tiles, vt_shape, vt_spec)
    fl_m, _, _, fk_m, fvt_m, _, _, _, _, k_m, vt_m = _in_proj(
        meta_tokens.astype(F32), g_pre, in_w, gq, gkv, mla_w, _rope_tables(0, N_META), N_META, 1,
        vt_meta_shape, vt_meta_spec)
    pack, spread = _placement()
    ex, crow, beta, ex_m = _decay_prep(fl, fl_m, bias_row, pack, batch, seq)

    o_mla = _mla_attention(q.reshape(batch, seq, -1), k.reshape(batch, seq, -1), vt, k_m, vt_m)
    o_fox = _fox_attention(beta.reshape(-1),fq.reshape(batch, seq, -1), fk.reshape(batch, seq, -1),
                           ex.reshape(batch, seq, -1), spread, fvt, fk_m, ex_m, fvt_m,
                           crow)

    out = _merge(o_mla.reshape(batch * seq, -1), szm, o_fox.reshape(batch * seq, -1), szf, ga, gb, x2d,
                 w_br_mla[0].astype(BF16), w_br_fox[0].astype(BF16), w_out[0].astype(BF16),
                 post_norm_g.astype(F32))
    return out.reshape(batch, seq, d)
```
